```python
import math
import jax, jax.numpy as jnp
from jax import lax
import numpy as np

D_MODEL = 4096
BATCH = 1
SEQ = 8192
DEPTH = 1

RET_HEADS = 8
RET_HEAD_DIM = 256
RET_WIDTH = RET_HEADS * RET_HEAD_DIM
LRU_WIDTH = D_MODEL - RET_WIDTH
LRU_BLOCKS = 8
LRU_BLOCK_DIM = LRU_WIDTH // LRU_BLOCKS
CONV_WIDTH = 4
RG_C = 8.0
RET_CHUNK = 128
ROPE_BASE = 10000.0
IN_COLS = 4 * RET_WIDTH + 2 * LRU_WIDTH
MEM_LEN = 256
X_HEADS = 4
X_HEAD_DIM = D_MODEL // X_HEADS
N_GROUPS = 4
EXPERTS_PER_GROUP = 8
N_EXPERTS = N_GROUPS * EXPERTS_PER_GROUP
TOP_K_IN_GROUP = 2
D_EXPERT = D_MODEL // 4
MOE_BLOCK = 128
NORM_EPS = 1e-6
GN_EPS = 1e-5

kernel_name = 'hymba_style_retention_rglru_memxattn_hmoe'


def rms_norm(x, g):
    xf = x.astype(jnp.float32)
    y = xf * lax.rsqrt(jnp.mean(xf * xf, axis=-1, keepdims=True) + NORM_EPS)
    return (y * g.astype(jnp.float32)).astype(x.dtype)


def rope_tables(positions):
    inv_freq = ROPE_BASE ** (-jnp.arange(0, RET_HEAD_DIM, 2, dtype=jnp.float32) / RET_HEAD_DIM)
    ang = positions.astype(jnp.float32)[..., None] * inv_freq
    return jnp.cos(ang)[:, :, None, :], jnp.sin(ang)[:, :, None, :]


def apply_rope(t, cos, sin):
    t1, t2 = jnp.split(t, 2, axis=-1)
    return jnp.concatenate([t1 * cos - t2 * sin, t1 * sin + t2 * cos], axis=-1)


def chunkwise_retention(q, k, v):
    B, S, H, dk = q.shape
    dv = v.shape[-1]
    nc = S // RET_CHUNK
    C = RET_CHUNK

    def to_chunks(t):
        return t.reshape(B, nc, C, H, t.shape[-1]).transpose(1, 0, 3, 2, 4)

    qc, kc, vc = to_chunks(q), to_chunks(k), to_chunks(v)
    lg = jnp.log1p(-jnp.exp2(-5.0 - jnp.arange(H, dtype=jnp.float32)))
    idx = jnp.arange(C, dtype=jnp.float32)
    diff = idx[:, None] - idx[None, :]
    decay = jnp.where(diff >= 0, jnp.exp(lg[:, None, None] * jnp.maximum(diff, 0.0)), 0.0)
    xi = jnp.exp(lg[:, None] * (idx + 1.0))
    zeta = jnp.exp(lg[:, None] * (C - 1.0 - idx))
    chunk_decay = jnp.exp(lg * C)

    def step(R, qkv):
        qi, ki, vi = qkv
        inner = jnp.einsum('bhnd,bhmd->bhnm', qi, ki) * decay
        o = (jnp.einsum('bhnm,bhme->bhne', inner, vi)
             + jnp.einsum('bhnd,bhde->bhne', qi, R) * xi[None, :, :, None])
        R = (R * chunk_decay[None, :, None, None]
             + jnp.einsum('bhmd,bhme->bhde', ki * zeta[None, :, :, None], vi))
        return R, o

    R0 = jnp.zeros((B, H, dk, dv), jnp.float32)
    _, o = lax.scan(step, R0, (qc, kc, vc))
    return o.transpose(1, 0, 3, 2, 4).reshape(B, S, H, dv)


def retention_group(q, k, v, g, cos, sin, gn_g):
    B, S, _ = q.shape
    shp = (B, S, RET_HEADS, RET_HEAD_DIM)
    qf = apply_rope(q.astype(jnp.float32).reshape(shp), cos, sin)
    kf = apply_rope(k.astype(jnp.float32).reshape(shp), cos, sin) * (RET_HEAD_DIM ** -0.5)
    vf = v.astype(jnp.float32).reshape(shp)
    o = chunkwise_retention(qf, kf, vf)
    mu = jnp.mean(o, axis=-1, keepdims=True)
    var = jnp.mean(jnp.square(o - mu), axis=-1, keepdims=True)
    o = (o - mu) * lax.rsqrt(var + GN_EPS) * gn_g.astype(jnp.float32).reshape(RET_HEADS, RET_HEAD_DIM)
    o = o.reshape(B, S, RET_WIDTH) * jax.nn.silu(g.astype(jnp.float32))
    return o.astype(q.dtype)


def rg_lru_group(xb, gb, conv_w, conv_b, w_a, b_a, w_i, b_i, lam, out_g):
    B, S, C = xb.shape
    xc = lax.conv_general_dilated(
        xb, conv_w.astype(xb.dtype)[:, None, :], window_strides=(1,),
        padding=[(CONV_WIDTH - 1, 0)], dimension_numbers=('NWC', 'WIO', 'NWC'),
        feature_group_count=C) + conv_b.astype(xb.dtype)
    xg = xc.astype(jnp.float32).reshape(B, S, LRU_BLOCKS, LRU_BLOCK_DIM)
    r = jax.nn.sigmoid(jnp.einsum('bsnc,ncd->bsnd', xg, w_a.astype(jnp.float32))
                       + b_a.astype(jnp.float32).reshape(LRU_BLOCKS, LRU_BLOCK_DIM))
    i = jax.nn.sigmoid(jnp.einsum('bsnc,ncd->bsnd', xg, w_i.astype(jnp.float32))
                       + b_i.astype(jnp.float32).reshape(LRU_BLOCKS, LRU_BLOCK_DIM))
    log_a = (-RG_C * r.reshape(B, S, C)) * jax.nn.softplus(-lam.astype(jnp.float32))
    a = jnp.exp(log_a)
    bterm = jnp.sqrt(-jnp.expm1(2.0 * log_a)) * (i * xg).reshape(B, S, C)

    def combine(left, right):
        a1, b1 = left
        a2, b2 = right
        return a1 * a2, a2 * b1 + b2

    _, h = lax.associative_scan(combine, (a, bterm), axis=1)
    y = h * jax.nn.gelu(gb.astype(jnp.float32), approximate=True)
    return rms_norm(y, out_g).astype(xb.dtype)


def memory_cross_attention(h, memn, wq, wk, wv, wo):
    B, S, D = h.shape
    M = memn.shape[1]
    q = (h @ wq).reshape(B, S, X_HEADS, X_HEAD_DIM)
    k = (memn @ wk).reshape(B, M, X_HEADS, X_HEAD_DIM)
    v = (memn @ wv).reshape(B, M, X_HEADS, X_HEAD_DIM)
    s = jnp.einsum('bshd,bmhd->bhsm', q, k).astype(jnp.float32) * (X_HEAD_DIM ** -0.5)
    p = jax.nn.softmax(s, axis=-1).astype(v.dtype)
    o = jnp.einsum('bhsm,bmhd->bshd', p, v).reshape(B, S, D)
    return o @ wo


def hierarchical_moe(h, wg_r, bg_r, we_r, be_r, w_gate, w_up, w_down):
    B, S, D = h.shape
    T = B * S
    N = T * TOP_K_IN_GROUP
    ht = h.reshape(T, D)
    g_prob = jax.nn.softmax((ht @ wg_r).astype(jnp.float32) + bg_r.astype(jnp.float32), axis=-1)
    g_val, g_idx = lax.top_k(g_prob, 1)
    e_logits = ((ht @ we_r).astype(jnp.float32) + be_r.astype(jnp.float32)).reshape(T, N_GROUPS, EXPERTS_PER_GROUP)
    e_sel = jnp.take_along_axis(e_logits, g_idx[:, :, None], axis=1)[:, 0]
    top_logit, top_local = lax.top_k(e_sel, TOP_K_IN_GROUP)
    top_w = jax.nn.softmax(top_logit, axis=-1) * g_val
    expert_id = (g_idx * EXPERTS_PER_GROUP + top_local).reshape(N)
    weight = top_w.reshape(N)
    token_id = jnp.repeat(jnp.arange(T, dtype=jnp.int32), TOP_K_IN_GROUP)
    order = jnp.argsort(expert_id)
    se, stok, sw = expert_id[order], token_id[order], weight[order]
    counts = jnp.bincount(expert_id, length=N_EXPERTS)
    starts = jnp.cumsum(counts) - counts
    padded = (counts + MOE_BLOCK - 1) // MOE_BLOCK * MOE_BLOCK
    pends = jnp.cumsum(padded)
    pstarts = pends - padded
    dest = pstarts[se] + jnp.arange(N, dtype=jnp.int32) - starts[se]
    n_blocks = (N + MOE_BLOCK - 1) // MOE_BLOCK + N_EXPERTS
    R = n_blocks * MOE_BLOCK
    row_tok = jnp.zeros((R,), jnp.int32).at[dest].set(stok)
    row_w = jnp.zeros((R,), jnp.float32).at[dest].set(sw)
    block_start = jnp.arange(n_blocks, dtype=jnp.int32) * MOE_BLOCK
    block_expert = jnp.minimum(jnp.searchsorted(pends, block_start, side='right'), N_EXPERTS - 1)

    def run_block(args):
        tok, wrow, e = args
        xb = ht[tok]
        y = (jax.nn.silu(xb @ w_gate[e]) * (xb @ w_up[e])) @ w_down[e]
        return y * wrow[:, None].astype(y.dtype)

    y = lax.map(run_block, (row_tok.reshape(n_blocks, MOE_BLOCK),
                            row_w.reshape(n_blocks, MOE_BLOCK), block_expert))
    out = jnp.zeros((T, D), h.dtype).at[row_tok].add(y.reshape(R, D).astype(h.dtype))
    return out.reshape(B, S, D)


def setup_inputs(seed: int = 0) -> dict:
    key = jax.random.key(seed)
    ks = jax.random.split(key, 32)
    f32 = jnp.float32
    L = DEPTH

    def nrm(k, shape, fan_in):
        return jax.random.normal(k, shape, f32) * (fan_in ** -0.5)

    def gain(k, shape):
        return 1.0 + 0.02 * jax.random.normal(k, shape, f32)

    def bias(k, shape):
        return 0.01 * jax.random.normal(k, shape, f32)

    a0 = jax.random.uniform(ks[12], (L, LRU_WIDTH), f32, 0.9, 0.999)
    s0 = a0 ** (1.0 / RG_C)
    lru_lambda = jnp.log(s0) - jnp.log1p(-s0)
    return {
        'x': jax.random.normal(ks[0], (BATCH, SEQ, D_MODEL), f32),
        'mem': jax.random.normal(ks[1], (BATCH, MEM_LEN, D_MODEL), f32),
        'positions': jnp.broadcast_to(jnp.arange(SEQ, dtype=jnp.int32), (BATCH, SEQ)),
        'mix_norm_g': gain(ks[2], (L, D_MODEL)),
        'w_in': nrm(ks[3], (L, D_MODEL, IN_COLS), D_MODEL),
        'ret_norm_g': gain(ks[4], (L, RET_WIDTH)),
        'lru_conv_w': nrm(ks[5], (L, CONV_WIDTH, LRU_WIDTH), CONV_WIDTH),
        'lru_conv_b': bias(ks[6], (L, LRU_WIDTH)),
        'lru_w_a': nrm(ks[7], (L, LRU_BLOCKS, LRU_BLOCK_DIM, LRU_BLOCK_DIM), LRU_BLOCK_DIM),
        'lru_b_a': bias(ks[8], (L, LRU_WIDTH)),
        'lru_w_i': nrm(ks[9], (L, LRU_BLOCKS, LRU_BLOCK_DIM, LRU_BLOCK_DIM), LRU_BLOCK_DIM),
        'lru_b_i': bias(ks[10], (L, LRU_WIDTH)),
        'lru_lambda': lru_lambda,
        'lru_norm_g': gain(ks[11], (L, LRU_WIDTH)),
        'w_out': nrm(ks[13], (L, D_MODEL, D_MODEL), D_MODEL),
        'xattn_norm_g': gain(ks[14], (L, D_MODEL)),
        'mem_norm_g': gain(ks[15], (L, D_MODEL)),
        'xattn_wq': nrm(ks[16], (L, D_MODEL, D_MODEL), D_MODEL),
        'xattn_wk': nrm(ks[17], (L, D_MODEL, D_MODEL), D_MODEL),
        'xattn_wv': nrm(ks[18], (L, D_MODEL, D_MODEL), D_MODEL),
        'xattn_wo': nrm(ks[19], (L, D_MODEL, D_MODEL), D_MODEL),
        'moe_norm_g': gain(ks[20], (L, D_MODEL)),
        'router_group_w': nrm(ks[21], (L, D_MODEL, N_GROUPS), D_MODEL),
        'router_group_b': bias(ks[22], (L, N_GROUPS)),
        'router_expert_w': nrm(ks[23], (L, D_MODEL, N_EXPERTS), D_MODEL),
        'router_expert_b': bias(ks[24], (L, N_EXPERTS)),
        'expert_w_gate': nrm(ks[25], (L, N_EXPERTS, D_MODEL, D_EXPERT), D_MODEL),
        'expert_w_up': nrm(ks[26], (L, N_EXPERTS, D_MODEL, D_EXPERT), D_MODEL),
        'expert_w_down': nrm(ks[27], (L, N_EXPERTS, D_EXPERT, D_MODEL), D_EXPERT),
        'final_norm_g': gain(ks[28], (D_MODEL,)),
    }


def reference(x, mem, positions, mix_norm_g, w_in, ret_norm_g, lru_conv_w, lru_conv_b,
              lru_w_a, lru_b_a, lru_w_i, lru_b_i, lru_lambda, lru_norm_g, w_out,
              xattn_norm_g, mem_norm_g, xattn_wq, xattn_wk, xattn_wv, xattn_wo,
              moe_norm_g, router_group_w, router_group_b, router_expert_w, router_expert_b,
              expert_w_gate, expert_w_up, expert_w_down, final_norm_g):
    cos, sin = rope_tables(positions)
    splits = [RET_WIDTH, 2 * RET_WIDTH, 3 * RET_WIDTH, 4 * RET_WIDTH, 4 * RET_WIDTH + LRU_WIDTH]
    for l in range(DEPTH):
        h = rms_norm(x, mix_norm_g[l])
        proj = h @ w_in[l]
        q, k, v, g, xb, gb = jnp.split(proj, splits, axis=-1)
        ret = retention_group(q, k, v, g, cos, sin, ret_norm_g[l])
        lru = rg_lru_group(xb, gb, lru_conv_w[l], lru_conv_b[l], lru_w_a[l], lru_b_a[l],
                           lru_w_i[l], lru_b_i[l], lru_lambda[l], lru_norm_g[l])
        x = x + jnp.concatenate([ret, lru], axis=-1) @ w_out[l]
        h = rms_norm(x, xattn_norm_g[l])
        memn = rms_norm(mem, mem_norm_g[l])
        x = x + memory_cross_attention(h, memn, xattn_wq[l], xattn_wk[l], xattn_wv[l], xattn_wo[l])
        h = rms_norm(x, moe_norm_g[l])
        x = x + hierarchical_moe(h, router_group_w[l], router_group_b[l], router_expert_w[l],
                                 router_expert_b[l], expert_w_gate[l], expert_w_up[l], expert_w_down[l])
    return rms_norm(x, final_norm_g)
```

```python
import functools

import jax
import jax.numpy as jnp
from jax import lax
from jax.experimental import pallas as pl
from jax.experimental.pallas import tpu as pltpu

F32 = jnp.float32
BF16 = jnp.bfloat16

RET_HEADS = 8
RET_HEAD_DIM = 256
RET_CHUNK = 128
LRU_BLOCKS = 8
CONV_WIDTH = 4
RG_C = 8.0
ROPE_BASE = 10000.0
X_HEADS = 4
N_GROUPS = 4
EXPERTS_PER_GROUP = 8
N_EXPERTS = N_GROUPS * EXPERTS_PER_GROUP
NORM_EPS = 1e-6
GN_EPS = 1e-5

LANES = 128
SUBLANES = 8
VMEM_LIMIT = 56 * 1024 * 1024

NORM_ROWS = 512
MM_TM = 1024
MM_TN = 512
RET_ROWS = 1024
LRU_ROWS = 256
ATT_ROWS = 512
ROUTE_ROWS = 256
MOE_BLK = 256
MOE_TF = 512
MOE_TN = 1024
DISP_ROWS = 256
COMB_ROWS = 256
ROUTE_LANES = LANES
GROUP_LANE0 = 0
EXPERT_LANE0 = 8


def _params(sem):
    return pltpu.CompilerParams(dimension_semantics=sem, vmem_limit_bytes=VMEM_LIMIT)


def _normcast_kernel(x_ref, g_ref, o_ref):
    x = x_ref[...]
    ms = jnp.mean(x * x, axis=-1, keepdims=True)
    o_ref[...] = (x * lax.rsqrt(ms + NORM_EPS) * g_ref[...]).astype(o_ref.dtype)


def normcast(x, g, out_dtype, tm):
    m, d = x.shape
    tm = min(tm, m)
    return pl.pallas_call(
        _normcast_kernel,
        grid=(m // tm,),
        in_specs=[pl.BlockSpec((tm, d), lambda i: (i, 0)),
                  pl.BlockSpec((1, d), lambda i: (0, 0))],
        out_specs=pl.BlockSpec((tm, d), lambda i: (i, 0)),
        out_shape=jax.ShapeDtypeStruct((m, d), out_dtype),
        compiler_params=_params(("arbitrary",)),
        name="normcast",
    )(x, g.reshape(1, d))


def _cast_rows(src_ref, dst_ref, rows_per_iter=256):
    k = src_ref.shape[0]
    step = min(rows_per_iter, k)

    def body(i, carry):
        r0 = pl.multiple_of(i * step, step)
        dst_ref[pl.ds(r0, step), :] = src_ref[pl.ds(r0, step), :].astype(dst_ref.dtype)
        return carry

    lax.fori_loop(0, k // step, body, 0)


def _mm_kernel(*refs, n_a, has_res):
    a_refs = refs[:n_a]
    w_ref = refs[n_a]
    res_ref = refs[n_a + 1] if has_res else None
    o_ref = refs[n_a + 1 + int(has_res)]
    wbf_ref = refs[n_a + 2 + int(has_res)]

    @pl.when(pl.program_id(1) == 0)
    def _():
        _cast_rows(w_ref, wbf_ref)

    kp = a_refs[0].shape[1]
    acc = None
    for p, a_ref in enumerate(a_refs):
        d = jnp.dot(a_ref[...], wbf_ref[p * kp:(p + 1) * kp, :], preferred_element_type=F32)
        acc = d if acc is None else acc + d
    if has_res:
        acc = acc + res_ref[...]
    o_ref[...] = acc.astype(o_ref.dtype)


def matmul(a_parts, w, out_dtype, res=None, tm=MM_TM, tn=MM_TN):
    m, kp = a_parts[0].shape
    k, n = w.shape
    assert kp * len(a_parts) == k
    tm = min(tm, m)
    tn = min(tn, n)
    in_specs = [pl.BlockSpec((tm, kp), lambda j, i: (i, 0)) for _ in a_parts]
    in_specs.append(pl.BlockSpec((k, tn), lambda j, i: (0, j)))
    args = list(a_parts) + [w]
    if res is not None:
        in_specs.append(pl.BlockSpec((tm, tn), lambda j, i: (i, j)))
        args.append(res)
    return pl.pallas_call(
        functools.partial(_mm_kernel, n_a=len(a_parts), has_res=res is not None),
        grid=(n // tn, m // tm),
        in_specs=in_specs,
        out_specs=pl.BlockSpec((tm, tn), lambda j, i: (i, j)),
        out_shape=jax.ShapeDtypeStruct((m, n), out_dtype),
        scratch_shapes=[pltpu.VMEM((k, tn), BF16)],
        compiler_params=_params(("arbitrary", "arbitrary")),
        name="matmul",
    )(*args)


def _rope_kernel(pos_ref, invf_ref, cos_ref, sin_ref):
    ang = pos_ref[...] * invf_ref[...]
    cos_ref[...] = jnp.cos(ang)
    sin_ref[...] = jnp.sin(ang)


def rope_tables(pos_f, inv_freq, tm=512):
    s = pos_f.shape[0]
    hd = inv_freq.shape[0]
    tm = min(tm, s)
    return pl.pallas_call(
        _rope_kernel,
        grid=(s // tm,),
        in_specs=[pl.BlockSpec((tm, 1), lambda i: (i, 0)),
                  pl.BlockSpec((1, hd), lambda i: (0, 0))],
        out_specs=[pl.BlockSpec((tm, hd), lambda i: (i, 0))] * 2,
        out_shape=[jax.ShapeDtypeStruct((s, hd), F32)] * 2,
        compiler_params=_params(("arbitrary",)),
        name="rope_tables",
    )(pos_f.reshape(s, 1), inv_freq.reshape(1, hd))


def _ret_kernel(q_ref, k_ref, v_ref, g_ref, cos_ref, sin_ref, lg_ref, gn_ref, o_ref, r_ref,
                *, n_chunks):
    c = RET_CHUNK
    dk = RET_HEAD_DIM
    half = dk // 2

    @pl.when(pl.program_id(1) == 0)
    def _():
        r_ref[...] = jnp.zeros_like(r_ref)

    lg = lg_ref[...]
    row = lax.broadcasted_iota(jnp.int32, (c, c), 0).astype(F32)
    col = lax.broadcasted_iota(jnp.int32, (c, c), 1).astype(F32)
    diff = row - col
    decay = jnp.where(diff >= 0, jnp.exp(lg[:, :c] * jnp.maximum(diff, 0.0)), 0.0)
    rowk = lax.broadcasted_iota(jnp.int32, (c, dk), 0).astype(F32)
    xi = jnp.exp(lg * (rowk + 1.0))
    zeta = jnp.exp(lg * (c - 1.0 - rowk))
    chunk_decay = jnp.exp(lg * c)
    scale = dk ** -0.5
    gn = gn_ref[...]

    def rope(t, cos, sin):
        t1 = t[:, :half]
        t2 = t[:, half:]
        return jnp.concatenate([t1 * cos - t2 * sin, t1 * sin + t2 * cos], axis=-1)

    def body(j, carry):
        r0 = pl.multiple_of(j * c, c)
        rows = pl.ds(r0, c)
        cos = cos_ref[rows, :]
        sin = sin_ref[rows, :]
        qr = rope(q_ref[rows, :], cos, sin)
        kr = rope(k_ref[rows, :], cos, sin) * scale
        qb = qr.astype(BF16)
        kb = kr.astype(BF16)
        vb = v_ref[rows, :].astype(BF16)
        state = r_ref[...]
        inner = lax.dot_general(qb, kb, (((1,), (1,)), ((), ())), preferred_element_type=F32) * decay
        o = (jnp.dot(inner.astype(BF16), vb, preferred_element_type=F32)
             + jnp.dot(qb, state.astype(BF16), preferred_element_type=F32) * xi)
        kz = (kr * zeta).astype(BF16)
        r_ref[...] = state * chunk_decay + lax.dot_general(
            kz, vb, (((0,), (0,)), ((), ())), preferred_element_type=F32)
        mu = jnp.mean(o, axis=-1, keepdims=True)
        oc = o - mu
        var = jnp.mean(oc * oc, axis=-1, keepdims=True)
        on = oc * lax.rsqrt(var + GN_EPS) * gn
        g = g_ref[rows, :]
        o_ref[rows, :] = (on * (g * (1.0 / (1.0 + jnp.exp(-g))))).astype(o_ref.dtype)
        return carry

    lax.fori_loop(0, n_chunks, body, 0)


def retention(proj, cos, sin, lg_rows, gn_g, tr=RET_ROWS):
    s = proj.shape[0]
    dk = RET_HEAD_DIM
    h = RET_HEADS
    tr = min(tr, s)

    def col(base):
        return pl.BlockSpec((tr, dk), lambda hh, c, base=base: (c, base + hh))

    return pl.pallas_call(
        functools.partial(_ret_kernel, n_chunks=tr // RET_CHUNK),
        grid=(h, s // tr),
        in_specs=[col(0), col(h), col(2 * h), col(3 * h),
                  pl.BlockSpec((tr, dk // 2), lambda hh, c: (c, 0)),
                  pl.BlockSpec((tr, dk // 2), lambda hh, c: (c, 0)),
                  pl.BlockSpec((None, 1, dk), lambda hh, c: (hh, 0, 0)),
                  pl.BlockSpec((None, 1, dk), lambda hh, c: (hh, 0, 0))],
        out_specs=pl.BlockSpec((tr, dk), lambda hh, c: (c, hh)),
        out_shape=jax.ShapeDtypeStruct((s, h * dk), BF16),
        scratch_shapes=[pltpu.VMEM((dk, dk), F32)],
        compiler_params=_params(("arbitrary", "arbitrary")),
        name="retention",
    )(proj, proj, proj, proj, cos, sin, lg_rows, gn_g.reshape(h, 1, dk))


def _sigmoid(x):
    return 1.0 / (1.0 + jnp.exp(-x))


def _lru_kernel(xb_ref, gb_ref, cw_ref, cb_ref, wa_ref, ba_ref, wi_ref, bi_ref, lam_ref, og_ref,
                o_ref, xpad_ref, a_ref, b_ref, h_ref, wabf_ref, wibf_ref):
    tr, cdim = xb_ref.shape
    nb = wa_ref.shape[0]
    bd = cdim // nb

    @pl.when(pl.program_id(0) == 0)
    def _():
        xpad_ref[0:SUBLANES, :] = jnp.zeros((SUBLANES, cdim), F32)
        h_ref[...] = jnp.zeros_like(h_ref)
        wabf_ref[...] = wa_ref[...].astype(BF16)
        wibf_ref[...] = wi_ref[...].astype(BF16)

    xpad_ref[SUBLANES:, :] = xb_ref[...]
    xc = cb_ref[...] + cw_ref[0:1, :] * xpad_ref[pl.ds(SUBLANES - 3, tr), :]
    for j in range(1, CONV_WIDTH):
        xc = xc + cw_ref[j:j + 1, :] * xpad_ref[pl.ds(SUBLANES - 3 + j, tr), :]

    lam = lam_ref[...]
    sp = jnp.maximum(-lam, 0.0) + jnp.log1p(jnp.exp(-jnp.abs(lam)))

    for n in range(nb):
        cs = slice(n * bd, (n + 1) * bd)
        xg = xc[:, cs]
        xgb = xg.astype(BF16)
        r = _sigmoid(jnp.dot(xgb, wabf_ref[n], preferred_element_type=F32) + ba_ref[:, cs])
        ig = _sigmoid(jnp.dot(xgb, wibf_ref[n], preferred_element_type=F32) + bi_ref[:, cs])
        log_a = (-RG_C * r) * sp[:, cs]
        a = jnp.exp(log_a)
        a_ref[:, cs] = a
        b_ref[:, cs] = jnp.sqrt(-jnp.tanh(log_a) * (a * a + 1.0)) * (ig * xg)

    rowi = lax.broadcasted_iota(jnp.int32, (SUBLANES, cdim), 0)

    def body(gi, h):
        r0 = pl.multiple_of(gi * SUBLANES, SUBLANES)
        rows = pl.ds(r0, SUBLANES)
        a = a_ref[rows, :]
        b = b_ref[rows, :]
        d = 1
        while d < SUBLANES:
            keep = rowi >= d
            a_sh = pltpu.roll(a, d, 0)
            b_sh = pltpu.roll(b, d, 0)
            b = jnp.where(keep, a * b_sh + b, b)
            a = jnp.where(keep, a * a_sh, a)
            d *= 2
        hh = a * h + b
        b_ref[rows, :] = hh
        return hh[SUBLANES - 1:SUBLANES, :]

    h_ref[...] = lax.fori_loop(0, tr // SUBLANES, body, h_ref[...])

    gb = gb_ref[...]
    gelu = 0.5 * gb * (1.0 + jnp.tanh(0.7978845608028654 * (gb + 0.044715 * (gb * gb * gb))))
    y = b_ref[...] * gelu
    ms = jnp.mean(y * y, axis=-1, keepdims=True)
    o_ref[...] = (y * lax.rsqrt(ms + NORM_EPS) * og_ref[...]).astype(o_ref.dtype)
    xpad_ref[0:SUBLANES, :] = xb_ref[tr - SUBLANES:tr, :]


def rg_lru(proj, xb_block, gb_block, conv_w, conv_b, w_a, b_a, w_i, b_i, lam, out_g, tr=LRU_ROWS):
    s = proj.shape[0]
    cdim = conv_w.shape[1]
    nb, bd, _ = w_a.shape
    tr = min(tr, s)
    vec = pl.BlockSpec((1, cdim), lambda i: (0, 0))
    wspec = pl.BlockSpec((nb, bd, bd), lambda i: (0, 0, 0))
    return pl.pallas_call(
        _lru_kernel,
        grid=(s // tr,),
        in_specs=[pl.BlockSpec((tr, cdim), lambda i: (i, xb_block)),
                  pl.BlockSpec((tr, cdim), lambda i: (i, gb_block)),
                  pl.BlockSpec((CONV_WIDTH, cdim), lambda i: (0, 0)),
                  vec, wspec, vec, wspec, vec, vec, vec],
        out_specs=pl.BlockSpec((tr, cdim), lambda i: (i, 0)),
        out_shape=jax.ShapeDtypeStruct((s, cdim), BF16),
        scratch_shapes=[pltpu.VMEM((tr + SUBLANES, cdim), F32),
                        pltpu.VMEM((tr, cdim), F32),
                        pltpu.VMEM((tr, cdim), F32),
                        pltpu.VMEM((1, cdim), F32),
                        pltpu.VMEM((nb, bd, bd), BF16),
                        pltpu.VMEM((nb, bd, bd), BF16)],
        compiler_params=_params(("arbitrary",)),
        name="rg_lru",
    )(proj, proj, conv_w, conv_b.reshape(1, cdim), w_a, b_a.reshape(1, cdim), w_i,
      b_i.reshape(1, cdim), lam.reshape(1, cdim), out_g.reshape(1, cdim))


def _xattn_kernel(q_ref, k_ref, v_ref, o_ref):
    d = q_ref.shape[1]
    hd = d // X_HEADS
    scale = hd ** -0.5
    for h in range(X_HEADS):
        cs = slice(h * hd, (h + 1) * hd)
        s = lax.dot_general(q_ref[:, cs], k_ref[:, cs], (((1,), (1,)), ((), ())),
                            preferred_element_type=F32) * scale
        m = jnp.max(s, axis=-1, keepdims=True)
        e = jnp.exp(s - m)
        p = e / jnp.sum(e, axis=-1, keepdims=True)
        o_ref[:, cs] = jnp.dot(p.astype(BF16), v_ref[:, cs],
                               preferred_element_type=F32).astype(o_ref.dtype)


def xattn_core(q, k, v, tm=ATT_ROWS):
    s, d = q.shape
    mlen = k.shape[0]
    tm = min(tm, s)
    return pl.pallas_call(
        _xattn_kernel,
        grid=(s // tm,),
        in_specs=[pl.BlockSpec((tm, d), lambda i: (i, 0)),
                  pl.BlockSpec((mlen, d), lambda i: (0, 0)),
                  pl.BlockSpec((mlen, d), lambda i: (0, 0))],
        out_specs=pl.BlockSpec((tm, d), lambda i: (i, 0)),
        out_shape=jax.ShapeDtypeStruct((s, d), BF16),
        compiler_params=_params(("arbitrary",)),
        name="xattn_core",
    )(q, k, v)


def _router_kernel(x_ref, g_ref, wr_ref, br_ref, h_ref, route_ref, counts_ref, carry_ref):
    tm = x_ref.shape[0]

    @pl.when(pl.program_id(0) == 0)
    def _():
        carry_ref[...] = jnp.zeros_like(carry_ref)

    x = x_ref[...]
    ms = jnp.mean(x * x, axis=-1, keepdims=True)
    h = x * lax.rsqrt(ms + NORM_EPS) * g_ref[...]
    h_ref[...] = h
    logits = jnp.dot(h, wr_ref[...], preferred_element_type=F32,
                     precision=lax.Precision.HIGHEST) + br_ref[...]
    lane = lax.broadcasted_iota(jnp.int32, (tm, ROUTE_LANES), 1)
    neg = -jnp.inf
    big = ROUTE_LANES

    gmask = (lane >= GROUP_LANE0) & (lane < GROUP_LANE0 + N_GROUPS)
    gl = jnp.where(gmask, logits, neg)
    gmax = jnp.max(gl, axis=-1, keepdims=True)
    gsum = jnp.sum(jnp.where(gmask, jnp.exp(gl - gmax), 0.0), axis=-1, keepdims=True)
    g_val = 1.0 / gsum
    g_idx = jnp.min(jnp.where(gl == gmax, lane, big), axis=-1, keepdims=True) - GROUP_LANE0

    lo = EXPERT_LANE0 + g_idx * EXPERTS_PER_GROUP
    emask = (lane >= lo) & (lane < lo + EXPERTS_PER_GROUP)
    el = jnp.where(emask, logits, neg)
    t1 = jnp.max(el, axis=-1, keepdims=True)
    i1 = jnp.min(jnp.where(emask & (el == t1), lane, big), axis=-1, keepdims=True)
    emask2 = emask & (lane != i1)
    el2 = jnp.where(emask2, logits, neg)
    t2 = jnp.max(el2, axis=-1, keepdims=True)
    i2 = jnp.min(jnp.where(emask2 & (el2 == t2), lane, big), axis=-1, keepdims=True)
    dexp = jnp.exp(t2 - t1)
    w0 = g_val / (1.0 + dexp)
    w1 = g_val * dexp / (1.0 + dexp)

    sel1 = lane == i1
    sel2 = lane == i2
    onehot = jnp.where(sel1 | sel2, 1.0, 0.0)
    rr = lax.broadcasted_iota(jnp.int32, (tm, tm), 0)
    cc = lax.broadcasted_iota(jnp.int32, (tm, tm), 1)
    tri = jnp.where(cc < rr, 1.0, 0.0).astype(BF16)
    prefix = jnp.dot(tri, onehot.astype(BF16), preferred_element_type=F32) + carry_ref[...]
    rank0 = jnp.sum(jnp.where(sel1, prefix, 0.0), axis=-1, keepdims=True)
    rank1 = jnp.sum(jnp.where(sel2, prefix, 0.0), axis=-1, keepdims=True)
    total = carry_ref[...] + jnp.sum(onehot, axis=0, keepdims=True)
    carry_ref[...] = total
    counts_ref[...] = total

    e0 = (i1 - EXPERT_LANE0).astype(F32)
    e1 = (i2 - EXPERT_LANE0).astype(F32)
    route = jnp.where(lane == 0, e0, 0.0)
    route = jnp.where(lane == 1, e1, route)
    route = jnp.where(lane == 2, w0, route)
    route = jnp.where(lane == 3, w1, route)
    route = jnp.where(lane == 4, rank0, route)
    route = jnp.where(lane == 5, rank1, route)
    route_ref[...] = route


def router(x, g, wr, br, tm=ROUTE_ROWS):
    t, d = x.shape
    tm = min(tm, t)
    return pl.pallas_call(
        _router_kernel,
        grid=(t // tm,),
        in_specs=[pl.BlockSpec((tm, d), lambda i: (i, 0)),
                  pl.BlockSpec((1, d), lambda i: (0, 0)),
                  pl.BlockSpec((d, ROUTE_LANES), lambda i: (0, 0)),
                  pl.BlockSpec((1, ROUTE_LANES), lambda i: (0, 0))],
        out_specs=[pl.BlockSpec((tm, d), lambda i: (i, 0)),
                   pl.BlockSpec((tm, ROUTE_LANES), lambda i: (i, 0)),
                   pl.BlockSpec((1, ROUTE_LANES), lambda i: (0, 0))],
        out_shape=[jax.ShapeDtypeStruct((t, d), F32),
                   jax.ShapeDtypeStruct((t, ROUTE_LANES), F32),
                   jax.ShapeDtypeStruct((1, ROUTE_LANES), F32)],
        scratch_shapes=[pltpu.VMEM((1, ROUTE_LANES), F32)],
        compiler_params=_params(("arbitrary",)),
        name="router",
    )(x, g.reshape(1, d), wr, br)


def _dispatch_kernel(d0_ref, d1_ref, h_hbm, xs_in_hbm, xs_hbm, sem, *, tb):
    del xs_in_hbm
    base = pl.program_id(0) * tb

    def copies(i):
        t = base + i
        src = h_hbm.at[pl.ds(t, 1), :]
        return (pltpu.make_async_copy(src, xs_hbm.at[pl.ds(d0_ref[t], 1), :], sem),
                pltpu.make_async_copy(src, xs_hbm.at[pl.ds(d1_ref[t], 1), :], sem))

    def start(i, carry):
        c0, c1 = copies(i)
        c0.start()
        c1.start()
        return carry

    def wait(i, carry):
        c0, c1 = copies(i)
        c0.wait()
        c1.wait()
        return carry

    lax.fori_loop(0, tb, start, 0)
    lax.fori_loop(0, tb, wait, 0)


def dispatch(h, dest0, dest1, n_rows, tb=DISP_ROWS):
    t, d = h.shape
    tb = min(tb, t)
    xs0 = jnp.zeros((n_rows, d), h.dtype)
    grid_spec = pltpu.PrefetchScalarGridSpec(
        num_scalar_prefetch=2,
        grid=(t // tb,),
        in_specs=[pl.BlockSpec(memory_space=pl.ANY), pl.BlockSpec(memory_space=pl.ANY)],
        out_specs=pl.BlockSpec(memory_space=pl.ANY),
        scratch_shapes=[pltpu.SemaphoreType.DMA(())],
    )
    return pl.pallas_call(
        functools.partial(_dispatch_kernel, tb=tb),
        grid_spec=grid_spec,
        out_shape=jax.ShapeDtypeStruct((n_rows, d), h.dtype),
        input_output_aliases={3: 0},
        compiler_params=pltpu.CompilerParams(dimension_semantics=("arbitrary",),
                                             has_side_effects=True),
        name="moe_dispatch",
    )(dest0, dest1, h, xs0)


def _expert_changed(be_ref, i):
    prev = be_ref[jnp.maximum(i - 1, 0)]
    return (i == 0) | (be_ref[i] != prev)


def _moe_up_kernel(be_ref, nu_ref, xs_ref, wg_ref, wu_ref, act_ref, wgbf_ref, wubf_ref):
    i = pl.program_id(1)

    @pl.when(i < nu_ref[0])
    def _():
        @pl.when(_expert_changed(be_ref, i))
        def _():
            _cast_rows(wg_ref, wgbf_ref)
            _cast_rows(wu_ref, wubf_ref)

        x = xs_ref[...].astype(BF16)
        gate = jnp.dot(x, wgbf_ref[...], preferred_element_type=F32)
        up = jnp.dot(x, wubf_ref[...], preferred_element_type=F32)
        act_ref[...] = (gate * _sigmoid(gate) * up).astype(act_ref.dtype)

    @pl.when(i >= nu_ref[0])
    def _():
        act_ref[...] = jnp.zeros_like(act_ref)


def _moe_down_kernel(be_ref, nu_ref, act_ref, wd_ref, y_ref, wdbf_ref):
    i = pl.program_id(1)

    @pl.when(i < nu_ref[0])
    def _():
        @pl.when(_expert_changed(be_ref, i))
        def _():
            _cast_rows(wd_ref, wdbf_ref)

        y_ref[...] = jnp.dot(act_ref[...], wdbf_ref[...], preferred_element_type=F32)

    @pl.when(i >= nu_ref[0])
    def _():
        y_ref[...] = jnp.zeros_like(y_ref)


def moe_experts(xs, block_expert, n_used, w_gate, w_up, w_down, blk=MOE_BLK, tf=MOE_TF, tn=MOE_TN):
    r, d = xs.shape
    _, _, f = w_gate.shape
    n_blocks = r // blk
    tf = min(tf, f)
    tn = min(tn, d)
    up_spec = pltpu.PrefetchScalarGridSpec(
        num_scalar_prefetch=2,
        grid=(f // tf, n_blocks),
        in_specs=[pl.BlockSpec((blk, d), lambda j, i, be, nu: (i, 0)),
                  pl.BlockSpec((None, d, tf), lambda j, i, be, nu: (be[i], 0, j)),
                  pl.BlockSpec((None, d, tf), lambda j, i, be, nu: (be[i], 0, j))],
        out_specs=pl.BlockSpec((blk, tf), lambda j, i, be, nu: (i, j)),
        scratch_shapes=[pltpu.VMEM((d, tf), BF16), pltpu.VMEM((d, tf), BF16)],
    )
    act = pl.pallas_call(
        _moe_up_kernel,
        grid_spec=up_spec,
        out_shape=jax.ShapeDtypeStruct((r, f), BF16),
        compiler_params=_params(("arbitrary", "arbitrary")),
        name="moe_up",
    )(block_expert, n_used, xs, w_gate, w_up)
    down_spec = pltpu.PrefetchScalarGridSpec(
        num_scalar_prefetch=2,
        grid=(d // tn, n_blocks),
        in_specs=[pl.BlockSpec((blk, f), lambda j, i, be, nu: (i, 0)),
                  pl.BlockSpec((None, f, tn), lambda j, i, be, nu: (be[i], 0, j))],
        out_specs=pl.BlockSpec((blk, tn), lambda j, i, be, nu: (i, j)),
        scratch_shapes=[pltpu.VMEM((f, tn), BF16)],
    )
    return pl.pallas_call(
        _moe_down_kernel,
        grid_spec=down_spec,
        out_shape=jax.ShapeDtypeStruct((r, d), F32),
        compiler_params=_params(("arbitrary", "arbitrary")),
        name="moe_down",
    )(block_expert, n_used, act, w_down)


def _combine_kernel(d0_ref, d1_ref, x_ref, route_ref, g_ref, y_hbm, o_ref, ya_ref, yb_ref, sem,
                    *, tb, final_norm):
    base = pl.program_id(0) * tb

    def copies(i):
        t = base + i
        return (pltpu.make_async_copy(y_hbm.at[pl.ds(d0_ref[t], 1), :], ya_ref.at[pl.ds(i, 1), :], sem),
                pltpu.make_async_copy(y_hbm.at[pl.ds(d1_ref[t], 1), :], yb_ref.at[pl.ds(i, 1), :], sem))

    def start(i, carry):
        c0, c1 = copies(i)
        c0.start()
        c1.start()
        return carry

    def wait(i, carry):
        c0, c1 = copies(i)
        c0.wait()
        c1.wait()
        return carry

    lax.fori_loop(0, tb, start, 0)
    lax.fori_loop(0, tb, wait, 0)
    w0 = route_ref[:, 2:3]
    w1 = route_ref[:, 3:4]
    x = x_ref[...] + (ya_ref[...] * w0 + yb_ref[...] * w1)
    if final_norm:
        ms = jnp.mean(x * x, axis=-1, keepdims=True)
        x = x * lax.rsqrt(ms + NORM_EPS) * g_ref[...]
    o_ref[...] = x


def combine(x, route, y, dest0, dest1, g, final_norm, tb=COMB_ROWS):
    t, d = x.shape
    tb = min(tb, t)
    grid_spec = pltpu.PrefetchScalarGridSpec(
        num_scalar_prefetch=2,
        grid=(t // tb,),
        in_specs=[pl.BlockSpec((tb, d), lambda i, d0, d1: (i, 0)),
                  pl.BlockSpec((tb, ROUTE_LANES), lambda i, d0, d1: (i, 0)),
                  pl.BlockSpec((1, d), lambda i, d0, d1: (0, 0)),
                  pl.BlockSpec(memory_space=pl.ANY)],
        out_specs=pl.BlockSpec((tb, d), lambda i, d0, d1: (i, 0)),
        scratch_shapes=[pltpu.VMEM((tb, d), F32), pltpu.VMEM((tb, d), F32),
                        pltpu.SemaphoreType.DMA(())],
    )
    return pl.pallas_call(
        functools.partial(_combine_kernel, tb=tb, final_norm=final_norm),
        grid_spec=grid_spec,
        out_shape=jax.ShapeDtypeStruct((t, d), F32),
        compiler_params=_params(("arbitrary",)),
        name="moe_combine",
    )(dest0, dest1, x, route, g.reshape(1, d), y)


def _moe_layout(route, counts, blk):
    t = route.shape[0]
    e0 = route[:, 0].astype(jnp.int32)
    e1 = route[:, 1].astype(jnp.int32)
    rank0 = route[:, 4].astype(jnp.int32)
    rank1 = route[:, 5].astype(jnp.int32)
    cnt = counts[0, EXPERT_LANE0:EXPERT_LANE0 + N_EXPERTS].astype(jnp.int32)
    padded = (cnt + blk - 1) // blk * blk
    pends = jnp.cumsum(padded)
    pstarts = pends - padded
    dest0 = pstarts[e0] + rank0
    dest1 = pstarts[e1] + rank1
    n_blocks = (2 * t) // blk + N_EXPERTS
    block_start = jnp.arange(n_blocks, dtype=jnp.int32) * blk
    block_expert = jnp.minimum(
        jnp.sum((block_start[:, None] >= pends[None, :]).astype(jnp.int32), axis=1), N_EXPERTS - 1)
    n_used = (pends[-1] // blk).astype(jnp.int32).reshape(1)
    return dest0, dest1, block_expert, n_used, n_blocks


def kernel(x, mem, positions, mix_norm_g, w_in, ret_norm_g, lru_conv_w, lru_conv_b, lru_w_a, lru_b_a, lru_w_i, lru_b_i, lru_lambda, lru_norm_g, w_out, xattn_norm_g, mem_norm_g, xattn_wq, xattn_wk, xattn_wv, xattn_wo, moe_norm_g, router_group_w, router_group_b, router_expert_w, router_expert_b, expert_w_gate, expert_w_up, expert_w_down, final_norm_g):
    b, s, d = x.shape
    depth = w_in.shape[0]
    ret_width = RET_HEADS * RET_HEAD_DIM
    lru_width = lru_conv_w.shape[-1]
    assert ret_width == lru_width and ret_width + lru_width == d
    inv_freq = ROPE_BASE ** (-jnp.arange(0, RET_HEAD_DIM, 2, dtype=F32) / RET_HEAD_DIM)
    lg = jnp.log1p(-jnp.exp2(-5.0 - jnp.arange(RET_HEADS, dtype=F32)))
    lg_rows = jnp.broadcast_to(lg[:, None, None], (RET_HEADS, 1, RET_HEAD_DIM))
    blk = min(MOE_BLK, s)
    outs = []
    for bi in range(b):
        xcur = x[bi]
        cos, sin = rope_tables(positions[bi].astype(F32), inv_freq)
        for l in range(depth):
            h = normcast(xcur, mix_norm_g[l], BF16, NORM_ROWS)
            proj = matmul([h], w_in[l], F32)
            ret = retention(proj, cos, sin, lg_rows, ret_norm_g[l])
            lru = rg_lru(proj, 4 * ret_width // lru_width, 4 * ret_width // lru_width + 1,
                         lru_conv_w[l], lru_conv_b[l], lru_w_a[l], lru_b_a[l], lru_w_i[l],
                         lru_b_i[l], lru_lambda[l], lru_norm_g[l])
            xcur = matmul([ret, lru], w_out[l], F32, res=xcur)
            h = normcast(xcur, xattn_norm_g[l], BF16, NORM_ROWS)
            memn = normcast(mem[bi], mem_norm_g[l], BF16, NORM_ROWS)
            q = matmul([h], xattn_wq[l], BF16)
            kk = matmul([memn], xattn_wk[l], BF16)
            vv = matmul([memn], xattn_wv[l], BF16)
            att = xattn_core(q, kk, vv)
            xcur = matmul([att], xattn_wo[l], F32, res=xcur)
            wr = jnp.zeros((d, ROUTE_LANES), F32)
            wr = wr.at[:, GROUP_LANE0:GROUP_LANE0 + N_GROUPS].set(router_group_w[l])
            wr = wr.at[:, EXPERT_LANE0:EXPERT_LANE0 + N_EXPERTS].set(router_expert_w[l])
            br = jnp.zeros((1, ROUTE_LANES), F32)
            br = br.at[0, GROUP_LANE0:GROUP_LANE0 + N_GROUPS].set(router_group_b[l])
            br = br.at[0, EXPERT_LANE0:EXPERT_LANE0 + N_EXPERTS].set(router_expert_b[l])
            hn, route, counts = router(xcur, moe_norm_g[l], wr, br)
            dest0, dest1, block_expert, n_used, n_blocks = _moe_layout(route, counts, blk)
            xs = dispatch(hn, dest0, dest1, n_blocks * blk)
            y = moe_experts(xs, block_expert, n_used, expert_w_gate[l], expert_w_up[l],
                            expert_w_down[l], blk=blk)
            xcur = combine(xcur, route, y, dest0, dest1, final_norm_g, final_norm=l == depth - 1)
        outs.append(xcur)
    return outs[0][None] if b == 1 else jnp.stack(outs, axis=0)
```

```python
import functools

import jax
import jax.numpy as jnp
from jax import lax
from jax.experimental import pallas as pl
from jax.experimental.pallas import tpu as pltpu

F32 = jnp.float32
BF16 = jnp.bfloat16

RET_HEADS = 8
RET_HEAD_DIM = 256
RET_CHUNK = 128
LRU_BLOCKS = 8
CONV_WIDTH = 4
RG_C = 8.0
ROPE_BASE = 10000.0
X_HEADS = 4
N_GROUPS = 4
EXPERTS_PER_GROUP = 8
N_EXPERTS = N_GROUPS * EXPERTS_PER_GROUP
NORM_EPS = 1e-6
GN_EPS = 1e-5

LANES = 128
SUBLANES = 8
VMEM_LIMIT = 56 * 1024 * 1024

NORM_ROWS = 512
MM_TM = 1024
MM_TN = 512
RET_ROWS = 1024
LRU_ROWS = 256
ATT_ROWS = 256
ROUTE_ROWS = 256
MOE_BLK = 256
MOE_TF = 512
MOE_TN = 4096
XF_TILE = 1024
COMB_ROWS = 256
ROUTE_LANES = LANES
GROUP_LANE0 = 0
EXPERT_LANE0 = 8


def _params(sem):
    return pltpu.CompilerParams(dimension_semantics=sem, vmem_limit_bytes=VMEM_LIMIT)


def _normcast_kernel(x_ref, g_ref, o_ref):
    x = x_ref[...]
    ms = jnp.mean(x * x, axis=-1, keepdims=True)
    o_ref[...] = (x * lax.rsqrt(ms + NORM_EPS) * g_ref[...]).astype(o_ref.dtype)


def normcast(x, g, out_dtype, tm):
    m, d = x.shape
    tm = min(tm, m)
    return pl.pallas_call(
        _normcast_kernel,
        grid=(m // tm,),
        in_specs=[pl.BlockSpec((tm, d), lambda i: (i, 0)),
                  pl.BlockSpec((1, d), lambda i: (0, 0))],
        out_specs=pl.BlockSpec((tm, d), lambda i: (i, 0)),
        out_shape=jax.ShapeDtypeStruct((m, d), out_dtype),
        compiler_params=_params(("arbitrary",)),
        name="normcast",
    )(x, g.reshape(1, d))


def _cast_rows(src_ref, dst_ref, rows_per_iter=256):
    k = src_ref.shape[0]
    step = min(rows_per_iter, k)

    def body(i, carry):
        r0 = pl.multiple_of(i * step, step)
        dst_ref[pl.ds(r0, step), :] = src_ref[pl.ds(r0, step), :].astype(dst_ref.dtype)
        return carry

    lax.fori_loop(0, k // step, body, 0)


def _mm_kernel(*refs, n_a, has_res):
    a_refs = refs[:n_a]
    w_ref = refs[n_a]
    res_ref = refs[n_a + 1] if has_res else None
    o_ref = refs[n_a + 1 + int(has_res)]
    wbf_ref = refs[n_a + 2 + int(has_res)]

    @pl.when(pl.program_id(1) == 0)
    def _():
        _cast_rows(w_ref, wbf_ref)

    kp = a_refs[0].shape[1]
    acc = None
    for p, a_ref in enumerate(a_refs):
        d = jnp.dot(a_ref[...], wbf_ref[p * kp:(p + 1) * kp, :], preferred_element_type=F32)
        acc = d if acc is None else acc + d
    if has_res:
        acc = acc + res_ref[...]
    o_ref[...] = acc.astype(o_ref.dtype)


def matmul(a_parts, w, out_dtype, res=None, tm=MM_TM, tn=MM_TN):
    m, kp = a_parts[0].shape
    k, n = w.shape
    assert kp * len(a_parts) == k
    tm = min(tm, m)
    tn = min(tn, n)
    in_specs = [pl.BlockSpec((tm, kp), lambda j, i: (i, 0)) for _ in a_parts]
    in_specs.append(pl.BlockSpec((k, tn), lambda j, i: (0, j)))
    args = list(a_parts) + [w]
    if res is not None:
        in_specs.append(pl.BlockSpec((tm, tn), lambda j, i: (i, j)))
        args.append(res)
    return pl.pallas_call(
        functools.partial(_mm_kernel, n_a=len(a_parts), has_res=res is not None),
        grid=(n // tn, m // tm),
        in_specs=in_specs,
        out_specs=pl.BlockSpec((tm, tn), lambda j, i: (i, j)),
        out_shape=jax.ShapeDtypeStruct((m, n), out_dtype),
        scratch_shapes=[pltpu.VMEM((k, tn), BF16)],
        compiler_params=_params(("arbitrary", "arbitrary")),
        name="matmul",
    )(*args)


def _rope_kernel(pos_ref, invf_ref, cos_ref, sin_ref):
    ang = pos_ref[...] * invf_ref[...]
    cos_ref[...] = jnp.cos(ang)
    sin_ref[...] = jnp.sin(ang)


def rope_tables(pos_f, inv_freq, tm=512):
    s = pos_f.shape[0]
    hd = inv_freq.shape[0]
    tm = min(tm, s)
    return pl.pallas_call(
        _rope_kernel,
        grid=(s // tm,),
        in_specs=[pl.BlockSpec((tm, 1), lambda i: (i, 0)),
                  pl.BlockSpec((1, hd), lambda i: (0, 0))],
        out_specs=[pl.BlockSpec((tm, hd), lambda i: (i, 0))] * 2,
        out_shape=[jax.ShapeDtypeStruct((s, hd), F32)] * 2,
        compiler_params=_params(("arbitrary",)),
        name="rope_tables",
    )(pos_f.reshape(s, 1), inv_freq.reshape(1, hd))


def _ret_kernel(q_ref, k_ref, v_ref, g_ref, cos_ref, sin_ref, lg_ref, gn_ref, o_ref, r_ref,
                *, n_chunks):
    c = RET_CHUNK
    dk = RET_HEAD_DIM
    half = dk // 2

    @pl.when(pl.program_id(1) == 0)
    def _():
        r_ref[...] = jnp.zeros_like(r_ref)

    lg = lg_ref[...]
    row = lax.broadcasted_iota(jnp.int32, (c, c), 0).astype(F32)
    col = lax.broadcasted_iota(jnp.int32, (c, c), 1).astype(F32)
    diff = row - col
    decay = jnp.where(diff >= 0, jnp.exp(lg[:, :c] * jnp.maximum(diff, 0.0)), 0.0)
    rowk = lax.broadcasted_iota(jnp.int32, (c, dk), 0).astype(F32)
    xi = jnp.exp(lg * (rowk + 1.0))
    zeta = jnp.exp(lg * (c - 1.0 - rowk))
    chunk_decay = jnp.exp(lg * c)
    scale = dk ** -0.5
    gn = gn_ref[...]

    def rope(t, cos, sin):
        t1 = t[:, :half]
        t2 = t[:, half:]
        return jnp.concatenate([t1 * cos - t2 * sin, t1 * sin + t2 * cos], axis=-1)

    def body(j, carry):
        r0 = pl.multiple_of(j * c, c)
        rows = pl.ds(r0, c)
        cos = cos_ref[rows, :]
        sin = sin_ref[rows, :]
        qr = rope(q_ref[rows, :], cos, sin)
        kr = rope(k_ref[rows, :], cos, sin) * scale
        qb = qr.astype(BF16)
        kb = kr.astype(BF16)
        vb = v_ref[rows, :].astype(BF16)
        state = r_ref[...]
        inner = lax.dot_general(qb, kb, (((1,), (1,)), ((), ())), preferred_element_type=F32) * decay
        o = (jnp.dot(inner.astype(BF16), vb, preferred_element_type=F32)
             + jnp.dot(qb, state.astype(BF16), preferred_element_type=F32) * xi)
        kz = (kr * zeta).astype(BF16)
        r_ref[...] = state * chunk_decay + lax.dot_general(
            kz, vb, (((0,), (0,)), ((), ())), preferred_element_type=F32)
        mu = jnp.mean(o, axis=-1, keepdims=True)
        oc = o - mu
        var = jnp.mean(oc * oc, axis=-1, keepdims=True)
        on = oc * lax.rsqrt(var + GN_EPS) * gn
        g = g_ref[rows, :]
        o_ref[rows, :] = (on * (g * (1.0 / (1.0 + jnp.exp(-g))))).astype(o_ref.dtype)
        return carry

    lax.fori_loop(0, n_chunks, body, 0)


def retention(proj, cos, sin, lg_rows, gn_g, tr=RET_ROWS):
    s = proj.shape[0]
    dk = RET_HEAD_DIM
    h = RET_HEADS
    tr = min(tr, s)

    def col(base):
        return pl.BlockSpec((tr, dk), lambda hh, c, base=base: (c, base + hh))

    return pl.pallas_call(
        functools.partial(_ret_kernel, n_chunks=tr // RET_CHUNK),
        grid=(h, s // tr),
        in_specs=[col(0), col(h), col(2 * h), col(3 * h),
                  pl.BlockSpec((tr, dk // 2), lambda hh, c: (c, 0)),
                  pl.BlockSpec((tr, dk // 2), lambda hh, c: (c, 0)),
                  pl.BlockSpec((None, 1, dk), lambda hh, c: (hh, 0, 0)),
                  pl.BlockSpec((None, 1, dk), lambda hh, c: (hh, 0, 0))],
        out_specs=pl.BlockSpec((tr, dk), lambda hh, c: (c, hh)),
        out_shape=jax.ShapeDtypeStruct((s, h * dk), BF16),
        scratch_shapes=[pltpu.VMEM((dk, dk), F32)],
        compiler_params=_params(("arbitrary", "arbitrary")),
        name="retention",
    )(proj, proj, proj, proj, cos, sin, lg_rows, gn_g.reshape(h, 1, dk))


def _sigmoid(x):
    return 1.0 / (1.0 + jnp.exp(-x))


def _lru_kernel(xb_ref, gb_ref, cw_ref, cb_ref, wa_ref, ba_ref, wi_ref, bi_ref, lam_ref, og_ref,
                o_ref, xpad_ref, a_ref, b_ref, h_ref, wabf_ref, wibf_ref):
    tr, cdim = xb_ref.shape
    nb = wa_ref.shape[0]
    bd = cdim // nb

    @pl.when(pl.program_id(0) == 0)
    def _():
        xpad_ref[0:SUBLANES, :] = jnp.zeros((SUBLANES, cdim), F32)
        h_ref[...] = jnp.zeros_like(h_ref)
        wabf_ref[...] = wa_ref[...].astype(BF16)
        wibf_ref[...] = wi_ref[...].astype(BF16)

    xpad_ref[SUBLANES:, :] = xb_ref[...]
    xc = cb_ref[...] + cw_ref[0:1, :] * xpad_ref[pl.ds(SUBLANES - 3, tr), :]
    for j in range(1, CONV_WIDTH):
        xc = xc + cw_ref[j:j + 1, :] * xpad_ref[pl.ds(SUBLANES - 3 + j, tr), :]

    lam = lam_ref[...]
    sp = jnp.maximum(-lam, 0.0) + jnp.log1p(jnp.exp(-jnp.abs(lam)))

    for n in range(nb):
        cs = slice(n * bd, (n + 1) * bd)
        xg = xc[:, cs]
        xgb = xg.astype(BF16)
        r = _sigmoid(jnp.dot(xgb, wabf_ref[n], preferred_element_type=F32) + ba_ref[:, cs])
        ig = _sigmoid(jnp.dot(xgb, wibf_ref[n], preferred_element_type=F32) + bi_ref[:, cs])
        log_a = (-RG_C * r) * sp[:, cs]
        a = jnp.exp(log_a)
        a_ref[:, cs] = a
        b_ref[:, cs] = jnp.sqrt(-jnp.tanh(log_a) * (a * a + 1.0)) * (ig * xg)

    rowi = lax.broadcasted_iota(jnp.int32, (SUBLANES, cdim), 0)

    def body(gi, h):
        r0 = pl.multiple_of(gi * SUBLANES, SUBLANES)
        rows = pl.ds(r0, SUBLANES)
        a = a_ref[rows, :]
        b = b_ref[rows, :]
        d = 1
        while d < SUBLANES:
            keep = rowi >= d
            a_sh = pltpu.roll(a, d, 0)
            b_sh = pltpu.roll(b, d, 0)
            b = jnp.where(keep, a * b_sh + b, b)
            a = jnp.where(keep, a * a_sh, a)
            d *= 2
        hh = a * h + b
        b_ref[rows, :] = hh
        return hh[SUBLANES - 1:SUBLANES, :]

    h_ref[...] = lax.fori_loop(0, tr // SUBLANES, body, h_ref[...])

    gb = gb_ref[...]
    gelu = 0.5 * gb * (1.0 + jnp.tanh(0.7978845608028654 * (gb + 0.044715 * (gb * gb * gb))))
    y = b_ref[...] * gelu
    ms = jnp.mean(y * y, axis=-1, keepdims=True)
    o_ref[...] = (y * lax.rsqrt(ms + NORM_EPS) * og_ref[...]).astype(o_ref.dtype)
    xpad_ref[0:SUBLANES, :] = xb_ref[tr - SUBLANES:tr, :]


def rg_lru(proj, xb_block, gb_block, conv_w, conv_b, w_a, b_a, w_i, b_i, lam, out_g, tr=LRU_ROWS):
    s = proj.shape[0]
    cdim = conv_w.shape[1]
    nb, bd, _ = w_a.shape
    tr = min(tr, s)
    vec = pl.BlockSpec((1, cdim), lambda i: (0, 0))
    wspec = pl.BlockSpec((nb, bd, bd), lambda i: (0, 0, 0))
    return pl.pallas_call(
        _lru_kernel,
        grid=(s // tr,),
        in_specs=[pl.BlockSpec((tr, cdim), lambda i: (i, xb_block)),
                  pl.BlockSpec((tr, cdim), lambda i: (i, gb_block)),
                  pl.BlockSpec((CONV_WIDTH, cdim), lambda i: (0, 0)),
                  vec, wspec, vec, wspec, vec, vec, vec],
        out_specs=pl.BlockSpec((tr, cdim), lambda i: (i, 0)),
        out_shape=jax.ShapeDtypeStruct((s, cdim), BF16),
        scratch_shapes=[pltpu.VMEM((tr + SUBLANES, cdim), F32),
                        pltpu.VMEM((tr, cdim), F32),
                        pltpu.VMEM((tr, cdim), F32),
                        pltpu.VMEM((1, cdim), F32),
                        pltpu.VMEM((nb, bd, bd), BF16),
                        pltpu.VMEM((nb, bd, bd), BF16)],
        compiler_params=_params(("arbitrary",)),
        name="rg_lru",
    )(proj, proj, conv_w, conv_b.reshape(1, cdim), w_a, b_a.reshape(1, cdim), w_i,
      b_i.reshape(1, cdim), lam.reshape(1, cdim), out_g.reshape(1, cdim))


def _wqk_kernel(wq_ref, k_ref, o_ref):
    o_ref[...] = lax.dot_general(wq_ref[...].astype(BF16), k_ref[...], (((1,), (1,)), ((), ())),
                                 preferred_element_type=F32).astype(o_ref.dtype)


def _vo_kernel(v_ref, wo_ref, o_ref):
    o_ref[...] = jnp.dot(v_ref[...], wo_ref[...].astype(BF16),
                         preferred_element_type=F32).astype(o_ref.dtype)


def xattn_fold(k, v, wq, wo, tile=XF_TILE):
    mlen, d = k.shape
    hd = d // X_HEADS
    tile = min(tile, d)
    wqk = pl.pallas_call(
        _wqk_kernel,
        grid=(X_HEADS, d // tile),
        in_specs=[pl.BlockSpec((tile, hd), lambda h, r: (r, h)),
                  pl.BlockSpec((mlen, hd), lambda h, r: (0, h))],
        out_specs=pl.BlockSpec((tile, mlen), lambda h, r: (r, h)),
        out_shape=jax.ShapeDtypeStruct((d, X_HEADS * mlen), BF16),
        compiler_params=_params(("arbitrary", "arbitrary")),
        name="xattn_wqk",
    )(wq, k)
    vo = pl.pallas_call(
        _vo_kernel,
        grid=(X_HEADS, d // tile),
        in_specs=[pl.BlockSpec((mlen, hd), lambda h, j: (0, h)),
                  pl.BlockSpec((hd, tile), lambda h, j: (h, j))],
        out_specs=pl.BlockSpec((mlen, tile), lambda h, j: (h, j)),
        out_shape=jax.ShapeDtypeStruct((X_HEADS * mlen, d), BF16),
        compiler_params=_params(("arbitrary", "arbitrary")),
        name="xattn_vo",
    )(v, wo)
    return wqk, vo


def _xattn_kernel(x_ref, g_ref, wqk_ref, vo_ref, o_ref):
    d = x_ref.shape[1]
    mlen = wqk_ref.shape[1] // X_HEADS
    scale = (d // X_HEADS) ** -0.5
    x = x_ref[...]
    ms = jnp.mean(x * x, axis=-1, keepdims=True)
    h = (x * lax.rsqrt(ms + NORM_EPS) * g_ref[...]).astype(BF16)
    s = jnp.dot(h, wqk_ref[...], preferred_element_type=F32) * scale
    ps = []
    for hh in range(X_HEADS):
        sh = s[:, hh * mlen:(hh + 1) * mlen]
        m = jnp.max(sh, axis=-1, keepdims=True)
        e = jnp.exp(sh - m)
        ps.append((e / jnp.sum(e, axis=-1, keepdims=True)).astype(BF16))
    p = jnp.concatenate(ps, axis=1)
    o_ref[...] = x + jnp.dot(p, vo_ref[...], preferred_element_type=F32)


def xattn(x, g, wqk, vo, tm=ATT_ROWS):
    s, d = x.shape
    tm = min(tm, s)
    return pl.pallas_call(
        _xattn_kernel,
        grid=(s // tm,),
        in_specs=[pl.BlockSpec((tm, d), lambda i: (i, 0)),
                  pl.BlockSpec((1, d), lambda i: (0, 0)),
                  pl.BlockSpec(wqk.shape, lambda i: (0, 0)),
                  pl.BlockSpec(vo.shape, lambda i: (0, 0))],
        out_specs=pl.BlockSpec((tm, d), lambda i: (i, 0)),
        out_shape=jax.ShapeDtypeStruct((s, d), F32),
        compiler_params=_params(("arbitrary",)),
        name="xattn",
    )(x, g.reshape(1, d), wqk, vo)


def _router_kernel(x_ref, g_ref, wr_ref, br_ref, route_ref, counts_ref, carry_ref):
    tm = x_ref.shape[0]

    @pl.when(pl.program_id(0) == 0)
    def _():
        carry_ref[...] = jnp.zeros_like(carry_ref)

    x = x_ref[...]
    ms = jnp.mean(x * x, axis=-1, keepdims=True)
    h = x * lax.rsqrt(ms + NORM_EPS) * g_ref[...]
    logits = jnp.dot(h, wr_ref[...], preferred_element_type=F32,
                     precision=lax.Precision.HIGHEST) + br_ref[...]
    lane = lax.broadcasted_iota(jnp.int32, (tm, ROUTE_LANES), 1)
    neg = -jnp.inf
    big = ROUTE_LANES

    gmask = (lane >= GROUP_LANE0) & (lane < GROUP_LANE0 + N_GROUPS)
    gl = jnp.where(gmask, logits, neg)
    gmax = jnp.max(gl, axis=-1, keepdims=True)
    gsum = jnp.sum(jnp.where(gmask, jnp.exp(gl - gmax), 0.0), axis=-1, keepdims=True)
    g_val = 1.0 / gsum
    g_idx = jnp.min(jnp.where(gl == gmax, lane, big), axis=-1, keepdims=True) - GROUP_LANE0

    lo = EXPERT_LANE0 + g_idx * EXPERTS_PER_GROUP
    emask = (lane >= lo) & (lane < lo + EXPERTS_PER_GROUP)
    el = jnp.where(emask, logits, neg)
    t1 = jnp.max(el, axis=-1, keepdims=True)
    i1 = jnp.min(jnp.where(emask & (el == t1), lane, big), axis=-1, keepdims=True)
    emask2 = emask & (lane != i1)
    el2 = jnp.where(emask2, logits, neg)
    t2 = jnp.max(el2, axis=-1, keepdims=True)
    i2 = jnp.min(jnp.where(emask2 & (el2 == t2), lane, big), axis=-1, keepdims=True)
    dexp = jnp.exp(t2 - t1)
    w0 = g_val / (1.0 + dexp)
    w1 = g_val * dexp / (1.0 + dexp)

    sel1 = lane == i1
    sel2 = lane == i2
    onehot = jnp.where(sel1 | sel2, 1.0, 0.0)
    rr = lax.broadcasted_iota(jnp.int32, (tm, tm), 0)
    cc = lax.broadcasted_iota(jnp.int32, (tm, tm), 1)
    tri = jnp.where(cc < rr, 1.0, 0.0).astype(BF16)
    prefix = jnp.dot(tri, onehot.astype(BF16), preferred_element_type=F32) + carry_ref[...]
    rank0 = jnp.sum(jnp.where(sel1, prefix, 0.0), axis=-1, keepdims=True)
    rank1 = jnp.sum(jnp.where(sel2, prefix, 0.0), axis=-1, keepdims=True)
    total = carry_ref[...] + jnp.sum(onehot, axis=0, keepdims=True)
    carry_ref[...] = total
    counts_ref[...] = total

    e0 = (i1 - EXPERT_LANE0).astype(F32)
    e1 = (i2 - EXPERT_LANE0).astype(F32)
    route = jnp.where(lane == 0, e0, 0.0)
    route = jnp.where(lane == 1, e1, route)
    route = jnp.where(lane == 2, w0, route)
    route = jnp.where(lane == 3, w1, route)
    route = jnp.where(lane == 4, rank0, route)
    route = jnp.where(lane == 5, rank1, route)
    route_ref[...] = route


def router(x, g, wr, br, tm=ROUTE_ROWS):
    t, d = x.shape
    tm = min(tm, t)
    return pl.pallas_call(
        _router_kernel,
        grid=(t // tm,),
        in_specs=[pl.BlockSpec((tm, d), lambda i: (i, 0)),
                  pl.BlockSpec((1, d), lambda i: (0, 0)),
                  pl.BlockSpec((d, ROUTE_LANES), lambda i: (0, 0)),
                  pl.BlockSpec((1, ROUTE_LANES), lambda i: (0, 0))],
        out_specs=[pl.BlockSpec((tm, ROUTE_LANES), lambda i: (i, 0)),
                   pl.BlockSpec((1, ROUTE_LANES), lambda i: (0, 0))],
        out_shape=[jax.ShapeDtypeStruct((t, ROUTE_LANES), F32),
                   jax.ShapeDtypeStruct((1, ROUTE_LANES), F32)],
        scratch_shapes=[pltpu.VMEM((1, ROUTE_LANES), F32)],
        compiler_params=_params(("arbitrary",)),
        name="router",
    )(x, g.reshape(1, d), wr, br)


def _gather_kernel(e0_ref, e1_ref, r0_ref, r1_ref, ps_ref, nu_ref, g_ref, x_hbm, xs_ref,
                   rowtok_ref, buf_ref, sem, *, blk, n_tok):
    i = pl.program_id(0)
    nu = nu_ref[0]

    def row_copy(b, slot, r):
        tok = rowtok_ref[b * blk + r]
        return pltpu.make_async_copy(x_hbm.at[pl.ds(tok, 1), :],
                                     buf_ref.at[slot, pl.ds(r, 1), :], sem.at[slot])

    def start_block(b, slot):
        def body(r, carry):
            row_copy(b, slot, r).start()
            return carry
        lax.fori_loop(0, blk, body, 0)

    def wait_block(b, slot):
        def body(r, carry):
            row_copy(b, slot, r).wait()
            return carry
        lax.fori_loop(0, blk, body, 0)

    @pl.when(i == 0)
    def _():
        def zero(r, carry):
            rowtok_ref[r] = 0
            return carry
        lax.fori_loop(0, rowtok_ref.shape[0], zero, 0)

        def fill(t, carry):
            rowtok_ref[ps_ref[e0_ref[t]] + r0_ref[t]] = t
            rowtok_ref[ps_ref[e1_ref[t]] + r1_ref[t]] = t
            return carry
        lax.fori_loop(0, n_tok, fill, 0)
        start_block(0, 0)

    @pl.when(i + 1 < nu)
    def _():
        start_block(i + 1, (i + 1) % 2)

    @pl.when(i < nu)
    def _():
        slot = i % 2
        wait_block(i, slot)
        x = buf_ref[slot]
        ms = jnp.mean(x * x, axis=-1, keepdims=True)
        xs_ref[...] = (x * lax.rsqrt(ms + NORM_EPS) * g_ref[...]).astype(xs_ref.dtype)

    @pl.when(i >= nu)
    def _():
        xs_ref[...] = jnp.zeros_like(xs_ref)


def moe_gather(x, g, slots, n_used, n_blocks, blk):
    t, d = x.shape
    grid_spec = pltpu.PrefetchScalarGridSpec(
        num_scalar_prefetch=6,
        grid=(n_blocks,),
        in_specs=[pl.BlockSpec((1, d), lambda i, *_: (0, 0)),
                  pl.BlockSpec(memory_space=pl.ANY)],
        out_specs=pl.BlockSpec((blk, d), lambda i, *_: (i, 0)),
        scratch_shapes=[pltpu.SMEM((n_blocks * blk,), jnp.int32),
                        pltpu.VMEM((2, blk, d), F32),
                        pltpu.SemaphoreType.DMA((2,))],
    )
    return pl.pallas_call(
        functools.partial(_gather_kernel, blk=blk, n_tok=t),
        grid_spec=grid_spec,
        out_shape=jax.ShapeDtypeStruct((n_blocks * blk, d), BF16),
        compiler_params=_params(("arbitrary",)),
        name="moe_gather",
    )(*slots, n_used, g.reshape(1, d), x)


def _expert_changed(be_ref, i):
    prev = be_ref[jnp.maximum(i - 1, 0)]
    return (i == 0) | (be_ref[i] != prev)


def _moe_up_kernel(be_ref, nu_ref, xs_ref, wg_ref, wu_ref, act_ref, wgbf_ref, wubf_ref):
    i = pl.program_id(1)

    @pl.when(i < nu_ref[0])
    def _():
        @pl.when(_expert_changed(be_ref, i))
        def _():
            _cast_rows(wg_ref, wgbf_ref)
            _cast_rows(wu_ref, wubf_ref)

        x = xs_ref[...]
        gate = jnp.dot(x, wgbf_ref[...], preferred_element_type=F32)
        up = jnp.dot(x, wubf_ref[...], preferred_element_type=F32)
        act_ref[...] = (gate * _sigmoid(gate) * up).astype(act_ref.dtype)

    @pl.when(i >= nu_ref[0])
    def _():
        act_ref[...] = jnp.zeros_like(act_ref)


def _moe_down_kernel(be_ref, nu_ref, act_ref, wd_ref, y_ref, wdbf_ref):
    i = pl.program_id(1)

    @pl.when(i < nu_ref[0])
    def _():
        @pl.when(_expert_changed(be_ref, i))
        def _():
            _cast_rows(wd_ref, wdbf_ref)

        y_ref[...] = jnp.dot(act_ref[...], wdbf_ref[...], preferred_element_type=F32)

    @pl.when(i >= nu_ref[0])
    def _():
        y_ref[...] = jnp.zeros_like(y_ref)


def moe_experts(xs, block_expert, n_used, w_gate, w_up, w_down, blk=MOE_BLK, tf=MOE_TF, tn=MOE_TN):
    r, dw = xs.shape
    _, d, f = w_gate.shape
    n_blocks = r // blk
    tf = min(tf, f)
    tn = min(tn, d)

    def used(i, nu):
        return jnp.minimum(i, nu[0] - 1)

    up_spec = pltpu.PrefetchScalarGridSpec(
        num_scalar_prefetch=2,
        grid=(f // tf, n_blocks),
        in_specs=[pl.BlockSpec((blk, dw), lambda j, i, be, nu: (used(i, nu), 0)),
                  pl.BlockSpec((None, d, tf), lambda j, i, be, nu: (be[i], 0, j)),
                  pl.BlockSpec((None, d, tf), lambda j, i, be, nu: (be[i], 0, j))],
        out_specs=pl.BlockSpec((blk, tf), lambda j, i, be, nu: (i, j)),
        scratch_shapes=[pltpu.VMEM((d, tf), BF16), pltpu.VMEM((d, tf), BF16)],
    )
    act = pl.pallas_call(
        _moe_up_kernel,
        grid_spec=up_spec,
        out_shape=jax.ShapeDtypeStruct((r, f), BF16),
        compiler_params=_params(("arbitrary", "arbitrary")),
        name="moe_up",
    )(block_expert, n_used, xs, w_gate, w_up)
    down_spec = pltpu.PrefetchScalarGridSpec(
        num_scalar_prefetch=2,
        grid=(d // tn, n_blocks),
        in_specs=[pl.BlockSpec((blk, f), lambda j, i, be, nu: (used(i, nu), 0)),
                  pl.BlockSpec((None, f, tn), lambda j, i, be, nu: (be[i], 0, j))],
        out_specs=pl.BlockSpec((blk, tn), lambda j, i, be, nu: (i, j)),
        scratch_shapes=[pltpu.VMEM((f, tn), BF16)],
    )
    return pl.pallas_call(
        _moe_down_kernel,
        grid_spec=down_spec,
        out_shape=jax.ShapeDtypeStruct((r, d), F32),
        compiler_params=_params(("arbitrary", "arbitrary")),
        name="moe_down",
    )(block_expert, n_used, act, w_down)


def _combine_kernel(e0_ref, e1_ref, r0_ref, r1_ref, ps_ref, x_ref, route_ref, g_ref, y_hbm, o_ref,
                    ya_ref, yb_ref, sem, *, tb, final_norm):
    step = pl.program_id(0)

    def copies(b, slot, i):
        t = b * tb + i
        row0 = ps_ref[e0_ref[t]] + r0_ref[t]
        row1 = ps_ref[e1_ref[t]] + r1_ref[t]
        return (pltpu.make_async_copy(y_hbm.at[pl.ds(row0, 1), :],
                                      ya_ref.at[slot, pl.ds(i, 1), :], sem.at[slot]),
                pltpu.make_async_copy(y_hbm.at[pl.ds(row1, 1), :],
                                      yb_ref.at[slot, pl.ds(i, 1), :], sem.at[slot]))

    def start_block(b, slot):
        def body(i, carry):
            c0, c1 = copies(b, slot, i)
            c0.start()
            c1.start()
            return carry
        lax.fori_loop(0, tb, body, 0)

    def wait_block(b, slot):
        def body(i, carry):
            c0, c1 = copies(b, slot, i)
            c0.wait()
            c1.wait()
            return carry
        lax.fori_loop(0, tb, body, 0)

    @pl.when(step == 0)
    def _():
        start_block(0, 0)

    @pl.when(step + 1 < pl.num_programs(0))
    def _():
        start_block(step + 1, (step + 1) % 2)

    slot = step % 2
    wait_block(step, slot)
    w0 = route_ref[:, 2:3]
    w1 = route_ref[:, 3:4]
    x = x_ref[...] + (ya_ref[slot] * w0 + yb_ref[slot] * w1)
    if final_norm:
        ms = jnp.mean(x * x, axis=-1, keepdims=True)
        x = x * lax.rsqrt(ms + NORM_EPS) * g_ref[...]
    o_ref[...] = x


def combine(x, route, y, slots, g, final_norm, tb=COMB_ROWS):
    t, d = x.shape
    tb = min(tb, t)
    grid_spec = pltpu.PrefetchScalarGridSpec(
        num_scalar_prefetch=5,
        grid=(t // tb,),
        in_specs=[pl.BlockSpec((tb, d), lambda i, *_: (i, 0)),
                  pl.BlockSpec((tb, ROUTE_LANES), lambda i, *_: (i, 0)),
                  pl.BlockSpec((1, d), lambda i, *_: (0, 0)),
                  pl.BlockSpec(memory_space=pl.ANY)],
        out_specs=pl.BlockSpec((tb, d), lambda i, *_: (i, 0)),
        scratch_shapes=[pltpu.VMEM((2, tb, d), F32), pltpu.VMEM((2, tb, d), F32),
                        pltpu.SemaphoreType.DMA((2,))],
    )
    return pl.pallas_call(
        functools.partial(_combine_kernel, tb=tb, final_norm=final_norm),
        grid_spec=grid_spec,
        out_shape=jax.ShapeDtypeStruct((t, d), F32),
        compiler_params=_params(("arbitrary",)),
        name="moe_combine",
    )(*slots, x, route, g.reshape(1, d), y)


def _route_lanes(group_part, expert_part):
    rows = group_part.shape[0]
    gap = jnp.zeros((rows, EXPERT_LANE0 - GROUP_LANE0 - N_GROUPS), F32)
    tail = jnp.zeros((rows, ROUTE_LANES - EXPERT_LANE0 - N_EXPERTS), F32)
    return jnp.concatenate([group_part, gap, expert_part, tail], axis=1)


def _moe_layout(route, counts, blk):
    t = route.shape[0]
    ri = route[:, :8].astype(jnp.int32)
    e0, e1, rank0, rank1 = ri[:, 0], ri[:, 1], ri[:, 4], ri[:, 5]
    cnt = counts[0, EXPERT_LANE0:EXPERT_LANE0 + N_EXPERTS].astype(jnp.int32)
    padded = (cnt + blk - 1) // blk * blk
    pends = jnp.cumsum(padded)
    pstarts = pends - padded
    n_blocks = (2 * t) // blk + N_EXPERTS
    block_start = jnp.arange(n_blocks, dtype=jnp.int32) * blk
    block_expert = jnp.minimum(
        jnp.sum((block_start[:, None] >= pends[None, :]).astype(jnp.int32), axis=1), N_EXPERTS - 1)
    n_used = (pends[-1] // blk).astype(jnp.int32).reshape(1)
    block_expert = block_expert[jnp.minimum(jnp.arange(n_blocks), n_used[0] - 1)]
    return (e0, e1, rank0, rank1, pstarts), block_expert, n_used, n_blocks


def kernel(x, mem, positions, mix_norm_g, w_in, ret_norm_g, lru_conv_w, lru_conv_b, lru_w_a, lru_b_a, lru_w_i, lru_b_i, lru_lambda, lru_norm_g, w_out, xattn_norm_g, mem_norm_g, xattn_wq, xattn_wk, xattn_wv, xattn_wo, moe_norm_g, router_group_w, router_group_b, router_expert_w, router_expert_b, expert_w_gate, expert_w_up, expert_w_down, final_norm_g):
    b, s, d = x.shape
    depth = w_in.shape[0]
    ret_width = RET_HEADS * RET_HEAD_DIM
    lru_width = lru_conv_w.shape[-1]
    assert ret_width == lru_width and ret_width + lru_width == d
    inv_freq = ROPE_BASE ** (-jnp.arange(0, RET_HEAD_DIM, 2, dtype=F32) / RET_HEAD_DIM)
    lg = jnp.log1p(-jnp.exp2(-5.0 - jnp.arange(RET_HEADS, dtype=F32)))
    lg_rows = jnp.broadcast_to(lg[:, None, None], (RET_HEADS, 1, RET_HEAD_DIM))
    blk = min(MOE_BLK, s)
    outs = []
    for bi in range(b):
        xcur = x[bi]
        cos, sin = rope_tables(positions[bi].astype(F32), inv_freq)
        for l in range(depth):
            h = normcast(xcur, mix_norm_g[l], BF16, NORM_ROWS)
            proj = matmul([h], w_in[l], F32)
            ret = retention(proj, cos, sin, lg_rows, ret_norm_g[l])
            lru = rg_lru(proj, 4 * ret_width // lru_width, 4 * ret_width // lru_width + 1,
                         lru_conv_w[l], lru_conv_b[l], lru_w_a[l], lru_b_a[l], lru_w_i[l],
                         lru_b_i[l], lru_lambda[l], lru_norm_g[l])
            xcur = matmul([ret, lru], w_out[l], F32, res=xcur)
            memn = normcast(mem[bi], mem_norm_g[l], BF16, NORM_ROWS)
            kk = matmul([memn], xattn_wk[l], BF16)
            vv = matmul([memn], xattn_wv[l], BF16)
            wqk, vo = xattn_fold(kk, vv, xattn_wq[l], xattn_wo[l])
            xcur = xattn(xcur, xattn_norm_g[l], wqk, vo)
            wr = _route_lanes(router_group_w[l], router_expert_w[l])
            br = _route_lanes(router_group_b[l][None], router_expert_b[l][None])
            route, counts = router(xcur, moe_norm_g[l], wr, br)
            slots, block_expert, n_used, n_blocks = _moe_layout(route, counts, blk)
            xs = moe_gather(xcur, moe_norm_g[l], slots, n_used, n_blocks, blk)
            y = moe_experts(xs, block_expert, n_used, expert_w_gate[l], expert_w_up[l],
                            expert_w_down[l], blk=blk)
            xcur = combine(xcur, route, y, slots, final_norm_g, final_norm=l == depth - 1)
        outs.append(xcur)
    return outs[0][None] if b == 1 else jnp.stack(outs, axis=0)
```

```python
import functools

import jax
import jax.numpy as jnp
from jax import lax
from jax.experimental import pallas as pl
from jax.experimental.pallas import tpu as pltpu

F32 = jnp.float32
BF16 = jnp.bfloat16

RET_HEADS = 8
RET_HEAD_DIM = 256
RET_CHUNK = 128
LRU_BLOCKS = 8
CONV_WIDTH = 4
RG_C = 8.0
ROPE_BASE = 10000.0
X_HEADS = 4
N_GROUPS = 4
EXPERTS_PER_GROUP = 8
N_EXPERTS = N_GROUPS * EXPERTS_PER_GROUP
NORM_EPS = 1e-6
GN_EPS = 1e-5

LANES = 128
SUBLANES = 8
VMEM_LIMIT = 56 * 1024 * 1024

NORM_ROWS = 512
MM_TM = 1024
MM_TN = 512
RET_ROWS = 512
RET_HEADS_PER_STEP = 4
LRU_ROWS = 256
ATT_ROWS = 256
ROUTE_ROWS = 256
MOE_BLK = 256
MOE_TF = 512
MOE_TN = 4096
XF_TILE = 1024
SCALAR_UNROLL = 8
COMB_ROWS = 256
ROUTE_LANES = LANES
GROUP_LANE0 = 0
EXPERT_LANE0 = 8


def _params(sem):
    return pltpu.CompilerParams(dimension_semantics=sem, vmem_limit_bytes=VMEM_LIMIT)


def _normcast_kernel(x_ref, g_ref, o_ref):
    x = x_ref[...]
    ms = jnp.mean(x * x, axis=-1, keepdims=True)
    o_ref[...] = (x * lax.rsqrt(ms + NORM_EPS) * g_ref[...]).astype(o_ref.dtype)


def normcast(x, g, out_dtype, tm):
    m, d = x.shape
    tm = min(tm, m)
    return pl.pallas_call(
        _normcast_kernel,
        grid=(m // tm,),
        in_specs=[pl.BlockSpec((tm, d), lambda i: (i, 0)),
                  pl.BlockSpec((1, d), lambda i: (0, 0))],
        out_specs=pl.BlockSpec((tm, d), lambda i: (i, 0)),
        out_shape=jax.ShapeDtypeStruct((m, d), out_dtype),
        compiler_params=_params(("arbitrary",)),
        name="normcast",
    )(x, g.reshape(1, d))


def _cast_rows(src_ref, dst_ref, rows_per_iter=256):
    k = src_ref.shape[0]
    step = min(rows_per_iter, k)

    def body(i, carry):
        r0 = pl.multiple_of(i * step, step)
        dst_ref[pl.ds(r0, step), :] = src_ref[pl.ds(r0, step), :].astype(dst_ref.dtype)
        return carry

    lax.fori_loop(0, k // step, body, 0)


def _mm_kernel(*refs, n_a, has_res):
    a_refs = refs[:n_a]
    w_ref = refs[n_a]
    res_ref = refs[n_a + 1] if has_res else None
    o_ref = refs[n_a + 1 + int(has_res)]
    wbf_ref = refs[n_a + 2 + int(has_res)]

    @pl.when(pl.program_id(1) == 0)
    def _():
        _cast_rows(w_ref, wbf_ref)

    kp = a_refs[0].shape[1]
    acc = None
    for p, a_ref in enumerate(a_refs):
        d = jnp.dot(a_ref[...], wbf_ref[p * kp:(p + 1) * kp, :], preferred_element_type=F32)
        acc = d if acc is None else acc + d
    if has_res:
        acc = acc + res_ref[...]
    o_ref[...] = acc.astype(o_ref.dtype)


def matmul(a_parts, w, out_dtype, res=None, tm=MM_TM, tn=MM_TN):
    m, kp = a_parts[0].shape
    k, n = w.shape
    assert kp * len(a_parts) == k
    tm = min(tm, m)
    tn = min(tn, n)
    in_specs = [pl.BlockSpec((tm, kp), lambda j, i: (i, 0)) for _ in a_parts]
    in_specs.append(pl.BlockSpec((k, tn), lambda j, i: (0, j)))
    args = list(a_parts) + [w]
    if res is not None:
        in_specs.append(pl.BlockSpec((tm, tn), lambda j, i: (i, j)))
        args.append(res)
    return pl.pallas_call(
        functools.partial(_mm_kernel, n_a=len(a_parts), has_res=res is not None),
        grid=(n // tn, m // tm),
        in_specs=in_specs,
        out_specs=pl.BlockSpec((tm, tn), lambda j, i: (i, j)),
        out_shape=jax.ShapeDtypeStruct((m, n), out_dtype),
        scratch_shapes=[pltpu.VMEM((k, tn), BF16)],
        compiler_params=_params(("arbitrary", "arbitrary")),
        name="matmul",
    )(*args)


def _rope_kernel(pos_ref, invf_ref, cos_ref, sin_ref):
    ang = pos_ref[...] * invf_ref[...]
    cos_ref[...] = jnp.cos(ang)
    sin_ref[...] = jnp.sin(ang)


def rope_tables(pos_f, inv_freq, tm=512):
    s = pos_f.shape[0]
    hd = inv_freq.shape[0]
    tm = min(tm, s)
    return pl.pallas_call(
        _rope_kernel,
        grid=(s // tm,),
        in_specs=[pl.BlockSpec((tm, 1), lambda i: (i, 0)),
                  pl.BlockSpec((1, hd), lambda i: (0, 0))],
        out_specs=[pl.BlockSpec((tm, hd), lambda i: (i, 0))] * 2,
        out_shape=[jax.ShapeDtypeStruct((s, hd), F32)] * 2,
        compiler_params=_params(("arbitrary",)),
        name="rope_tables",
    )(pos_f.reshape(s, 1), inv_freq.reshape(1, hd))


def _ret_kernel(q_ref, k_ref, v_ref, g_ref, cos_ref, sin_ref, lg_ref, gn_ref, o_ref, r_ref,
                decay_ref, xi_ref, zeta_ref, *, n_chunks, hpb):
    c = RET_CHUNK
    dk = RET_HEAD_DIM
    half = dk // 2
    scale = dk ** -0.5

    @pl.when(pl.program_id(1) == 0)
    def _():
        r_ref[...] = jnp.zeros_like(r_ref)
        row = lax.broadcasted_iota(jnp.int32, (c, c), 0).astype(F32)
        col = lax.broadcasted_iota(jnp.int32, (c, c), 1).astype(F32)
        diff = row - col
        rowk = lax.broadcasted_iota(jnp.int32, (c, dk), 0).astype(F32)
        for hh in range(hpb):
            lg = lg_ref[hh]
            decay_ref[hh] = jnp.where(diff >= 0, jnp.exp(lg[:, :c] * jnp.maximum(diff, 0.0)), 0.0)
            xi_ref[hh] = jnp.exp(lg * (rowk + 1.0))
            zeta_ref[hh] = jnp.exp(lg * (c - 1.0 - rowk))

    def rope(t, cos, sin):
        t1 = t[:, :half]
        t2 = t[:, half:]
        return jnp.concatenate([t1 * cos - t2 * sin, t1 * sin + t2 * cos], axis=-1)

    def body(j, carry):
        r0 = pl.multiple_of(j * c, c)
        rows = pl.ds(r0, c)
        cos = cos_ref[rows, :]
        sin = sin_ref[rows, :]
        for hh in range(hpb):
            cs = slice(hh * dk, (hh + 1) * dk)
            qr = rope(q_ref[rows, cs], cos, sin)
            kr = rope(k_ref[rows, cs], cos, sin) * scale
            qb = qr.astype(BF16)
            kb = kr.astype(BF16)
            vb = v_ref[rows, cs].astype(BF16)
            state = r_ref[hh]
            inner = lax.dot_general(qb, kb, (((1,), (1,)), ((), ())),
                                    preferred_element_type=F32) * decay_ref[hh]
            o = (jnp.dot(inner.astype(BF16), vb, preferred_element_type=F32)
                 + jnp.dot(qb, state.astype(BF16), preferred_element_type=F32) * xi_ref[hh])
            kz = (kr * zeta_ref[hh]).astype(BF16)
            chunk_decay = jnp.exp(lg_ref[hh] * c)
            r_ref[hh] = state * chunk_decay + lax.dot_general(
                kz, vb, (((0,), (0,)), ((), ())), preferred_element_type=F32)
            mu = jnp.mean(o, axis=-1, keepdims=True)
            oc = o - mu
            var = jnp.mean(oc * oc, axis=-1, keepdims=True)
            on = oc * lax.rsqrt(var + GN_EPS) * gn_ref[hh]
            g = g_ref[rows, cs]
            o_ref[rows, cs] = (on * (g * (1.0 / (1.0 + jnp.exp(-g))))).astype(o_ref.dtype)
        return carry

    lax.fori_loop(0, n_chunks, body, 0)


def retention(proj, cos, sin, lg_rows, gn_g, tr=RET_ROWS, hpb=RET_HEADS_PER_STEP):
    s = proj.shape[0]
    dk = RET_HEAD_DIM
    h = RET_HEADS
    tr = min(tr, s)
    c = RET_CHUNK
    w = hpb * dk

    def col(base):
        return pl.BlockSpec((tr, w), lambda hg, ci, base=base: (ci, base // hpb + hg))

    per_head = pl.BlockSpec((hpb, 1, dk), lambda hg, ci: (hg, 0, 0))
    return pl.pallas_call(
        functools.partial(_ret_kernel, n_chunks=tr // c, hpb=hpb),
        grid=(h // hpb, s // tr),
        in_specs=[col(0), col(h), col(2 * h), col(3 * h),
                  pl.BlockSpec((tr, dk // 2), lambda hg, ci: (ci, 0)),
                  pl.BlockSpec((tr, dk // 2), lambda hg, ci: (ci, 0)),
                  per_head, per_head],
        out_specs=pl.BlockSpec((tr, w), lambda hg, ci: (ci, hg)),
        out_shape=jax.ShapeDtypeStruct((s, h * dk), BF16),
        scratch_shapes=[pltpu.VMEM((hpb, dk, dk), F32),
                        pltpu.VMEM((hpb, c, c), F32),
                        pltpu.VMEM((hpb, c, dk), F32),
                        pltpu.VMEM((hpb, c, dk), F32)],
        compiler_params=_params(("arbitrary", "arbitrary")),
        name="retention",
    )(proj, proj, proj, proj, cos, sin, lg_rows, gn_g.reshape(h, 1, dk))


def _sigmoid(x):
    return 1.0 / (1.0 + jnp.exp(-x))


def _lru_kernel(xb_ref, gb_ref, cw_ref, cb_ref, wa_ref, ba_ref, wi_ref, bi_ref, lam_ref, og_ref,
                o_ref, xpad_ref, a_ref, b_ref, h_ref, wabf_ref, wibf_ref):
    tr, cdim = xb_ref.shape
    nb = wa_ref.shape[0]
    bd = cdim // nb

    @pl.when(pl.program_id(0) == 0)
    def _():
        xpad_ref[0:SUBLANES, :] = jnp.zeros((SUBLANES, cdim), F32)
        h_ref[...] = jnp.zeros_like(h_ref)
        wabf_ref[...] = wa_ref[...].astype(BF16)
        wibf_ref[...] = wi_ref[...].astype(BF16)

    xpad_ref[SUBLANES:, :] = xb_ref[...]
    xc = cb_ref[...] + cw_ref[0:1, :] * xpad_ref[pl.ds(SUBLANES - 3, tr), :]
    for j in range(1, CONV_WIDTH):
        xc = xc + cw_ref[j:j + 1, :] * xpad_ref[pl.ds(SUBLANES - 3 + j, tr), :]

    lam = lam_ref[...]
    sp = jnp.maximum(-lam, 0.0) + jnp.log1p(jnp.exp(-jnp.abs(lam)))

    for n in range(nb):
        cs = slice(n * bd, (n + 1) * bd)
        xg = xc[:, cs]
        xgb = xg.astype(BF16)
        r = _sigmoid(jnp.dot(xgb, wabf_ref[n], preferred_element_type=F32) + ba_ref[:, cs])
        ig = _sigmoid(jnp.dot(xgb, wibf_ref[n], preferred_element_type=F32) + bi_ref[:, cs])
        log_a = (-RG_C * r) * sp[:, cs]
        a = jnp.exp(log_a)
        a_ref[:, cs] = a
        b_ref[:, cs] = jnp.sqrt(-jnp.tanh(log_a) * (a * a + 1.0)) * (ig * xg)

    rowi = lax.broadcasted_iota(jnp.int32, (SUBLANES, cdim), 0)

    def body(gi, h):
        r0 = pl.multiple_of(gi * SUBLANES, SUBLANES)
        rows = pl.ds(r0, SUBLANES)
        a = a_ref[rows, :]
        b = b_ref[rows, :]
        d = 1
        while d < SUBLANES:
            keep = rowi >= d
            a_sh = pltpu.roll(a, d, 0)
            b_sh = pltpu.roll(b, d, 0)
            b = jnp.where(keep, a * b_sh + b, b)
            a = jnp.where(keep, a * a_sh, a)
            d *= 2
        hh = a * h + b
        b_ref[rows, :] = hh
        return hh[SUBLANES - 1:SUBLANES, :]

    h_ref[...] = lax.fori_loop(0, tr // SUBLANES, body, h_ref[...])

    gb = gb_ref[...]
    gelu = 0.5 * gb * (1.0 + jnp.tanh(0.7978845608028654 * (gb + 0.044715 * (gb * gb * gb))))
    y = b_ref[...] * gelu
    ms = jnp.mean(y * y, axis=-1, keepdims=True)
    o_ref[...] = (y * lax.rsqrt(ms + NORM_EPS) * og_ref[...]).astype(o_ref.dtype)
    xpad_ref[0:SUBLANES, :] = xb_ref[tr - SUBLANES:tr, :]


def rg_lru(proj, xb_block, gb_block, conv_w, conv_b, w_a, b_a, w_i, b_i, lam, out_g, tr=LRU_ROWS):
    s = proj.shape[0]
    cdim = conv_w.shape[1]
    nb, bd, _ = w_a.shape
    tr = min(tr, s)
    vec = pl.BlockSpec((1, cdim), lambda i: (0, 0))
    wspec = pl.BlockSpec((nb, bd, bd), lambda i: (0, 0, 0))
    return pl.pallas_call(
        _lru_kernel,
        grid=(s // tr,),
        in_specs=[pl.BlockSpec((tr, cdim), lambda i: (i, xb_block)),
                  pl.BlockSpec((tr, cdim), lambda i: (i, gb_block)),
                  pl.BlockSpec((CONV_WIDTH, cdim), lambda i: (0, 0)),
                  vec, wspec, vec, wspec, vec, vec, vec],
        out_specs=pl.BlockSpec((tr, cdim), lambda i: (i, 0)),
        out_shape=jax.ShapeDtypeStruct((s, cdim), BF16),
        scratch_shapes=[pltpu.VMEM((tr + SUBLANES, cdim), F32),
                        pltpu.VMEM((tr, cdim), F32),
                        pltpu.VMEM((tr, cdim), F32),
                        pltpu.VMEM((1, cdim), F32),
                        pltpu.VMEM((nb, bd, bd), BF16),
                        pltpu.VMEM((nb, bd, bd), BF16)],
        compiler_params=_params(("arbitrary",)),
        name="rg_lru",
    )(proj, proj, conv_w, conv_b.reshape(1, cdim), w_a, b_a.reshape(1, cdim), w_i,
      b_i.reshape(1, cdim), lam.reshape(1, cdim), out_g.reshape(1, cdim))


def _wqk_kernel(wq_ref, k_ref, o_ref):
    o_ref[...] = lax.dot_general(wq_ref[...].astype(BF16), k_ref[...], (((1,), (1,)), ((), ())),
                                 preferred_element_type=F32).astype(o_ref.dtype)


def _vo_kernel(v_ref, wo_ref, o_ref):
    o_ref[...] = jnp.dot(v_ref[...], wo_ref[...].astype(BF16),
                         preferred_element_type=F32).astype(o_ref.dtype)


def xattn_fold(k, v, wq, wo, tile=XF_TILE):
    mlen, d = k.shape
    hd = d // X_HEADS
    tile = min(tile, d)
    wqk = pl.pallas_call(
        _wqk_kernel,
        grid=(X_HEADS, d // tile),
        in_specs=[pl.BlockSpec((tile, hd), lambda h, r: (r, h)),
                  pl.BlockSpec((mlen, hd), lambda h, r: (0, h))],
        out_specs=pl.BlockSpec((tile, mlen), lambda h, r: (r, h)),
        out_shape=jax.ShapeDtypeStruct((d, X_HEADS * mlen), BF16),
        compiler_params=_params(("arbitrary", "arbitrary")),
        name="xattn_wqk",
    )(wq, k)
    vo = pl.pallas_call(
        _vo_kernel,
        grid=(X_HEADS, d // tile),
        in_specs=[pl.BlockSpec((mlen, hd), lambda h, j: (0, h)),
                  pl.BlockSpec((hd, tile), lambda h, j: (h, j))],
        out_specs=pl.BlockSpec((mlen, tile), lambda h, j: (h, j)),
        out_shape=jax.ShapeDtypeStruct((X_HEADS * mlen, d), BF16),
        compiler_params=_params(("arbitrary", "arbitrary")),
        name="xattn_vo",
    )(v, wo)
    return wqk, vo


def _xattn_kernel(x_ref, g_ref, wqk_ref, vo_ref, o_ref):
    d = x_ref.shape[1]
    mlen = wqk_ref.shape[1] // X_HEADS
    scale = (d // X_HEADS) ** -0.5
    x = x_ref[...]
    ms = jnp.mean(x * x, axis=-1, keepdims=True)
    h = (x * lax.rsqrt(ms + NORM_EPS) * g_ref[...]).astype(BF16)
    s = jnp.dot(h, wqk_ref[...], preferred_element_type=F32) * scale
    ps = []
    for hh in range(X_HEADS):
        sh = s[:, hh * mlen:(hh + 1) * mlen]
        m = jnp.max(sh, axis=-1, keepdims=True)
        e = jnp.exp(sh - m)
        ps.append((e / jnp.sum(e, axis=-1, keepdims=True)).astype(BF16))
    p = jnp.concatenate(ps, axis=1)
    o_ref[...] = x + jnp.dot(p, vo_ref[...], preferred_element_type=F32)


def xattn(x, g, wqk, vo, tm=ATT_ROWS):
    s, d = x.shape
    tm = min(tm, s)
    return pl.pallas_call(
        _xattn_kernel,
        grid=(s // tm,),
        in_specs=[pl.BlockSpec((tm, d), lambda i: (i, 0)),
                  pl.BlockSpec((1, d), lambda i: (0, 0)),
                  pl.BlockSpec(wqk.shape, lambda i: (0, 0)),
                  pl.BlockSpec(vo.shape, lambda i: (0, 0))],
        out_specs=pl.BlockSpec((tm, d), lambda i: (i, 0)),
        out_shape=jax.ShapeDtypeStruct((s, d), F32),
        compiler_params=_params(("arbitrary",)),
        name="xattn",
    )(x, g.reshape(1, d), wqk, vo)


def _split_bf16(a):
    hi = a.astype(BF16)
    return hi, (a - hi.astype(F32)).astype(BF16)


def _router_kernel(x_ref, g_ref, wr_ref, br_ref, route_ref, counts_ref, carry_ref, wsplit_ref):
    tm = x_ref.shape[0]
    nl = ROUTE_LANES

    @pl.when(pl.program_id(0) == 0)
    def _():
        carry_ref[...] = jnp.zeros_like(carry_ref)
        w_hi, w_lo = _split_bf16(wr_ref[...])
        wsplit_ref[:, :nl] = w_hi
        wsplit_ref[:, nl:] = w_lo

    x = x_ref[...]
    ms = jnp.mean(x * x, axis=-1, keepdims=True)
    h = x * lax.rsqrt(ms + NORM_EPS) * g_ref[...]
    h_hi, h_lo = _split_bf16(h)
    both = jnp.dot(h_hi, wsplit_ref[...], preferred_element_type=F32)
    cross = jnp.dot(h_lo, wsplit_ref[:, :nl], preferred_element_type=F32)
    logits = both[:, :nl] + (both[:, nl:] + cross) + br_ref[...]
    lane = lax.broadcasted_iota(jnp.int32, (tm, ROUTE_LANES), 1)
    neg = -jnp.inf
    big = ROUTE_LANES

    gmask = (lane >= GROUP_LANE0) & (lane < GROUP_LANE0 + N_GROUPS)
    gl = jnp.where(gmask, logits, neg)
    gmax = jnp.max(gl, axis=-1, keepdims=True)
    gsum = jnp.sum(jnp.where(gmask, jnp.exp(gl - gmax), 0.0), axis=-1, keepdims=True)
    g_val = 1.0 / gsum
    g_idx = jnp.min(jnp.where(gl == gmax, lane, big), axis=-1, keepdims=True) - GROUP_LANE0

    lo = EXPERT_LANE0 + g_idx * EXPERTS_PER_GROUP
    emask = (lane >= lo) & (lane < lo + EXPERTS_PER_GROUP)
    el = jnp.where(emask, logits, neg)
    t1 = jnp.max(el, axis=-1, keepdims=True)
    i1 = jnp.min(jnp.where(emask & (el == t1), lane, big), axis=-1, keepdims=True)
    emask2 = emask & (lane != i1)
    el2 = jnp.where(emask2, logits, neg)
    t2 = jnp.max(el2, axis=-1, keepdims=True)
    i2 = jnp.min(jnp.where(emask2 & (el2 == t2), lane, big), axis=-1, keepdims=True)
    dexp = jnp.exp(t2 - t1)
    w0 = g_val / (1.0 + dexp)
    w1 = g_val * dexp / (1.0 + dexp)

    sel1 = lane == i1
    sel2 = lane == i2
    onehot = jnp.where(sel1 | sel2, 1.0, 0.0)
    rr = lax.broadcasted_iota(jnp.int32, (tm, tm), 0)
    cc = lax.broadcasted_iota(jnp.int32, (tm, tm), 1)
    tri = jnp.where(cc < rr, 1.0, 0.0).astype(BF16)
    prefix = jnp.dot(tri, onehot.astype(BF16), preferred_element_type=F32) + carry_ref[...]
    rank0 = jnp.sum(jnp.where(sel1, prefix, 0.0), axis=-1, keepdims=True)
    rank1 = jnp.sum(jnp.where(sel2, prefix, 0.0), axis=-1, keepdims=True)
    total = carry_ref[...] + jnp.sum(onehot, axis=0, keepdims=True)
    carry_ref[...] = total
    counts_ref[...] = total

    e0 = (i1 - EXPERT_LANE0).astype(F32)
    e1 = (i2 - EXPERT_LANE0).astype(F32)
    route = jnp.where(lane == 0, e0, 0.0)
    route = jnp.where(lane == 1, e1, route)
    route = jnp.where(lane == 2, w0, route)
    route = jnp.where(lane == 3, w1, route)
    route = jnp.where(lane == 4, rank0, route)
    route = jnp.where(lane == 5, rank1, route)
    route_ref[...] = route


def router(x, g, wr, br, tm=ROUTE_ROWS):
    t, d = x.shape
    tm = min(tm, t)
    return pl.pallas_call(
        _router_kernel,
        grid=(t // tm,),
        in_specs=[pl.BlockSpec((tm, d), lambda i: (i, 0)),
                  pl.BlockSpec((1, d), lambda i: (0, 0)),
                  pl.BlockSpec((d, ROUTE_LANES), lambda i: (0, 0)),
                  pl.BlockSpec((1, ROUTE_LANES), lambda i: (0, 0))],
        out_specs=[pl.BlockSpec((tm, ROUTE_LANES), lambda i: (i, 0)),
                   pl.BlockSpec((1, ROUTE_LANES), lambda i: (0, 0))],
        out_shape=[jax.ShapeDtypeStruct((t, ROUTE_LANES), F32),
                   jax.ShapeDtypeStruct((1, ROUTE_LANES), F32)],
        scratch_shapes=[pltpu.VMEM((1, ROUTE_LANES), F32),
                        pltpu.VMEM((d, 2 * ROUTE_LANES), BF16)],
        compiler_params=_params(("arbitrary",)),
        name="router",
    )(x, g.reshape(1, d), wr, br)


def _gather_kernel(e0_ref, e1_ref, r0_ref, r1_ref, ps_ref, nu_ref, g_ref, x_hbm, xs_ref,
                   rowtok_ref, buf_ref, sem, *, blk, n_tok):
    i = pl.program_id(0)
    nu = nu_ref[0]

    def row_copy(b, slot, r):
        tok = rowtok_ref[b * blk + r]
        return pltpu.make_async_copy(x_hbm.at[pl.ds(tok, 1), :],
                                     buf_ref.at[slot, pl.ds(r, 1), :], sem.at[slot])

    def start_block(b, slot):
        def body(r, carry):
            row_copy(b, slot, r).start()
            return carry
        lax.fori_loop(0, blk, body, 0, unroll=SCALAR_UNROLL)

    def wait_block(b, slot):
        def body(r, carry):
            row_copy(b, slot, r).wait()
            return carry
        lax.fori_loop(0, blk, body, 0, unroll=SCALAR_UNROLL)

    @pl.when(i == 0)
    def _():
        n_rows = rowtok_ref.shape[0]
        for base in range(0, n_rows, n_tok):
            def init(r, carry, base=base):
                rowtok_ref[base + r] = r
                return carry
            lax.fori_loop(0, min(n_tok, n_rows - base), init, 0, unroll=SCALAR_UNROLL)

        def fill(t, carry):
            rowtok_ref[ps_ref[e0_ref[t]] + r0_ref[t]] = t
            rowtok_ref[ps_ref[e1_ref[t]] + r1_ref[t]] = t
            return carry
        lax.fori_loop(0, n_tok, fill, 0, unroll=SCALAR_UNROLL)
        start_block(0, 0)

    @pl.when(i + 1 < nu)
    def _():
        start_block(i + 1, (i + 1) % 2)

    @pl.when(i < nu)
    def _():
        slot = i % 2
        wait_block(i, slot)
        x = buf_ref[slot]
        ms = jnp.mean(x * x, axis=-1, keepdims=True)
        xs_ref[...] = (x * lax.rsqrt(ms + NORM_EPS) * g_ref[...]).astype(xs_ref.dtype)

    @pl.when(i >= nu)
    def _():
        xs_ref[...] = jnp.zeros_like(xs_ref)


def moe_gather(x, g, slots, n_used, n_blocks, blk):
    t, d = x.shape
    grid_spec = pltpu.PrefetchScalarGridSpec(
        num_scalar_prefetch=6,
        grid=(n_blocks,),
        in_specs=[pl.BlockSpec((1, d), lambda i, *_: (0, 0)),
                  pl.BlockSpec(memory_space=pl.ANY)],
        out_specs=pl.BlockSpec((blk, d), lambda i, *_: (i, 0)),
        scratch_shapes=[pltpu.SMEM((n_blocks * blk,), jnp.int32),
                        pltpu.VMEM((2, blk, d), F32),
                        pltpu.SemaphoreType.DMA((2,))],
    )
    return pl.pallas_call(
        functools.partial(_gather_kernel, blk=blk, n_tok=t),
        grid_spec=grid_spec,
        out_shape=jax.ShapeDtypeStruct((n_blocks * blk, d), BF16),
        compiler_params=_params(("arbitrary",)),
        name="moe_gather",
    )(*slots, n_used, g.reshape(1, d), x)


def _expert_changed(be_ref, i):
    prev = be_ref[jnp.maximum(i - 1, 0)]
    return (i == 0) | (be_ref[i] != prev)


def _stream_expert_weights(w_hbms, col0, be_ref, nx_ref, nu, i, wst_ref, wbf_ref, sem, slot_ref):
    tn = wst_ref.shape[-1]

    def copies(e, slot):
        return [pltpu.make_async_copy(w.at[e, :, pl.ds(col0, tn)], wst_ref.at[slot, l], sem.at[slot])
                for l, w in enumerate(w_hbms)]

    @pl.when(i == 0)
    def _():
        slot_ref[0] = 0
        for c in copies(be_ref[0], 0):
            c.start()

    @pl.when(_expert_changed(be_ref, i))
    def _():
        slot = slot_ref[0]
        for c in copies(be_ref[i], slot):
            c.wait()
        nxt = nx_ref[i]

        @pl.when(nxt < nu)
        def _():
            for c in copies(be_ref[jnp.minimum(nxt, be_ref.shape[0] - 1)], 1 - slot):
                c.start()

        for l in range(len(w_hbms)):
            _cast_rows(wst_ref.at[slot, l], wbf_ref.at[l])
        slot_ref[0] = 1 - slot


def _moe_up_kernel(be_ref, nx_ref, nu_ref, xs_ref, wg_hbm, wu_hbm, act_ref,
                   wst_ref, wbf_ref, sem, slot_ref):
    j = pl.program_id(0)
    i = pl.program_id(1)
    nu = nu_ref[0]
    tf = act_ref.shape[1]

    @pl.when(i < nu)
    def _():
        _stream_expert_weights([wg_hbm, wu_hbm], pl.multiple_of(j * tf, tf), be_ref, nx_ref, nu, i,
                               wst_ref, wbf_ref, sem, slot_ref)
        x = xs_ref[...]
        gate = jnp.dot(x, wbf_ref[0], preferred_element_type=F32)
        up = jnp.dot(x, wbf_ref[1], preferred_element_type=F32)
        act_ref[...] = (gate * _sigmoid(gate) * up).astype(act_ref.dtype)

    @pl.when(i >= nu)
    def _():
        act_ref[...] = jnp.zeros_like(act_ref)


def _moe_down_kernel(be_ref, nx_ref, nu_ref, act_ref, wd_hbm, y_ref, wst_ref, wbf_ref, sem, slot_ref):
    j = pl.program_id(0)
    i = pl.program_id(1)
    nu = nu_ref[0]
    tn = y_ref.shape[1]

    @pl.when(i < nu)
    def _():
        _stream_expert_weights([wd_hbm], pl.multiple_of(j * tn, tn), be_ref, nx_ref, nu, i,
                               wst_ref, wbf_ref, sem, slot_ref)
        y_ref[...] = jnp.dot(act_ref[...], wbf_ref[0], preferred_element_type=F32)

    @pl.when(i >= nu)
    def _():
        y_ref[...] = jnp.zeros_like(y_ref)


def moe_experts(xs, block_expert, next_expert_block, n_used, w_gate, w_up, w_down,
                blk=MOE_BLK, tf=MOE_TF, tn=MOE_TN):
    r, dw = xs.shape
    _, d, f = w_gate.shape
    n_blocks = r // blk
    tf = min(tf, f)
    tn = min(tn, d)

    def used(i, nu):
        return jnp.minimum(i, jnp.maximum(nu[0] - 1, 0))

    def stream_scratch(n_mats, k, n):
        return [pltpu.VMEM((2, n_mats, k, n), F32),
                pltpu.VMEM((n_mats, k, n), BF16),
                pltpu.SemaphoreType.DMA((2,)),
                pltpu.SMEM((1,), jnp.int32)]

    up_spec = pltpu.PrefetchScalarGridSpec(
        num_scalar_prefetch=3,
        grid=(f // tf, n_blocks),
        in_specs=[pl.BlockSpec((blk, dw), lambda j, i, be, nx, nu: (used(i, nu), 0)),
                  pl.BlockSpec(memory_space=pl.ANY),
                  pl.BlockSpec(memory_space=pl.ANY)],
        out_specs=pl.BlockSpec((blk, tf), lambda j, i, be, nx, nu: (i, j)),
        scratch_shapes=stream_scratch(2, d, tf),
    )
    act = pl.pallas_call(
        _moe_up_kernel,
        grid_spec=up_spec,
        out_shape=jax.ShapeDtypeStruct((r, f), BF16),
        compiler_params=_params(("arbitrary", "arbitrary")),
        name="moe_up",
    )(block_expert, next_expert_block, n_used, xs, w_gate, w_up)
    down_spec = pltpu.PrefetchScalarGridSpec(
        num_scalar_prefetch=3,
        grid=(d // tn, n_blocks),
        in_specs=[pl.BlockSpec((blk, f), lambda j, i, be, nx, nu: (used(i, nu), 0)),
                  pl.BlockSpec(memory_space=pl.ANY)],
        out_specs=pl.BlockSpec((blk, tn), lambda j, i, be, nx, nu: (i, j)),
        scratch_shapes=stream_scratch(1, f, tn),
    )
    return pl.pallas_call(
        _moe_down_kernel,
        grid_spec=down_spec,
        out_shape=jax.ShapeDtypeStruct((r, d), F32),
        compiler_params=_params(("arbitrary", "arbitrary")),
        name="moe_down",
    )(block_expert, next_expert_block, n_used, act, w_down)


def _combine_kernel(e0_ref, e1_ref, r0_ref, r1_ref, ps_ref, x_ref, route_ref, g_ref, y_hbm, o_ref,
                    ya_ref, yb_ref, sem, *, tb, final_norm):
    step = pl.program_id(0)

    def copies(b, slot, i):
        t = b * tb + i
        row0 = ps_ref[e0_ref[t]] + r0_ref[t]
        row1 = ps_ref[e1_ref[t]] + r1_ref[t]
        return (pltpu.make_async_copy(y_hbm.at[pl.ds(row0, 1), :],
                                      ya_ref.at[slot, pl.ds(i, 1), :], sem.at[slot]),
                pltpu.make_async_copy(y_hbm.at[pl.ds(row1, 1), :],
                                      yb_ref.at[slot, pl.ds(i, 1), :], sem.at[slot]))

    def start_block(b, slot):
        def body(i, carry):
            c0, c1 = copies(b, slot, i)
            c0.start()
            c1.start()
            return carry
        lax.fori_loop(0, tb, body, 0, unroll=SCALAR_UNROLL)

    def wait_block(b, slot):
        def body(i, carry):
            c0, c1 = copies(b, slot, i)
            c0.wait()
            c1.wait()
            return carry
        lax.fori_loop(0, tb, body, 0, unroll=SCALAR_UNROLL)

    @pl.when(step == 0)
    def _():
        start_block(0, 0)

    @pl.when(step + 1 < pl.num_programs(0))
    def _():
        start_block(step + 1, (step + 1) % 2)

    slot = step % 2
    wait_block(step, slot)
    w0 = route_ref[:, 2:3]
    w1 = route_ref[:, 3:4]
    x = x_ref[...] + (ya_ref[slot] * w0 + yb_ref[slot] * w1)
    if final_norm:
        ms = jnp.mean(x * x, axis=-1, keepdims=True)
        x = x * lax.rsqrt(ms + NORM_EPS) * g_ref[...]
    o_ref[...] = x


def combine(x, route, y, slots, g, final_norm, tb=COMB_ROWS):
    t, d = x.shape
    tb = min(tb, t)
    grid_spec = pltpu.PrefetchScalarGridSpec(
        num_scalar_prefetch=5,
        grid=(t // tb,),
        in_specs=[pl.BlockSpec((tb, d), lambda i, *_: (i, 0)),
                  pl.BlockSpec((tb, ROUTE_LANES), lambda i, *_: (i, 0)),
                  pl.BlockSpec((1, d), lambda i, *_: (0, 0)),
                  pl.BlockSpec(memory_space=pl.ANY)],
        out_specs=pl.BlockSpec((tb, d), lambda i, *_: (i, 0)),
        scratch_shapes=[pltpu.VMEM((2, tb, d), F32), pltpu.VMEM((2, tb, d), F32),
                        pltpu.SemaphoreType.DMA((2,))],
    )
    return pl.pallas_call(
        functools.partial(_combine_kernel, tb=tb, final_norm=final_norm),
        grid_spec=grid_spec,
        out_shape=jax.ShapeDtypeStruct((t, d), F32),
        compiler_params=_params(("arbitrary",)),
        name="moe_combine",
    )(*slots, x, route, g.reshape(1, d), y)


def _route_lanes(group_part, expert_part):
    rows = group_part.shape[0]
    gap = jnp.zeros((rows, EXPERT_LANE0 - GROUP_LANE0 - N_GROUPS), F32)
    tail = jnp.zeros((rows, ROUTE_LANES - EXPERT_LANE0 - N_EXPERTS), F32)
    return jnp.concatenate([group_part, gap, expert_part, tail], axis=1)


def _moe_layout(route, counts, blk):
    t = route.shape[0]
    ri = route[:, :8].astype(jnp.int32)
    e0, e1, rank0, rank1 = ri[:, 0], ri[:, 1], ri[:, 4], ri[:, 5]
    cnt = counts[0, EXPERT_LANE0:EXPERT_LANE0 + N_EXPERTS].astype(jnp.int32)
    padded = (cnt + blk - 1) // blk * blk
    pends = jnp.cumsum(padded)
    pstarts = pends - padded
    n_blocks = (2 * t) // blk + N_EXPERTS
    block_start = jnp.arange(n_blocks, dtype=jnp.int32) * blk
    block_expert = jnp.minimum(
        jnp.sum((block_start[:, None] >= pends[None, :]).astype(jnp.int32), axis=1), N_EXPERTS - 1)
    n_used = (pends[-1] // blk).astype(jnp.int32).reshape(1)
    block_expert = block_expert[jnp.minimum(jnp.arange(n_blocks), jnp.maximum(n_used[0] - 1, 0))]
    next_expert_block = pends[block_expert] // blk
    return (e0, e1, rank0, rank1, pstarts), block_expert, next_expert_block, n_used, n_blocks


def kernel(x, mem, positions, mix_norm_g, w_in, ret_norm_g, lru_conv_w, lru_conv_b, lru_w_a, lru_b_a, lru_w_i, lru_b_i, lru_lambda, lru_norm_g, w_out, xattn_norm_g, mem_norm_g, xattn_wq, xattn_wk, xattn_wv, xattn_wo, moe_norm_g, router_group_w, router_group_b, router_expert_w, router_expert_b, expert_w_gate, expert_w_up, expert_w_down, final_norm_g):
    b, s, d = x.shape
    depth = w_in.shape[0]
    ret_width = RET_HEADS * RET_HEAD_DIM
    lru_width = lru_conv_w.shape[-1]
    assert ret_width == lru_width and ret_width + lru_width == d
    inv_freq = ROPE_BASE ** (-jnp.arange(0, RET_HEAD_DIM, 2, dtype=F32) / RET_HEAD_DIM)
    lg = jnp.log1p(-jnp.exp2(-5.0 - jnp.arange(RET_HEADS, dtype=F32)))
    lg_rows = jnp.broadcast_to(lg[:, None, None], (RET_HEADS, 1, RET_HEAD_DIM))
    blk = min(MOE_BLK, s)
    outs = []
    for bi in range(b):
        xcur = x[bi]
        cos, sin = rope_tables(positions[bi].astype(F32), inv_freq)
        for l in range(depth):
            h = normcast(xcur, mix_norm_g[l], BF16, NORM_ROWS)
            proj = matmul([h], w_in[l], F32)
            ret = retention(proj, cos, sin, lg_rows, ret_norm_g[l])
            lru = rg_lru(proj, 4 * ret_width // lru_width, 4 * ret_width // lru_width + 1,
                         lru_conv_w[l], lru_conv_b[l], lru_w_a[l], lru_b_a[l], lru_w_i[l],
                         lru_b_i[l], lru_lambda[l], lru_norm_g[l])
            xcur = matmul([ret, lru], w_out[l], F32, res=xcur)
            memn = normcast(mem[bi], mem_norm_g[l], BF16, NORM_ROWS)
            kk = matmul([memn], xattn_wk[l], BF16)
            vv = matmul([memn], xattn_wv[l], BF16)
            wqk, vo = xattn_fold(kk, vv, xattn_wq[l], xattn_wo[l])
            xcur = xattn(xcur, xattn_norm_g[l], wqk, vo)
            wr = _route_lanes(router_group_w[l], router_expert_w[l])
            br = _route_lanes(router_group_b[l][None], router_expert_b[l][None])
            route, counts = router(xcur, moe_norm_g[l], wr, br)
            slots, block_expert, next_block, n_used, n_blocks = _moe_layout(route, counts, blk)
            xs = moe_gather(xcur, moe_norm_g[l], slots, n_used, n_blocks, blk)
            y = moe_experts(xs, block_expert, next_block, n_used, expert_w_gate[l], expert_w_up[l],
                            expert_w_down[l], blk=blk)
            xcur = combine(xcur, route, y, slots, final_norm_g, final_norm=l == depth - 1)
        outs.append(xcur)
    return outs[0][None] if b == 1 else jnp.stack(outs, axis=0)
```

```python
import functools

import jax
import jax.numpy as jnp
from jax import lax
from jax.experimental import pallas as pl
from jax.experimental.pallas import tpu as pltpu

F32 = jnp.float32
BF16 = jnp.bfloat16

RET_HEADS = 8
RET_HEAD_DIM = 256
RET_CHUNK = 128
LRU_BLOCKS = 8
CONV_WIDTH = 4
RG_C = 8.0
ROPE_BASE = 10000.0
X_HEADS = 4
N_GROUPS = 4
EXPERTS_PER_GROUP = 8
N_EXPERTS = N_GROUPS * EXPERTS_PER_GROUP
NORM_EPS = 1e-6
GN_EPS = 1e-5

LANES = 128
SUBLANES = 8
VMEM_LIMIT = 56 * 1024 * 1024

NORM_ROWS = 512
MM_TM = 1024
MM_TN = 512
RET_ROWS = 512
RET_HEADS_PER_STEP = 8
LRU_ROWS = 256
ATT_ROWS = 256
ROUTE_ROWS = 256
MOE_BLK = 256
MOE_TF = 512
MOE_TN = 4096
XF_TILE = 1024
SCALAR_UNROLL = 8
COMB_ROWS = 256
ROUTE_LANES = LANES
GROUP_LANE0 = 0
EXPERT_LANE0 = 8


def _params(sem):
    return pltpu.CompilerParams(dimension_semantics=sem, vmem_limit_bytes=VMEM_LIMIT)


def _normcast_kernel(x_ref, g_ref, o_ref):
    x = x_ref[...]
    ms = jnp.mean(x * x, axis=-1, keepdims=True)
    o_ref[...] = (x * lax.rsqrt(ms + NORM_EPS) * g_ref[...]).astype(o_ref.dtype)


def normcast(x, g, out_dtype, tm):
    m, d = x.shape
    tm = min(tm, m)
    return pl.pallas_call(
        _normcast_kernel,
        grid=(m // tm,),
        in_specs=[pl.BlockSpec((tm, d), lambda i: (i, 0)),
                  pl.BlockSpec((1, d), lambda i: (0, 0))],
        out_specs=pl.BlockSpec((tm, d), lambda i: (i, 0)),
        out_shape=jax.ShapeDtypeStruct((m, d), out_dtype),
        compiler_params=_params(("arbitrary",)),
        name="normcast",
    )(x, g.reshape(1, d))


def _cast_rows(src_ref, dst_ref, rows_per_iter=256):
    k = src_ref.shape[0]
    step = min(rows_per_iter, k)

    def body(i, carry):
        r0 = pl.multiple_of(i * step, step)
        dst_ref[pl.ds(r0, step), :] = src_ref[pl.ds(r0, step), :].astype(dst_ref.dtype)
        return carry

    lax.fori_loop(0, k // step, body, 0)


def _mm_kernel(*refs, n_a, has_res):
    a_refs = refs[:n_a]
    w_ref = refs[n_a]
    res_ref = refs[n_a + 1] if has_res else None
    o_ref = refs[n_a + 1 + int(has_res)]
    wbf_ref = refs[n_a + 2 + int(has_res)]

    @pl.when(pl.program_id(1) == 0)
    def _():
        _cast_rows(w_ref, wbf_ref)

    kp = a_refs[0].shape[1]
    acc = None
    for p, a_ref in enumerate(a_refs):
        d = jnp.dot(a_ref[...], wbf_ref[p * kp:(p + 1) * kp, :], preferred_element_type=F32)
        acc = d if acc is None else acc + d
    if has_res:
        acc = acc + res_ref[...]
    o_ref[...] = acc.astype(o_ref.dtype)


def matmul(a_parts, w, out_dtype, res=None, tm=MM_TM, tn=MM_TN):
    m, kp = a_parts[0].shape
    k, n = w.shape
    assert kp * len(a_parts) == k
    tm = min(tm, m)
    tn = min(tn, n)
    in_specs = [pl.BlockSpec((tm, kp), lambda j, i: (i, 0)) for _ in a_parts]
    in_specs.append(pl.BlockSpec((k, tn), lambda j, i: (0, j)))
    args = list(a_parts) + [w]
    if res is not None:
        in_specs.append(pl.BlockSpec((tm, tn), lambda j, i: (i, j)))
        args.append(res)
    return pl.pallas_call(
        functools.partial(_mm_kernel, n_a=len(a_parts), has_res=res is not None),
        grid=(n // tn, m // tm),
        in_specs=in_specs,
        out_specs=pl.BlockSpec((tm, tn), lambda j, i: (i, j)),
        out_shape=jax.ShapeDtypeStruct((m, n), out_dtype),
        scratch_shapes=[pltpu.VMEM((k, tn), BF16)],
        compiler_params=_params(("arbitrary", "arbitrary")),
        name="matmul",
    )(*args)


def _rope_kernel(pos_ref, invf_ref, cos_ref, sin_ref):
    ang = pos_ref[...] * invf_ref[...]
    cos_ref[...] = jnp.cos(ang)
    sin_ref[...] = jnp.sin(ang)


def rope_tables(pos_f, inv_freq, tm=512):
    s = pos_f.shape[0]
    hd = inv_freq.shape[0]
    tm = min(tm, s)
    return pl.pallas_call(
        _rope_kernel,
        grid=(s // tm,),
        in_specs=[pl.BlockSpec((tm, 1), lambda i: (i, 0)),
                  pl.BlockSpec((1, hd), lambda i: (0, 0))],
        out_specs=[pl.BlockSpec((tm, hd), lambda i: (i, 0))] * 2,
        out_shape=[jax.ShapeDtypeStruct((s, hd), F32)] * 2,
        compiler_params=_params(("arbitrary",)),
        name="rope_tables",
    )(pos_f.reshape(s, 1), inv_freq.reshape(1, hd))


def _ret_kernel(q_ref, k_ref, v_ref, g_ref, cos_ref, sin_ref, lg_ref, gn_ref, o_ref, r_ref,
                decay_ref, xi_ref, zeta_ref, *, n_chunks, hpb):
    c = RET_CHUNK
    dk = RET_HEAD_DIM
    half = dk // 2
    scale = dk ** -0.5

    @pl.when(pl.program_id(1) == 0)
    def _():
        r_ref[...] = jnp.zeros_like(r_ref)
        row = lax.broadcasted_iota(jnp.int32, (c, c), 0).astype(F32)
        col = lax.broadcasted_iota(jnp.int32, (c, c), 1).astype(F32)
        diff = row - col
        rowk = lax.broadcasted_iota(jnp.int32, (c, dk), 0).astype(F32)
        for hh in range(hpb):
            lg = lg_ref[hh]
            decay_ref[hh] = jnp.where(diff >= 0, jnp.exp(lg[:, :c] * jnp.maximum(diff, 0.0)), 0.0)
            xi_ref[hh] = jnp.exp(lg * (rowk + 1.0))
            zeta_ref[hh] = jnp.exp(lg * (c - 1.0 - rowk))

    def rope(t, cos, sin):
        t1 = t[:, :half]
        t2 = t[:, half:]
        return jnp.concatenate([t1 * cos - t2 * sin, t1 * sin + t2 * cos], axis=-1)

    def body(j, carry):
        r0 = pl.multiple_of(j * c, c)
        rows = pl.ds(r0, c)
        cos = cos_ref[rows, :]
        sin = sin_ref[rows, :]
        for hh in range(hpb):
            cs = slice(hh * dk, (hh + 1) * dk)
            qr = rope(q_ref[rows, cs], cos, sin)
            kr = rope(k_ref[rows, cs], cos, sin) * scale
            qb = qr.astype(BF16)
            kb = kr.astype(BF16)
            vb = v_ref[rows, cs].astype(BF16)
            state = r_ref[hh]
            inner = lax.dot_general(qb, kb, (((1,), (1,)), ((), ())),
                                    preferred_element_type=F32) * decay_ref[hh]
            o = (jnp.dot(inner.astype(BF16), vb, preferred_element_type=F32)
                 + jnp.dot(qb, state.astype(BF16), preferred_element_type=F32) * xi_ref[hh])
            kz = (kr * zeta_ref[hh]).astype(BF16)
            chunk_decay = jnp.exp(lg_ref[hh] * c)
            r_ref[hh] = state * chunk_decay + lax.dot_general(
                kz, vb, (((0,), (0,)), ((), ())), preferred_element_type=F32)
            mu = jnp.mean(o, axis=-1, keepdims=True)
            oc = o - mu
            var = jnp.mean(oc * oc, axis=-1, keepdims=True)
            on = oc * lax.rsqrt(var + GN_EPS) * gn_ref[hh]
            g = g_ref[rows, cs]
            o_ref[rows, cs] = (on * (g * (1.0 / (1.0 + jnp.exp(-g))))).astype(o_ref.dtype)
        return carry

    lax.fori_loop(0, n_chunks, body, 0)


def retention(proj, cos, sin, lg_rows, gn_g, tr=RET_ROWS, hpb=RET_HEADS_PER_STEP):
    s = proj.shape[0]
    dk = RET_HEAD_DIM
    h = RET_HEADS
    tr = min(tr, s)
    c = RET_CHUNK
    w = hpb * dk

    def col(base):
        return pl.BlockSpec((tr, w), lambda hg, ci, base=base: (ci, base // hpb + hg))

    per_head = pl.BlockSpec((hpb, 1, dk), lambda hg, ci: (hg, 0, 0))
    return pl.pallas_call(
        functools.partial(_ret_kernel, n_chunks=tr // c, hpb=hpb),
        grid=(h // hpb, s // tr),
        in_specs=[col(0), col(h), col(2 * h), col(3 * h),
                  pl.BlockSpec((tr, dk // 2), lambda hg, ci: (ci, 0)),
                  pl.BlockSpec((tr, dk // 2), lambda hg, ci: (ci, 0)),
                  per_head, per_head],
        out_specs=pl.BlockSpec((tr, w), lambda hg, ci: (ci, hg)),
        out_shape=jax.ShapeDtypeStruct((s, h * dk), BF16),
        scratch_shapes=[pltpu.VMEM((hpb, dk, dk), F32),
                        pltpu.VMEM((hpb, c, c), F32),
                        pltpu.VMEM((hpb, c, dk), F32),
                        pltpu.VMEM((hpb, c, dk), F32)],
        compiler_params=_params(("arbitrary", "arbitrary")),
        name="retention",
    )(proj, proj, proj, proj, cos, sin, lg_rows, gn_g.reshape(h, 1, dk))


def _sigmoid(x):
    return 1.0 / (1.0 + jnp.exp(-x))


def _lru_kernel(xb_ref, gb_ref, cw_ref, cb_ref, wa_ref, ba_ref, wi_ref, bi_ref, lam_ref, og_ref,
                o_ref, tail_ref, xs_ref, hs_ref, h_ref, wabf_ref, wibf_ref):
    tr, cdim = xb_ref.shape
    nb = wa_ref.shape[0]
    bd = cdim // nb
    ph = SUBLANES
    ng = tr // ph

    @pl.when(pl.program_id(0) == 0)
    def _():
        tail_ref[...] = jnp.zeros_like(tail_ref)
        h_ref[...] = jnp.zeros_like(h_ref)
        wabf_ref[...] = wa_ref[...].astype(BF16)
        wibf_ref[...] = wi_ref[...].astype(BF16)

    lam = lam_ref[...]
    sp = jnp.maximum(-lam, 0.0) + jnp.log1p(jnp.exp(-jnp.abs(lam)))
    rowg = lax.broadcasted_iota(jnp.int32, (ng, bd), 0)
    lpb = bd // LANES
    for c in range(cdim // LANES):
        xs_ref[c] = xb_ref[:, c * LANES:(c + 1) * LANES]

    def phase_rows(ref, n, p):
        return jnp.concatenate([ref[n * lpb + c, pl.ds(p, ng, stride=ph), :] for c in range(lpb)],
                               axis=1)

    for n in range(nb):
        cs = slice(n * bd, (n + 1) * bd)
        x = [phase_rows(xs_ref, n, p) for p in range(ph)]

        def prev_group(p):
            return jnp.where(rowg == 0, tail_ref[p:p + 1, cs], pltpu.roll(x[p], 1, 0))

        back = {-k: prev_group(ph - k) for k in range(1, CONV_WIDTH)}

        def xat(p):
            return x[p] if p >= 0 else back[p]

        xc = []
        for p in range(ph):
            acc = cb_ref[:, cs] + cw_ref[CONV_WIDTH - 1:CONV_WIDTH, cs] * xat(p)
            for k in range(1, CONV_WIDTH):
                acc = acc + cw_ref[CONV_WIDTH - 1 - k:CONV_WIDTH - k, cs] * xat(p - k)
            xc.append(acc)
        xg = jnp.concatenate(xc, axis=0)
        xgb = xg.astype(BF16)
        r = _sigmoid(jnp.dot(xgb, wabf_ref[n], preferred_element_type=F32) + ba_ref[:, cs])
        ig = _sigmoid(jnp.dot(xgb, wibf_ref[n], preferred_element_type=F32) + bi_ref[:, cs])
        log_a = (-RG_C * r) * sp[:, cs]
        a = jnp.exp(log_a)
        b = jnp.sqrt(-jnp.tanh(log_a) * (a * a + 1.0)) * (ig * xg)

        cum_a = [a[0:ng]]
        cum_b = [b[0:ng]]
        for p in range(1, ph):
            ap = a[p * ng:(p + 1) * ng]
            cum_b.append(ap * cum_b[-1] + b[p * ng:(p + 1) * ng])
            cum_a.append(ap * cum_a[-1])
        sa, sb = cum_a[-1], cum_b[-1]
        d = 1
        while d < ng:
            keep = rowg >= d
            sa_sh = pltpu.roll(sa, d, 0)
            sb_sh = pltpu.roll(sb, d, 0)
            sb = jnp.where(keep, sa * sb_sh + sb, sb)
            sa = jnp.where(keep, sa * sa_sh, sa)
            d *= 2
        h_in = h_ref[:, cs]
        h_end = sa * h_in + sb
        h_prev = jnp.where(rowg == 0, h_in, pltpu.roll(h_end, 1, 0))
        for p in range(ph):
            hp = cum_a[p] * h_prev + cum_b[p]
            for c in range(lpb):
                hs_ref[n * lpb + c, pl.ds(p, ng, stride=ph), :] = hp[:, c * LANES:(c + 1) * LANES]
        h_ref[:, cs] = h_end[ng - 1:ng, :]

    gb = gb_ref[...]
    gelu = 0.5 * gb * (1.0 + jnp.tanh(0.7978845608028654 * (gb + 0.044715 * (gb * gb * gb))))
    y = jnp.concatenate([hs_ref[c] for c in range(cdim // LANES)], axis=1) * gelu
    ms = jnp.mean(y * y, axis=-1, keepdims=True)
    o_ref[...] = (y * lax.rsqrt(ms + NORM_EPS) * og_ref[...]).astype(o_ref.dtype)
    tail_ref[...] = xb_ref[tr - ph:tr, :]


def rg_lru(proj, xb_block, gb_block, conv_w, conv_b, w_a, b_a, w_i, b_i, lam, out_g, tr=LRU_ROWS):
    s = proj.shape[0]
    cdim = conv_w.shape[1]
    nb, bd, _ = w_a.shape
    tr = min(tr, s)
    vec = pl.BlockSpec((1, cdim), lambda i: (0, 0))
    wspec = pl.BlockSpec((nb, bd, bd), lambda i: (0, 0, 0))
    return pl.pallas_call(
        _lru_kernel,
        grid=(s // tr,),
        in_specs=[pl.BlockSpec((tr, cdim), lambda i: (i, xb_block)),
                  pl.BlockSpec((tr, cdim), lambda i: (i, gb_block)),
                  pl.BlockSpec((CONV_WIDTH, cdim), lambda i: (0, 0)),
                  vec, wspec, vec, wspec, vec, vec, vec],
        out_specs=pl.BlockSpec((tr, cdim), lambda i: (i, 0)),
        out_shape=jax.ShapeDtypeStruct((s, cdim), BF16),
        scratch_shapes=[pltpu.VMEM((SUBLANES, cdim), F32),
                        pltpu.VMEM((cdim // LANES, tr, LANES), F32),
                        pltpu.VMEM((cdim // LANES, tr, LANES), F32),
                        pltpu.VMEM((1, cdim), F32),
                        pltpu.VMEM((nb, bd, bd), BF16),
                        pltpu.VMEM((nb, bd, bd), BF16)],
        compiler_params=_params(("arbitrary",)),
        name="rg_lru",
    )(proj, proj, conv_w, conv_b.reshape(1, cdim), w_a, b_a.reshape(1, cdim), w_i,
      b_i.reshape(1, cdim), lam.reshape(1, cdim), out_g.reshape(1, cdim))


def _wqk_kernel(wq_ref, k_ref, o_ref):
    o_ref[...] = lax.dot_general(wq_ref[...].astype(BF16), k_ref[...], (((1,), (1,)), ((), ())),
                                 preferred_element_type=F32).astype(o_ref.dtype)


def _vo_kernel(v_ref, wo_ref, o_ref):
    o_ref[...] = jnp.dot(v_ref[...], wo_ref[...].astype(BF16),
                         preferred_element_type=F32).astype(o_ref.dtype)


def xattn_fold(k, v, wq, wo, tile=XF_TILE):
    mlen, d = k.shape
    hd = d // X_HEADS
    tile = min(tile, d)
    wqk = pl.pallas_call(
        _wqk_kernel,
        grid=(X_HEADS, d // tile),
        in_specs=[pl.BlockSpec((tile, hd), lambda h, r: (r, h)),
                  pl.BlockSpec((mlen, hd), lambda h, r: (0, h))],
        out_specs=pl.BlockSpec((tile, mlen), lambda h, r: (r, h)),
        out_shape=jax.ShapeDtypeStruct((d, X_HEADS * mlen), BF16),
        compiler_params=_params(("arbitrary", "arbitrary")),
        name="xattn_wqk",
    )(wq, k)
    vo = pl.pallas_call(
        _vo_kernel,
        grid=(X_HEADS, d // tile),
        in_specs=[pl.BlockSpec((mlen, hd), lambda h, j: (0, h)),
                  pl.BlockSpec((hd, tile), lambda h, j: (h, j))],
        out_specs=pl.BlockSpec((mlen, tile), lambda h, j: (h, j)),
        out_shape=jax.ShapeDtypeStruct((X_HEADS * mlen, d), BF16),
        compiler_params=_params(("arbitrary", "arbitrary")),
        name="xattn_vo",
    )(v, wo)
    return wqk, vo


def _xattn_kernel(x_ref, g_ref, wqk_ref, vo_ref, o_ref):
    d = x_ref.shape[1]
    mlen = wqk_ref.shape[1] // X_HEADS
    scale = (d // X_HEADS) ** -0.5
    x = x_ref[...]
    ms = jnp.mean(x * x, axis=-1, keepdims=True)
    h = (x * lax.rsqrt(ms + NORM_EPS) * g_ref[...]).astype(BF16)
    s = jnp.dot(h, wqk_ref[...], preferred_element_type=F32) * scale
    ps = []
    for hh in range(X_HEADS):
        sh = s[:, hh * mlen:(hh + 1) * mlen]
        m = jnp.max(sh, axis=-1, keepdims=True)
        e = jnp.exp(sh - m)
        ps.append((e / jnp.sum(e, axis=-1, keepdims=True)).astype(BF16))
    p = jnp.concatenate(ps, axis=1)
    o_ref[...] = x + jnp.dot(p, vo_ref[...], preferred_element_type=F32)


def xattn(x, g, wqk, vo, tm=ATT_ROWS):
    s, d = x.shape
    tm = min(tm, s)
    return pl.pallas_call(
        _xattn_kernel,
        grid=(s // tm,),
        in_specs=[pl.BlockSpec((tm, d), lambda i: (i, 0)),
                  pl.BlockSpec((1, d), lambda i: (0, 0)),
                  pl.BlockSpec(wqk.shape, lambda i: (0, 0)),
                  pl.BlockSpec(vo.shape, lambda i: (0, 0))],
        out_specs=pl.BlockSpec((tm, d), lambda i: (i, 0)),
        out_shape=jax.ShapeDtypeStruct((s, d), F32),
        compiler_params=_params(("arbitrary",)),
        name="xattn",
    )(x, g.reshape(1, d), wqk, vo)


def _split_bf16(a):
    hi = a.astype(BF16)
    return hi, (a - hi.astype(F32)).astype(BF16)


def _router_kernel(x_ref, g_ref, wr_ref, br_ref, route_ref, counts_ref, carry_ref, wsplit_ref):
    tm = x_ref.shape[0]
    nl = ROUTE_LANES

    @pl.when(pl.program_id(0) == 0)
    def _():
        carry_ref[...] = jnp.zeros_like(carry_ref)
        w_hi, w_lo = _split_bf16(wr_ref[...])
        wsplit_ref[:, :nl] = w_hi
        wsplit_ref[:, nl:] = w_lo

    x = x_ref[...]
    ms = jnp.mean(x * x, axis=-1, keepdims=True)
    h = x * lax.rsqrt(ms + NORM_EPS) * g_ref[...]
    h_hi, h_lo = _split_bf16(h)
    both = jnp.dot(h_hi, wsplit_ref[...], preferred_element_type=F32)
    cross = jnp.dot(h_lo, wsplit_ref[:, :nl], preferred_element_type=F32)
    logits = both[:, :nl] + (both[:, nl:] + cross) + br_ref[...]
    lane = lax.broadcasted_iota(jnp.int32, (tm, ROUTE_LANES), 1)
    neg = -jnp.inf
    big = ROUTE_LANES

    gmask = (lane >= GROUP_LANE0) & (lane < GROUP_LANE0 + N_GROUPS)
    gl = jnp.where(gmask, logits, neg)
    gmax = jnp.max(gl, axis=-1, keepdims=True)
    gsum = jnp.sum(jnp.where(gmask, jnp.exp(gl - gmax), 0.0), axis=-1, keepdims=True)
    g_val = 1.0 / gsum
    g_idx = jnp.min(jnp.where(gl == gmax, lane, big), axis=-1, keepdims=True) - GROUP_LANE0

    lo = EXPERT_LANE0 + g_idx * EXPERTS_PER_GROUP
    emask = (lane >= lo) & (lane < lo + EXPERTS_PER_GROUP)
    el = jnp.where(emask, logits, neg)
    t1 = jnp.max(el, axis=-1, keepdims=True)
    i1 = jnp.min(jnp.where(emask & (el == t1), lane, big), axis=-1, keepdims=True)
    emask2 = emask & (lane != i1)
    el2 = jnp.where(emask2, logits, neg)
    t2 = jnp.max(el2, axis=-1, keepdims=True)
    i2 = jnp.min(jnp.where(emask2 & (el2 == t2), lane, big), axis=-1, keepdims=True)
    dexp = jnp.exp(t2 - t1)
    w0 = g_val / (1.0 + dexp)
    w1 = g_val * dexp / (1.0 + dexp)

    sel1 = lane == i1
    sel2 = lane == i2
    onehot = jnp.where(sel1 | sel2, 1.0, 0.0)
    rr = lax.broadcasted_iota(jnp.int32, (tm, tm), 0)
    cc = lax.broadcasted_iota(jnp.int32, (tm, tm), 1)
    tri = jnp.where(cc < rr, 1.0, 0.0).astype(BF16)
    prefix = jnp.dot(tri, onehot.astype(BF16), preferred_element_type=F32) + carry_ref[...]
    rank0 = jnp.sum(jnp.where(sel1, prefix, 0.0), axis=-1, keepdims=True)
    rank1 = jnp.sum(jnp.where(sel2, prefix, 0.0), axis=-1, keepdims=True)
    total = carry_ref[...] + jnp.sum(onehot, axis=0, keepdims=True)
    carry_ref[...] = total
    counts_ref[...] = total

    e0 = (i1 - EXPERT_LANE0).astype(F32)
    e1 = (i2 - EXPERT_LANE0).astype(F32)
    route = jnp.where(lane == 0, e0, 0.0)
    route = jnp.where(lane == 1, e1, route)
    route = jnp.where(lane == 2, w0, route)
    route = jnp.where(lane == 3, w1, route)
    route = jnp.where(lane == 4, rank0, route)
    route = jnp.where(lane == 5, rank1, route)
    route_ref[...] = route


def router(x, g, wr, br, tm=ROUTE_ROWS):
    t, d = x.shape
    tm = min(tm, t)
    return pl.pallas_call(
        _router_kernel,
        grid=(t // tm,),
        in_specs=[pl.BlockSpec((tm, d), lambda i: (i, 0)),
                  pl.BlockSpec((1, d), lambda i: (0, 0)),
                  pl.BlockSpec((d, ROUTE_LANES), lambda i: (0, 0)),
                  pl.BlockSpec((1, ROUTE_LANES), lambda i: (0, 0))],
        out_specs=[pl.BlockSpec((tm, ROUTE_LANES), lambda i: (i, 0)),
                   pl.BlockSpec((1, ROUTE_LANES), lambda i: (0, 0))],
        out_shape=[jax.ShapeDtypeStruct((t, ROUTE_LANES), F32),
                   jax.ShapeDtypeStruct((1, ROUTE_LANES), F32)],
        scratch_shapes=[pltpu.VMEM((1, ROUTE_LANES), F32),
                        pltpu.VMEM((d, 2 * ROUTE_LANES), BF16)],
        compiler_params=_params(("arbitrary",)),
        name="router",
    )(x, g.reshape(1, d), wr, br)


def _gather_kernel(e0_ref, e1_ref, r0_ref, r1_ref, ps_ref, nu_ref, g_ref, x_hbm, xs_ref,
                   rowtok_ref, buf_ref, sem, *, blk, n_tok):
    i = pl.program_id(0)
    nu = nu_ref[0]

    def row_copy(b, slot, r):
        tok = rowtok_ref[b * blk + r]
        return pltpu.make_async_copy(x_hbm.at[pl.ds(tok, 1), :],
                                     buf_ref.at[slot, pl.ds(r, 1), :], sem.at[slot])

    def start_block(b, slot):
        def body(r, carry):
            row_copy(b, slot, r).start()
            return carry
        lax.fori_loop(0, blk, body, 0, unroll=SCALAR_UNROLL)

    def wait_block(b, slot):
        def body(r, carry):
            row_copy(b, slot, r).wait()
            return carry
        lax.fori_loop(0, blk, body, 0, unroll=SCALAR_UNROLL)

    @pl.when(i == 0)
    def _():
        n_rows = rowtok_ref.shape[0]
        for base in range(0, n_rows, n_tok):
            def init(r, carry, base=base):
                rowtok_ref[base + r] = r
                return carry
            lax.fori_loop(0, min(n_tok, n_rows - base), init, 0, unroll=SCALAR_UNROLL)

        def fill(t, carry):
            rowtok_ref[ps_ref[e0_ref[t]] + r0_ref[t]] = t
            rowtok_ref[ps_ref[e1_ref[t]] + r1_ref[t]] = t
            return carry
        lax.fori_loop(0, n_tok, fill, 0, unroll=SCALAR_UNROLL)
        start_block(0, 0)

    @pl.when(i + 1 < nu)
    def _():
        start_block(i + 1, (i + 1) % 2)

    @pl.when(i < nu)
    def _():
        slot = i % 2
        wait_block(i, slot)
        x = buf_ref[slot]
        ms = jnp.mean(x * x, axis=-1, keepdims=True)
        xs_ref[...] = (x * lax.rsqrt(ms + NORM_EPS) * g_ref[...]).astype(xs_ref.dtype)

    @pl.when(i >= nu)
    def _():
        xs_ref[...] = jnp.zeros_like(xs_ref)


def moe_gather(x, g, slots, n_used, n_blocks, blk):
    t, d = x.shape
    grid_spec = pltpu.PrefetchScalarGridSpec(
        num_scalar_prefetch=6,
        grid=(n_blocks,),
        in_specs=[pl.BlockSpec((1, d), lambda i, *_: (0, 0)),
                  pl.BlockSpec(memory_space=pl.ANY)],
        out_specs=pl.BlockSpec((blk, d), lambda i, *_: (i, 0)),
        scratch_shapes=[pltpu.SMEM((n_blocks * blk,), jnp.int32),
                        pltpu.VMEM((2, blk, d), F32),
                        pltpu.SemaphoreType.DMA((2,))],
    )
    return pl.pallas_call(
        functools.partial(_gather_kernel, blk=blk, n_tok=t),
        grid_spec=grid_spec,
        out_shape=jax.ShapeDtypeStruct((n_blocks * blk, d), BF16),
        compiler_params=_params(("arbitrary",)),
        name="moe_gather",
    )(*slots, n_used, g.reshape(1, d), x)


def _expert_changed(be_ref, i):
    prev = be_ref[jnp.maximum(i - 1, 0)]
    return (i == 0) | (be_ref[i] != prev)


def _stream_expert_weights(w_hbms, col0, be_ref, nx_ref, nu, i, wst_ref, wbf_ref, sem, slot_ref):
    tn = wst_ref.shape[-1]

    def copies(e, slot):
        return [pltpu.make_async_copy(w.at[e, :, pl.ds(col0, tn)], wst_ref.at[slot, l], sem.at[slot])
                for l, w in enumerate(w_hbms)]

    @pl.when(i == 0)
    def _():
        slot_ref[0] = 0
        for c in copies(be_ref[0], 0):
            c.start()

    @pl.when(_expert_changed(be_ref, i))
    def _():
        slot = slot_ref[0]
        for c in copies(be_ref[i], slot):
            c.wait()
        nxt = nx_ref[i]

        @pl.when(nxt < nu)
        def _():
            for c in copies(be_ref[jnp.minimum(nxt, be_ref.shape[0] - 1)], 1 - slot):
                c.start()

        for l in range(len(w_hbms)):
            _cast_rows(wst_ref.at[slot, l], wbf_ref.at[l])
        slot_ref[0] = 1 - slot


def _moe_up_kernel(be_ref, nx_ref, nu_ref, xs_ref, wg_hbm, wu_hbm, act_ref,
                   wst_ref, wbf_ref, sem, slot_ref):
    j = pl.program_id(0)
    i = pl.program_id(1)
    nu = nu_ref[0]
    tf = act_ref.shape[1]

    @pl.when(i < nu)
    def _():
        _stream_expert_weights([wg_hbm, wu_hbm], pl.multiple_of(j * tf, tf), be_ref, nx_ref, nu, i,
                               wst_ref, wbf_ref, sem, slot_ref)
        x = xs_ref[...]
        gate = jnp.dot(x, wbf_ref[0], preferred_element_type=F32)
        up = jnp.dot(x, wbf_ref[1], preferred_element_type=F32)
        act_ref[...] = (gate * _sigmoid(gate) * up).astype(act_ref.dtype)

    @pl.when(i >= nu)
    def _():
        act_ref[...] = jnp.zeros_like(act_ref)


def _moe_down_kernel(be_ref, nx_ref, nu_ref, act_ref, wd_hbm, y_ref, wst_ref, wbf_ref, sem, slot_ref):
    j = pl.program_id(0)
    i = pl.program_id(1)
    nu = nu_ref[0]
    tn = y_ref.shape[1]

    @pl.when(i < nu)
    def _():
        _stream_expert_weights([wd_hbm], pl.multiple_of(j * tn, tn), be_ref, nx_ref, nu, i,
                               wst_ref, wbf_ref, sem, slot_ref)
        y_ref[...] = jnp.dot(act_ref[...], wbf_ref[0], preferred_element_type=F32)

    @pl.when(i >= nu)
    def _():
        y_ref[...] = jnp.zeros_like(y_ref)


def moe_experts(xs, block_expert, next_expert_block, n_used, w_gate, w_up, w_down,
                blk=MOE_BLK, tf=MOE_TF, tn=MOE_TN):
    r, dw = xs.shape
    _, d, f = w_gate.shape
    n_blocks = r // blk
    tf = min(tf, f)
    tn = min(tn, d)

    def used(i, nu):
        return jnp.minimum(i, jnp.maximum(nu[0] - 1, 0))

    def stream_scratch(n_mats, k, n):
        return [pltpu.VMEM((2, n_mats, k, n), F32),
                pltpu.VMEM((n_mats, k, n), BF16),
                pltpu.SemaphoreType.DMA((2,)),
                pltpu.SMEM((1,), jnp.int32)]

    up_spec = pltpu.PrefetchScalarGridSpec(
        num_scalar_prefetch=3,
        grid=(f // tf, n_blocks),
        in_specs=[pl.BlockSpec((blk, dw), lambda j, i, be, nx, nu: (used(i, nu), 0)),
                  pl.BlockSpec(memory_space=pl.ANY),
                  pl.BlockSpec(memory_space=pl.ANY)],
        out_specs=pl.BlockSpec((blk, tf), lambda j, i, be, nx, nu: (i, j)),
        scratch_shapes=stream_scratch(2, d, tf),
    )
    act = pl.pallas_call(
        _moe_up_kernel,
        grid_spec=up_spec,
        out_shape=jax.ShapeDtypeStruct((r, f), BF16),
        compiler_params=_params(("arbitrary", "arbitrary")),
        name="moe_up",
    )(block_expert, next_expert_block, n_used, xs, w_gate, w_up)
    down_spec = pltpu.PrefetchScalarGridSpec(
        num_scalar_prefetch=3,
        grid=(d // tn, n_blocks),
        in_specs=[pl.BlockSpec((blk, f), lambda j, i, be, nx, nu: (used(i, nu), 0)),
                  pl.BlockSpec(memory_space=pl.ANY)],
        out_specs=pl.BlockSpec((blk, tn), lambda j, i, be, nx, nu: (i, j)),
        scratch_shapes=stream_scratch(1, f, tn),
    )
    return pl.pallas_call(
        _moe_down_kernel,
        grid_spec=down_spec,
        out_shape=jax.ShapeDtypeStruct((r, d), F32),
        compiler_params=_params(("arbitrary", "arbitrary")),
        name="moe_down",
    )(block_expert, next_expert_block, n_used, act, w_down)


def _combine_kernel(e0_ref, e1_ref, r0_ref, r1_ref, ps_ref, x_ref, route_ref, g_ref, y_hbm, o_ref,
                    ya_ref, yb_ref, sem, *, tb, final_norm):
    step = pl.program_id(0)

    def copies(b, slot, i):
        t = b * tb + i
        row0 = ps_ref[e0_ref[t]] + r0_ref[t]
        row1 = ps_ref[e1_ref[t]] + r1_ref[t]
        return (pltpu.make_async_copy(y_hbm.at[pl.ds(row0, 1), :],
                                      ya_ref.at[slot, pl.ds(i, 1), :], sem.at[slot]),
                pltpu.make_async_copy(y_hbm.at[pl.ds(row1, 1), :],
                                      yb_ref.at[slot, pl.ds(i, 1), :], sem.at[slot]))

    def start_block(b, slot):
        def body(i, carry):
            c0, c1 = copies(b, slot, i)
            c0.start()
            c1.start()
            return carry
        lax.fori_loop(0, tb, body, 0, unroll=SCALAR_UNROLL)

    def wait_block(b, slot):
        def body(i, carry):
            c0, c1 = copies(b, slot, i)
            c0.wait()
            c1.wait()
            return carry
        lax.fori_loop(0, tb, body, 0, unroll=SCALAR_UNROLL)

    @pl.when(step == 0)
    def _():
        start_block(0, 0)

    @pl.when(step + 1 < pl.num_programs(0))
    def _():
        start_block(step + 1, (step + 1) % 2)

    slot = step % 2
    wait_block(step, slot)
    w0 = route_ref[:, 2:3]
    w1 = route_ref[:, 3:4]
    x = x_ref[...] + (ya_ref[slot] * w0 + yb_ref[slot] * w1)
    if final_norm:
        ms = jnp.mean(x * x, axis=-1, keepdims=True)
        x = x * lax.rsqrt(ms + NORM_EPS) * g_ref[...]
    o_ref[...] = x


def combine(x, route, y, slots, g, final_norm, tb=COMB_ROWS):
    t, d = x.shape
    tb = min(tb, t)
    grid_spec = pltpu.PrefetchScalarGridSpec(
        num_scalar_prefetch=5,
        grid=(t // tb,),
        in_specs=[pl.BlockSpec((tb, d), lambda i, *_: (i, 0)),
                  pl.BlockSpec((tb, ROUTE_LANES), lambda i, *_: (i, 0)),
                  pl.BlockSpec((1, d), lambda i, *_: (0, 0)),
                  pl.BlockSpec(memory_space=pl.ANY)],
        out_specs=pl.BlockSpec((tb, d), lambda i, *_: (i, 0)),
        scratch_shapes=[pltpu.VMEM((2, tb, d), F32), pltpu.VMEM((2, tb, d), F32),
                        pltpu.SemaphoreType.DMA((2,))],
    )
    return pl.pallas_call(
        functools.partial(_combine_kernel, tb=tb, final_norm=final_norm),
        grid_spec=grid_spec,
        out_shape=jax.ShapeDtypeStruct((t, d), F32),
        compiler_params=_params(("arbitrary",)),
        name="moe_combine",
    )(*slots, x, route, g.reshape(1, d), y)


def _route_lanes(group_part, expert_part):
    rows = group_part.shape[0]
    gap = jnp.zeros((rows, EXPERT_LANE0 - GROUP_LANE0 - N_GROUPS), F32)
    tail = jnp.zeros((rows, ROUTE_LANES - EXPERT_LANE0 - N_EXPERTS), F32)
    return jnp.concatenate([group_part, gap, expert_part, tail], axis=1)


def _moe_layout(route, counts, blk):
    t = route.shape[0]
    ri = route[:, :8].astype(jnp.int32)
    e0, e1, rank0, rank1 = ri[:, 0], ri[:, 1], ri[:, 4], ri[:, 5]
    cnt = counts[0, EXPERT_LANE0:EXPERT_LANE0 + N_EXPERTS].astype(jnp.int32)
    padded = (cnt + blk - 1) // blk * blk
    pends = jnp.cumsum(padded)
    pstarts = pends - padded
    n_blocks = (2 * t) // blk + N_EXPERTS
    block_start = jnp.arange(n_blocks, dtype=jnp.int32) * blk
    block_expert = jnp.minimum(
        jnp.sum((block_start[:, None] >= pends[None, :]).astype(jnp.int32), axis=1), N_EXPERTS - 1)
    n_used = (pends[-1] // blk).astype(jnp.int32).reshape(1)
    block_expert = block_expert[jnp.minimum(jnp.arange(n_blocks), jnp.maximum(n_used[0] - 1, 0))]
    next_expert_block = pends[block_expert] // blk
    return (e0, e1, rank0, rank1, pstarts), block_expert, next_expert_block, n_used, n_blocks


def kernel(x, mem, positions, mix_norm_g, w_in, ret_norm_g, lru_conv_w, lru_conv_b, lru_w_a, lru_b_a, lru_w_i, lru_b_i, lru_lambda, lru_norm_g, w_out, xattn_norm_g, mem_norm_g, xattn_wq, xattn_wk, xattn_wv, xattn_wo, moe_norm_g, router_group_w, router_group_b, router_expert_w, router_expert_b, expert_w_gate, expert_w_up, expert_w_down, final_norm_g):
    b, s, d = x.shape
    depth = w_in.shape[0]
    ret_width = RET_HEADS * RET_HEAD_DIM
    lru_width = lru_conv_w.shape[-1]
    assert ret_width == lru_width and ret_width + lru_width == d
    inv_freq = ROPE_BASE ** (-jnp.arange(0, RET_HEAD_DIM, 2, dtype=F32) / RET_HEAD_DIM)
    lg = jnp.log1p(-jnp.exp2(-5.0 - jnp.arange(RET_HEADS, dtype=F32)))
    lg_rows = jnp.broadcast_to(lg[:, None, None], (RET_HEADS, 1, RET_HEAD_DIM))
    blk = min(MOE_BLK, s)
    outs = []
    for bi in range(b):
        xcur = x[bi]
        cos, sin = rope_tables(positions[bi].astype(F32), inv_freq)
        for l in range(depth):
            h = normcast(xcur, mix_norm_g[l], BF16, NORM_ROWS)
            proj = matmul([h], w_in[l], F32)
            ret = retention(proj, cos, sin, lg_rows, ret_norm_g[l])
            lru = rg_lru(proj, 4 * ret_width // lru_width, 4 * ret_width // lru_width + 1,
                         lru_conv_w[l], lru_conv_b[l], lru_w_a[l], lru_b_a[l], lru_w_i[l],
                         lru_b_i[l], lru_lambda[l], lru_norm_g[l])
            xcur = matmul([ret, lru], w_out[l], F32, res=xcur)
            memn = normcast(mem[bi], mem_norm_g[l], BF16, NORM_ROWS)
            kk = matmul([memn], xattn_wk[l], BF16)
            vv = matmul([memn], xattn_wv[l], BF16)
            wqk, vo = xattn_fold(kk, vv, xattn_wq[l], xattn_wo[l])
            xcur = xattn(xcur, xattn_norm_g[l], wqk, vo)
            wr = _route_lanes(router_group_w[l], router_expert_w[l])
            br = _route_lanes(router_group_b[l][None], router_expert_b[l][None])
            route, counts = router(xcur, moe_norm_g[l], wr, br)
            slots, block_expert, next_block, n_used, n_blocks = _moe_layout(route, counts, blk)
            xs = moe_gather(xcur, moe_norm_g[l], slots, n_used, n_blocks, blk)
            y = moe_experts(xs, block_expert, next_block, n_used, expert_w_gate[l], expert_w_up[l],
                            expert_w_down[l], blk=blk)
            xcur = combine(xcur, route, y, slots, final_norm_g, final_norm=l == depth - 1)
        outs.append(xcur)
    return outs[0][None] if b == 1 else jnp.stack(outs, axis=0)
```

```python
import functools

import jax
import jax.numpy as jnp
from jax import lax
from jax.experimental import pallas as pl
from jax.experimental.pallas import tpu as pltpu

F32 = jnp.float32
BF16 = jnp.bfloat16

RET_HEADS = 8
RET_HEAD_DIM = 256
RET_CHUNK = 128
LRU_BLOCKS = 8
CONV_WIDTH = 4
RG_C = 8.0
ROPE_BASE = 10000.0
X_HEADS = 4
N_GROUPS = 4
EXPERTS_PER_GROUP = 8
N_EXPERTS = N_GROUPS * EXPERTS_PER_GROUP
NORM_EPS = 1e-6
GN_EPS = 1e-5

LANES = 128
SUBLANES = 8
VMEM_LIMIT = 56 * 1024 * 1024

NORM_ROWS = 512
MM_TM = 1024
MM_TN = 512
MM_WIDE_TM = 512
MM_WIDE_TN = 1024
RET_ROWS = 512
RET_HEADS_PER_STEP = 8
LRU_ROWS = 256
ATT_ROWS = 512
ROUTE_ROWS = 256
MOE_BLK = 256
MOE_TF = 512
MOE_TN = 4096
XF_TILE = 1024
SCALAR_UNROLL = 8
COMB_ROWS = 256
ROUTE_LANES = LANES
GROUP_LANE0 = 0
EXPERT_LANE0 = 8


def _params(sem):
    return pltpu.CompilerParams(dimension_semantics=sem, vmem_limit_bytes=VMEM_LIMIT)


def _normcast_kernel(x_ref, g_ref, o_ref):
    x = x_ref[...]
    ms = jnp.mean(x * x, axis=-1, keepdims=True)
    o_ref[...] = (x * lax.rsqrt(ms + NORM_EPS) * g_ref[...]).astype(o_ref.dtype)


def normcast(x, g, out_dtype, tm):
    m, d = x.shape
    tm = min(tm, m)
    return pl.pallas_call(
        _normcast_kernel,
        grid=(m // tm,),
        in_specs=[pl.BlockSpec((tm, d), lambda i: (i, 0)),
                  pl.BlockSpec((1, d), lambda i: (0, 0))],
        out_specs=pl.BlockSpec((tm, d), lambda i: (i, 0)),
        out_shape=jax.ShapeDtypeStruct((m, d), out_dtype),
        compiler_params=_params(("arbitrary",)),
        name="normcast",
    )(x, g.reshape(1, d))


def _cast_rows(src_ref, dst_ref, rows_per_iter=256):
    k = src_ref.shape[0]
    step = min(rows_per_iter, k)

    def body(i, carry):
        r0 = pl.multiple_of(i * step, step)
        dst_ref[pl.ds(r0, step), :] = src_ref[pl.ds(r0, step), :].astype(dst_ref.dtype)
        return carry

    lax.fori_loop(0, k // step, body, 0)


def _mm_kernel(*refs, n_a, has_res):
    a_refs = refs[:n_a]
    w_ref = refs[n_a]
    res_ref = refs[n_a + 1] if has_res else None
    o_ref = refs[n_a + 1 + int(has_res)]
    wbf_ref = refs[n_a + 2 + int(has_res)]

    @pl.when(pl.program_id(1) == 0)
    def _():
        _cast_rows(w_ref, wbf_ref)

    kp = a_refs[0].shape[1]
    acc = None
    for p, a_ref in enumerate(a_refs):
        d = jnp.dot(a_ref[...], wbf_ref[p * kp:(p + 1) * kp, :], preferred_element_type=F32)
        acc = d if acc is None else acc + d
    if has_res:
        acc = acc + res_ref[...]
    o_ref[...] = acc.astype(o_ref.dtype)


def matmul(a_parts, w, out_dtype, res=None, tm=MM_TM, tn=MM_TN):
    m, kp = a_parts[0].shape
    k, n = w.shape
    assert kp * len(a_parts) == k
    tm = min(tm, m)
    tn = min(tn, n)
    in_specs = [pl.BlockSpec((tm, kp), lambda j, i: (i, 0)) for _ in a_parts]
    in_specs.append(pl.BlockSpec((k, tn), lambda j, i: (0, j)))
    args = list(a_parts) + [w]
    if res is not None:
        in_specs.append(pl.BlockSpec((tm, tn), lambda j, i: (i, j)))
        args.append(res)
    return pl.pallas_call(
        functools.partial(_mm_kernel, n_a=len(a_parts), has_res=res is not None),
        grid=(n // tn, m // tm),
        in_specs=in_specs,
        out_specs=pl.BlockSpec((tm, tn), lambda j, i: (i, j)),
        out_shape=jax.ShapeDtypeStruct((m, n), out_dtype),
        scratch_shapes=[pltpu.VMEM((k, tn), BF16)],
        compiler_params=_params(("arbitrary", "arbitrary")),
        name="matmul",
    )(*args)


def _rope_kernel(pos_ref, invf_ref, cos_ref, sin_ref):
    ang = pos_ref[...] * invf_ref[...]
    cos_ref[...] = jnp.cos(ang)
    sin_ref[...] = jnp.sin(ang)


def rope_tables(pos_f, inv_freq, tm=512):
    s = pos_f.shape[0]
    hd = inv_freq.shape[0]
    tm = min(tm, s)
    return pl.pallas_call(
        _rope_kernel,
        grid=(s // tm,),
        in_specs=[pl.BlockSpec((tm, 1), lambda i: (i, 0)),
                  pl.BlockSpec((1, hd), lambda i: (0, 0))],
        out_specs=[pl.BlockSpec((tm, hd), lambda i: (i, 0))] * 2,
        out_shape=[jax.ShapeDtypeStruct((s, hd), F32)] * 2,
        compiler_params=_params(("arbitrary",)),
        name="rope_tables",
    )(pos_f.reshape(s, 1), inv_freq.reshape(1, hd))


def _ret_kernel(q_ref, k_ref, v_ref, g_ref, cos_ref, sin_ref, lg_ref, gn_ref, o_ref, r_ref,
                decay_ref, xi_ref, zeta_ref, *, n_chunks, hpb):
    c = RET_CHUNK
    dk = RET_HEAD_DIM
    half = dk // 2
    scale = dk ** -0.5

    @pl.when(pl.program_id(1) == 0)
    def _():
        r_ref[...] = jnp.zeros_like(r_ref)
        row = lax.broadcasted_iota(jnp.int32, (c, c), 0).astype(F32)
        col = lax.broadcasted_iota(jnp.int32, (c, c), 1).astype(F32)
        diff = row - col
        rowk = lax.broadcasted_iota(jnp.int32, (c, dk), 0).astype(F32)
        for hh in range(hpb):
            lg = lg_ref[hh]
            decay_ref[hh] = jnp.where(diff >= 0, jnp.exp(lg[:, :c] * jnp.maximum(diff, 0.0)), 0.0)
            xi_ref[hh] = jnp.exp(lg * (rowk + 1.0))
            zeta_ref[hh] = jnp.exp(lg * (c - 1.0 - rowk))

    def rope(t, cos, sin):
        t1 = t[:, :half]
        t2 = t[:, half:]
        return jnp.concatenate([t1 * cos - t2 * sin, t1 * sin + t2 * cos], axis=-1)

    def body(j, carry):
        r0 = pl.multiple_of(j * c, c)
        rows = pl.ds(r0, c)
        cos = cos_ref[rows, :]
        sin = sin_ref[rows, :]
        for hh in range(hpb):
            cs = slice(hh * dk, (hh + 1) * dk)
            qr = rope(q_ref[rows, cs], cos, sin)
            kr = rope(k_ref[rows, cs], cos, sin) * scale
            qb = qr.astype(BF16)
            kb = kr.astype(BF16)
            vb = v_ref[rows, cs].astype(BF16)
            state = r_ref[hh]
            inner = lax.dot_general(qb, kb, (((1,), (1,)), ((), ())),
                                    preferred_element_type=F32) * decay_ref[hh]
            o = (jnp.dot(inner.astype(BF16), vb, preferred_element_type=F32)
                 + jnp.dot(qb, state.astype(BF16), preferred_element_type=F32) * xi_ref[hh])
            kz = (kr * zeta_ref[hh]).astype(BF16)
            chunk_decay = jnp.exp(lg_ref[hh] * c)
            r_ref[hh] = state * chunk_decay + lax.dot_general(
                kz, vb, (((0,), (0,)), ((), ())), preferred_element_type=F32)
            mu = jnp.mean(o, axis=-1, keepdims=True)
            oc = o - mu
            var = jnp.mean(oc * oc, axis=-1, keepdims=True)
            on = oc * lax.rsqrt(var + GN_EPS) * gn_ref[hh]
            g = g_ref[rows, cs]
            o_ref[rows, cs] = (on * (g * (1.0 / (1.0 + jnp.exp(-g))))).astype(o_ref.dtype)
        return carry

    lax.fori_loop(0, n_chunks, body, 0)


def retention(proj, cos, sin, lg_rows, gn_g, tr=RET_ROWS, hpb=RET_HEADS_PER_STEP):
    s = proj.shape[0]
    dk = RET_HEAD_DIM
    h = RET_HEADS
    tr = min(tr, s)
    c = RET_CHUNK
    w = hpb * dk

    def col(base):
        return pl.BlockSpec((tr, w), lambda hg, ci, base=base: (ci, base // hpb + hg))

    per_head = pl.BlockSpec((hpb, 1, dk), lambda hg, ci: (hg, 0, 0))
    return pl.pallas_call(
        functools.partial(_ret_kernel, n_chunks=tr // c, hpb=hpb),
        grid=(h // hpb, s // tr),
        in_specs=[col(0), col(h), col(2 * h), col(3 * h),
                  pl.BlockSpec((tr, dk // 2), lambda hg, ci: (ci, 0)),
                  pl.BlockSpec((tr, dk // 2), lambda hg, ci: (ci, 0)),
                  per_head, per_head],
        out_specs=pl.BlockSpec((tr, w), lambda hg, ci: (ci, hg)),
        out_shape=jax.ShapeDtypeStruct((s, h * dk), BF16),
        scratch_shapes=[pltpu.VMEM((hpb, dk, dk), F32),
                        pltpu.VMEM((hpb, c, c), F32),
                        pltpu.VMEM((hpb, c, dk), F32),
                        pltpu.VMEM((hpb, c, dk), F32)],
        compiler_params=_params(("arbitrary", "arbitrary")),
        name="retention",
    )(proj, proj, proj, proj, cos, sin, lg_rows, gn_g.reshape(h, 1, dk))


def _sigmoid(x):
    return 1.0 / (1.0 + jnp.exp(-x))


def _lru_kernel(xb_ref, gb_ref, cw_ref, cb_ref, wa_ref, ba_ref, wi_ref, bi_ref, lam_ref, og_ref,
                o_ref, tail_ref, xs_ref, hs_ref, h_ref, wabf_ref, wibf_ref):
    tr, cdim = xb_ref.shape
    nb = wa_ref.shape[0]
    bd = cdim // nb
    ph = SUBLANES
    ng = tr // ph

    @pl.when(pl.program_id(0) == 0)
    def _():
        tail_ref[...] = jnp.zeros_like(tail_ref)
        h_ref[...] = jnp.zeros_like(h_ref)
        wabf_ref[...] = wa_ref[...].astype(BF16)
        wibf_ref[...] = wi_ref[...].astype(BF16)

    lam = lam_ref[...]
    sp = jnp.maximum(-lam, 0.0) + jnp.log1p(jnp.exp(-jnp.abs(lam)))
    rowg = lax.broadcasted_iota(jnp.int32, (ng, bd), 0)
    lpb = bd // LANES
    for c in range(cdim // LANES):
        xs_ref[c] = xb_ref[:, c * LANES:(c + 1) * LANES]

    def phase_rows(ref, n, p):
        return jnp.concatenate([ref[n * lpb + c, pl.ds(p, ng, stride=ph), :] for c in range(lpb)],
                               axis=1)

    for n in range(nb):
        cs = slice(n * bd, (n + 1) * bd)
        x = [phase_rows(xs_ref, n, p) for p in range(ph)]

        def prev_group(p):
            return jnp.where(rowg == 0, tail_ref[p:p + 1, cs], pltpu.roll(x[p], 1, 0))

        back = {-k: prev_group(ph - k) for k in range(1, CONV_WIDTH)}

        def xat(p):
            return x[p] if p >= 0 else back[p]

        xc = []
        for p in range(ph):
            acc = cb_ref[:, cs] + cw_ref[CONV_WIDTH - 1:CONV_WIDTH, cs] * xat(p)
            for k in range(1, CONV_WIDTH):
                acc = acc + cw_ref[CONV_WIDTH - 1 - k:CONV_WIDTH - k, cs] * xat(p - k)
            xc.append(acc)
        xg = jnp.concatenate(xc, axis=0)
        xgb = xg.astype(BF16)
        r = _sigmoid(jnp.dot(xgb, wabf_ref[n], preferred_element_type=F32) + ba_ref[:, cs])
        ig = _sigmoid(jnp.dot(xgb, wibf_ref[n], preferred_element_type=F32) + bi_ref[:, cs])
        log_a = (-RG_C * r) * sp[:, cs]
        a = jnp.exp(log_a)
        b = jnp.sqrt(-jnp.tanh(log_a) * (a * a + 1.0)) * (ig * xg)

        cum_a = [a[0:ng]]
        cum_b = [b[0:ng]]
        for p in range(1, ph):
            ap = a[p * ng:(p + 1) * ng]
            cum_b.append(ap * cum_b[-1] + b[p * ng:(p + 1) * ng])
            cum_a.append(ap * cum_a[-1])
        sa, sb = cum_a[-1], cum_b[-1]
        d = 1
        while d < ng:
            keep = rowg >= d
            sa_sh = pltpu.roll(sa, d, 0)
            sb_sh = pltpu.roll(sb, d, 0)
            sb = jnp.where(keep, sa * sb_sh + sb, sb)
            sa = jnp.where(keep, sa * sa_sh, sa)
            d *= 2
        h_in = h_ref[:, cs]
        h_end = sa * h_in + sb
        h_prev = jnp.where(rowg == 0, h_in, pltpu.roll(h_end, 1, 0))
        for p in range(ph):
            hp = cum_a[p] * h_prev + cum_b[p]
            for c in range(lpb):
                hs_ref[n * lpb + c, pl.ds(p, ng, stride=ph), :] = hp[:, c * LANES:(c + 1) * LANES]
        h_ref[:, cs] = h_end[ng - 1:ng, :]

    gb = gb_ref[...]
    gelu = 0.5 * gb * (1.0 + jnp.tanh(0.7978845608028654 * (gb + 0.044715 * (gb * gb * gb))))
    y = jnp.concatenate([hs_ref[c] for c in range(cdim // LANES)], axis=1) * gelu
    ms = jnp.mean(y * y, axis=-1, keepdims=True)
    o_ref[...] = (y * lax.rsqrt(ms + NORM_EPS) * og_ref[...]).astype(o_ref.dtype)
    tail_ref[...] = xb_ref[tr - ph:tr, :]


def rg_lru(proj, xb_block, gb_block, conv_w, conv_b, w_a, b_a, w_i, b_i, lam, out_g, tr=LRU_ROWS):
    s = proj.shape[0]
    cdim = conv_w.shape[1]
    nb, bd, _ = w_a.shape
    tr = min(tr, s)
    vec = pl.BlockSpec((1, cdim), lambda i: (0, 0))
    wspec = pl.BlockSpec((nb, bd, bd), lambda i: (0, 0, 0))
    return pl.pallas_call(
        _lru_kernel,
        grid=(s // tr,),
        in_specs=[pl.BlockSpec((tr, cdim), lambda i: (i, xb_block)),
                  pl.BlockSpec((tr, cdim), lambda i: (i, gb_block)),
                  pl.BlockSpec((CONV_WIDTH, cdim), lambda i: (0, 0)),
                  vec, wspec, vec, wspec, vec, vec, vec],
        out_specs=pl.BlockSpec((tr, cdim), lambda i: (i, 0)),
        out_shape=jax.ShapeDtypeStruct((s, cdim), BF16),
        scratch_shapes=[pltpu.VMEM((SUBLANES, cdim), F32),
                        pltpu.VMEM((cdim // LANES, tr, LANES), F32),
                        pltpu.VMEM((cdim // LANES, tr, LANES), F32),
                        pltpu.VMEM((1, cdim), F32),
                        pltpu.VMEM((nb, bd, bd), BF16),
                        pltpu.VMEM((nb, bd, bd), BF16)],
        compiler_params=_params(("arbitrary",)),
        name="rg_lru",
    )(proj, proj, conv_w, conv_b.reshape(1, cdim), w_a, b_a.reshape(1, cdim), w_i,
      b_i.reshape(1, cdim), lam.reshape(1, cdim), out_g.reshape(1, cdim))


def _wqk_kernel(wq_ref, k_ref, o_ref):
    o_ref[...] = lax.dot_general(wq_ref[...].astype(BF16), k_ref[...], (((1,), (1,)), ((), ())),
                                 preferred_element_type=F32).astype(o_ref.dtype)


def _vo_kernel(v_ref, wo_ref, o_ref):
    o_ref[...] = jnp.dot(v_ref[...], wo_ref[...].astype(BF16),
                         preferred_element_type=F32).astype(o_ref.dtype)


def xattn_fold(k, v, wq, wo, tile=XF_TILE):
    mlen, d = k.shape
    hd = d // X_HEADS
    tile = min(tile, d)
    wqk = pl.pallas_call(
        _wqk_kernel,
        grid=(X_HEADS, d // tile),
        in_specs=[pl.BlockSpec((tile, hd), lambda h, r: (r, h)),
                  pl.BlockSpec((mlen, hd), lambda h, r: (0, h))],
        out_specs=pl.BlockSpec((tile, mlen), lambda h, r: (r, h)),
        out_shape=jax.ShapeDtypeStruct((d, X_HEADS * mlen), BF16),
        compiler_params=_params(("arbitrary", "arbitrary")),
        name="xattn_wqk",
    )(wq, k)
    vo = pl.pallas_call(
        _vo_kernel,
        grid=(X_HEADS, d // tile),
        in_specs=[pl.BlockSpec((mlen, hd), lambda h, j: (0, h)),
                  pl.BlockSpec((hd, tile), lambda h, j: (h, j))],
        out_specs=pl.BlockSpec((mlen, tile), lambda h, j: (h, j)),
        out_shape=jax.ShapeDtypeStruct((X_HEADS * mlen, d), BF16),
        compiler_params=_params(("arbitrary", "arbitrary")),
        name="xattn_vo",
    )(v, wo)
    return wqk, vo


def _xattn_kernel(x_ref, g_ref, wqk_ref, vo_ref, o_ref):
    d = x_ref.shape[1]
    mlen = wqk_ref.shape[1] // X_HEADS
    scale = (d // X_HEADS) ** -0.5
    x = x_ref[...]
    ms = jnp.mean(x * x, axis=-1, keepdims=True)
    h = (x * lax.rsqrt(ms + NORM_EPS) * g_ref[...]).astype(BF16)
    s = jnp.dot(h, wqk_ref[...], preferred_element_type=F32) * scale
    ps = []
    for hh in range(X_HEADS):
        sh = s[:, hh * mlen:(hh + 1) * mlen]
        m = jnp.max(sh, axis=-1, keepdims=True)
        e = jnp.exp(sh - m)
        ps.append((e / jnp.sum(e, axis=-1, keepdims=True)).astype(BF16))
    p = jnp.concatenate(ps, axis=1)
    o_ref[...] = x + jnp.dot(p, vo_ref[...], preferred_element_type=F32)


def xattn(x, g, wqk, vo, tm=ATT_ROWS):
    s, d = x.shape
    tm = min(tm, s)
    return pl.pallas_call(
        _xattn_kernel,
        grid=(s // tm,),
        in_specs=[pl.BlockSpec((tm, d), lambda i: (i, 0)),
                  pl.BlockSpec((1, d), lambda i: (0, 0)),
                  pl.BlockSpec(wqk.shape, lambda i: (0, 0), pipeline_mode=pl.Buffered(1)),
                  pl.BlockSpec(vo.shape, lambda i: (0, 0), pipeline_mode=pl.Buffered(1))],
        out_specs=pl.BlockSpec((tm, d), lambda i: (i, 0)),
        out_shape=jax.ShapeDtypeStruct((s, d), F32),
        compiler_params=_params(("arbitrary",)),
        name="xattn",
    )(x, g.reshape(1, d), wqk, vo)


def _split_bf16(a):
    hi = a.astype(BF16)
    return hi, (a - hi.astype(F32)).astype(BF16)


def _router_kernel(x_ref, g_ref, wr_ref, br_ref, route_ref, counts_ref, carry_ref, wsplit_ref):
    tm = x_ref.shape[0]
    nl = ROUTE_LANES

    @pl.when(pl.program_id(0) == 0)
    def _():
        carry_ref[...] = jnp.zeros_like(carry_ref)
        w_hi, w_lo = _split_bf16(wr_ref[...])
        wsplit_ref[:, :nl] = w_hi
        wsplit_ref[:, nl:] = w_lo

    x = x_ref[...]
    ms = jnp.mean(x * x, axis=-1, keepdims=True)
    h = x * lax.rsqrt(ms + NORM_EPS) * g_ref[...]
    h_hi, h_lo = _split_bf16(h)
    both = jnp.dot(h_hi, wsplit_ref[...], preferred_element_type=F32)
    cross = jnp.dot(h_lo, wsplit_ref[:, :nl], preferred_element_type=F32)
    logits = both[:, :nl] + (both[:, nl:] + cross) + br_ref[...]
    lane = lax.broadcasted_iota(jnp.int32, (tm, ROUTE_LANES), 1)
    neg = -jnp.inf
    big = ROUTE_LANES

    gmask = (lane >= GROUP_LANE0) & (lane < GROUP_LANE0 + N_GROUPS)
    gl = jnp.where(gmask, logits, neg)
    gmax = jnp.max(gl, axis=-1, keepdims=True)
    gsum = jnp.sum(jnp.where(gmask, jnp.exp(gl - gmax), 0.0), axis=-1, keepdims=True)
    g_val = 1.0 / gsum
    g_idx = jnp.min(jnp.where(gl == gmax, lane, big), axis=-1, keepdims=True) - GROUP_LANE0

    lo = EXPERT_LANE0 + g_idx * EXPERTS_PER_GROUP
    emask = (lane >= lo) & (lane < lo + EXPERTS_PER_GROUP)
    el = jnp.where(emask, logits, neg)
    t1 = jnp.max(el, axis=-1, keepdims=True)
    i1 = jnp.min(jnp.where(emask & (el == t1), lane, big), axis=-1, keepdims=True)
    emask2 = emask & (lane != i1)
    el2 = jnp.where(emask2, logits, neg)
    t2 = jnp.max(el2, axis=-1, keepdims=True)
    i2 = jnp.min(jnp.where(emask2 & (el2 == t2), lane, big), axis=-1, keepdims=True)
    dexp = jnp.exp(t2 - t1)
    w0 = g_val / (1.0 + dexp)
    w1 = g_val * dexp / (1.0 + dexp)

    sel1 = lane == i1
    sel2 = lane == i2
    onehot = jnp.where(sel1 | sel2, 1.0, 0.0)
    rr = lax.broadcasted_iota(jnp.int32, (tm, tm), 0)
    cc = lax.broadcasted_iota(jnp.int32, (tm, tm), 1)
    tri = jnp.where(cc < rr, 1.0, 0.0).astype(BF16)
    prefix = jnp.dot(tri, onehot.astype(BF16), preferred_element_type=F32) + carry_ref[...]
    rank0 = jnp.sum(jnp.where(sel1, prefix, 0.0), axis=-1, keepdims=True)
    rank1 = jnp.sum(jnp.where(sel2, prefix, 0.0), axis=-1, keepdims=True)
    total = carry_ref[...] + jnp.sum(onehot, axis=0, keepdims=True)
    carry_ref[...] = total
    counts_ref[...] = total

    e0 = (i1 - EXPERT_LANE0).astype(F32)
    e1 = (i2 - EXPERT_LANE0).astype(F32)
    route = jnp.where(lane == 0, e0, 0.0)
    route = jnp.where(lane == 1, e1, route)
    route = jnp.where(lane == 2, w0, route)
    route = jnp.where(lane == 3, w1, route)
    route = jnp.where(lane == 4, rank0, route)
    route = jnp.where(lane == 5, rank1, route)
    route_ref[...] = route


def router(x, g, wr, br, tm=ROUTE_ROWS):
    t, d = x.shape
    tm = min(tm, t)
    return pl.pallas_call(
        _router_kernel,
        grid=(t // tm,),
        in_specs=[pl.BlockSpec((tm, d), lambda i: (i, 0)),
                  pl.BlockSpec((1, d), lambda i: (0, 0)),
                  pl.BlockSpec((d, ROUTE_LANES), lambda i: (0, 0)),
                  pl.BlockSpec((1, ROUTE_LANES), lambda i: (0, 0))],
        out_specs=[pl.BlockSpec((tm, ROUTE_LANES), lambda i: (i, 0)),
                   pl.BlockSpec((1, ROUTE_LANES), lambda i: (0, 0))],
        out_shape=[jax.ShapeDtypeStruct((t, ROUTE_LANES), F32),
                   jax.ShapeDtypeStruct((1, ROUTE_LANES), F32)],
        scratch_shapes=[pltpu.VMEM((1, ROUTE_LANES), F32),
                        pltpu.VMEM((d, 2 * ROUTE_LANES), BF16)],
        compiler_params=_params(("arbitrary",)),
        name="router",
    )(x, g.reshape(1, d), wr, br)


def _gather_kernel(e0_ref, e1_ref, r0_ref, r1_ref, ps_ref, nu_ref, g_ref, x_hbm, xs_ref,
                   rowtok_ref, buf_ref, sem, *, blk, n_tok):
    i = pl.program_id(0)
    nu = nu_ref[0]

    def row_copy(b, slot, r):
        tok = rowtok_ref[b * blk + r]
        return pltpu.make_async_copy(x_hbm.at[pl.ds(tok, 1), :],
                                     buf_ref.at[slot, pl.ds(r, 1), :], sem.at[slot])

    def start_block(b, slot):
        def body(r, carry):
            row_copy(b, slot, r).start()
            return carry
        lax.fori_loop(0, blk, body, 0, unroll=SCALAR_UNROLL)

    def wait_block(b, slot):
        def body(r, carry):
            row_copy(b, slot, r).wait()
            return carry
        lax.fori_loop(0, blk, body, 0, unroll=SCALAR_UNROLL)

    @pl.when(i == 0)
    def _():
        n_rows = rowtok_ref.shape[0]
        for base in range(0, n_rows, n_tok):
            def init(r, carry, base=base):
                rowtok_ref[base + r] = r
                return carry
            lax.fori_loop(0, min(n_tok, n_rows - base), init, 0, unroll=SCALAR_UNROLL)

        def fill(t, carry):
            rowtok_ref[ps_ref[e0_ref[t]] + r0_ref[t]] = t
            rowtok_ref[ps_ref[e1_ref[t]] + r1_ref[t]] = t
            return carry
        lax.fori_loop(0, n_tok, fill, 0, unroll=SCALAR_UNROLL)
        start_block(0, 0)

    @pl.when(i + 1 < nu)
    def _():
        start_block(i + 1, (i + 1) % 2)

    @pl.when(i < nu)
    def _():
        slot = i % 2
        wait_block(i, slot)
        x = buf_ref[slot]
        ms = jnp.mean(x * x, axis=-1, keepdims=True)
        xs_ref[...] = (x * lax.rsqrt(ms + NORM_EPS) * g_ref[...]).astype(xs_ref.dtype)

    @pl.when(i >= nu)
    def _():
        xs_ref[...] = jnp.zeros_like(xs_ref)


def moe_gather(x, g, slots, n_used, n_blocks, blk):
    t, d = x.shape
    grid_spec = pltpu.PrefetchScalarGridSpec(
        num_scalar_prefetch=6,
        grid=(n_blocks,),
        in_specs=[pl.BlockSpec((1, d), lambda i, *_: (0, 0)),
                  pl.BlockSpec(memory_space=pl.ANY)],
        out_specs=pl.BlockSpec((blk, d), lambda i, *_: (i, 0)),
        scratch_shapes=[pltpu.SMEM((n_blocks * blk,), jnp.int32),
                        pltpu.VMEM((2, blk, d), F32),
                        pltpu.SemaphoreType.DMA((2,))],
    )
    return pl.pallas_call(
        functools.partial(_gather_kernel, blk=blk, n_tok=t),
        grid_spec=grid_spec,
        out_shape=jax.ShapeDtypeStruct((n_blocks * blk, d), BF16),
        compiler_params=_params(("arbitrary",)),
        name="moe_gather",
    )(*slots, n_used, g.reshape(1, d), x)


def _expert_changed(be_ref, i):
    prev = be_ref[jnp.maximum(i - 1, 0)]
    return (i == 0) | (be_ref[i] != prev)


def _stream_expert_weights(w_hbms, col0, be_ref, nx_ref, nu, i, wst_ref, wbf_ref, sem, slot_ref):
    tn = wst_ref.shape[-1]

    def copies(e, slot):
        return [pltpu.make_async_copy(w.at[e, :, pl.ds(col0, tn)], wst_ref.at[slot, l], sem.at[slot])
                for l, w in enumerate(w_hbms)]

    @pl.when(i == 0)
    def _():
        slot_ref[0] = 0
        for c in copies(be_ref[0], 0):
            c.start()

    @pl.when(_expert_changed(be_ref, i))
    def _():
        slot = slot_ref[0]
        for c in copies(be_ref[i], slot):
            c.wait()
        nxt = nx_ref[i]

        @pl.when(nxt < nu)
        def _():
            for c in copies(be_ref[jnp.minimum(nxt, be_ref.shape[0] - 1)], 1 - slot):
                c.start()

        for l in range(len(w_hbms)):
            _cast_rows(wst_ref.at[slot, l], wbf_ref.at[l])
        slot_ref[0] = 1 - slot


def _moe_up_kernel(be_ref, nx_ref, nu_ref, xs_ref, wg_hbm, wu_hbm, act_ref,
                   wst_ref, wbf_ref, sem, slot_ref):
    j = pl.program_id(0)
    i = pl.program_id(1)
    nu = nu_ref[0]
    tf = act_ref.shape[1]

    @pl.when(i < nu)
    def _():
        _stream_expert_weights([wg_hbm, wu_hbm], pl.multiple_of(j * tf, tf), be_ref, nx_ref, nu, i,
                               wst_ref, wbf_ref, sem, slot_ref)
        x = xs_ref[...]
        gate = jnp.dot(x, wbf_ref[0], preferred_element_type=F32)
        up = jnp.dot(x, wbf_ref[1], preferred_element_type=F32)
        act_ref[...] = (gate * _sigmoid(gate) * up).astype(act_ref.dtype)

    @pl.when(i >= nu)
    def _():
        act_ref[...] = jnp.zeros_like(act_ref)


def _moe_down_kernel(be_ref, nx_ref, nu_ref, act_ref, wd_hbm, y_ref, wst_ref, wbf_ref, sem, slot_ref):
    j = pl.program_id(0)
    i = pl.program_id(1)
    nu = nu_ref[0]
    tn = y_ref.shape[1]

    @pl.when(i < nu)
    def _():
        _stream_expert_weights([wd_hbm], pl.multiple_of(j * tn, tn), be_ref, nx_ref, nu, i,
                               wst_ref, wbf_ref, sem, slot_ref)
        y_ref[...] = jnp.dot(act_ref[...], wbf_ref[0], preferred_element_type=F32)

    @pl.when(i >= nu)
    def _():
        y_ref[...] = jnp.zeros_like(y_ref)


def moe_experts(xs, block_expert, next_expert_block, n_used, w_gate, w_up, w_down,
                blk=MOE_BLK, tf=MOE_TF, tn=MOE_TN):
    r, dw = xs.shape
    _, d, f = w_gate.shape
    n_blocks = r // blk
    tf = min(tf, f)
    tn = min(tn, d)

    def used(i, nu):
        return jnp.minimum(i, jnp.maximum(nu[0] - 1, 0))

    def stream_scratch(n_mats, k, n):
        return [pltpu.VMEM((2, n_mats, k, n), F32),
                pltpu.VMEM((n_mats, k, n), BF16),
                pltpu.SemaphoreType.DMA((2,)),
                pltpu.SMEM((1,), jnp.int32)]

    up_spec = pltpu.PrefetchScalarGridSpec(
        num_scalar_prefetch=3,
        grid=(f // tf, n_blocks),
        in_specs=[pl.BlockSpec((blk, dw), lambda j, i, be, nx, nu: (used(i, nu), 0)),
                  pl.BlockSpec(memory_space=pl.ANY),
                  pl.BlockSpec(memory_space=pl.ANY)],
        out_specs=pl.BlockSpec((blk, tf), lambda j, i, be, nx, nu: (i, j)),
        scratch_shapes=stream_scratch(2, d, tf),
    )
    act = pl.pallas_call(
        _moe_up_kernel,
        grid_spec=up_spec,
        out_shape=jax.ShapeDtypeStruct((r, f), BF16),
        compiler_params=_params(("arbitrary", "arbitrary")),
        name="moe_up",
    )(block_expert, next_expert_block, n_used, xs, w_gate, w_up)
    down_spec = pltpu.PrefetchScalarGridSpec(
        num_scalar_prefetch=3,
        grid=(d // tn, n_blocks),
        in_specs=[pl.BlockSpec((blk, f), lambda j, i, be, nx, nu: (used(i, nu), 0)),
                  pl.BlockSpec(memory_space=pl.ANY)],
        out_specs=pl.BlockSpec((blk, tn), lambda j, i, be, nx, nu: (i, j)),
        scratch_shapes=stream_scratch(1, f, tn),
    )
    return pl.pallas_call(
        _moe_down_kernel,
        grid_spec=down_spec,
        out_shape=jax.ShapeDtypeStruct((r, d), F32),
        compiler_params=_params(("arbitrary", "arbitrary")),
        name="moe_down",
    )(block_expert, next_expert_block, n_used, act, w_down)


def _combine_kernel(e0_ref, e1_ref, r0_ref, r1_ref, ps_ref, x_ref, route_ref, g_ref, y_hbm, o_ref,
                    ya_ref, yb_ref, sem, *, tb, final_norm):
    step = pl.program_id(0)

    def copies(b, slot, i):
        t = b * tb + i
        row0 = ps_ref[e0_ref[t]] + r0_ref[t]
        row1 = ps_ref[e1_ref[t]] + r1_ref[t]
        return (pltpu.make_async_copy(y_hbm.at[pl.ds(row0, 1), :],
                                      ya_ref.at[slot, pl.ds(i, 1), :], sem.at[slot]),
                pltpu.make_async_copy(y_hbm.at[pl.ds(row1, 1), :],
                                      yb_ref.at[slot, pl.ds(i, 1), :], sem.at[slot]))

    def start_block(b, slot):
        def body(i, carry):
            c0, c1 = copies(b, slot, i)
            c0.start()
            c1.start()
            return carry
        lax.fori_loop(0, tb, body, 0, unroll=SCALAR_UNROLL)

    def wait_block(b, slot):
        def body(i, carry):
            c0, c1 = copies(b, slot, i)
            c0.wait()
            c1.wait()
            return carry
        lax.fori_loop(0, tb, body, 0, unroll=SCALAR_UNROLL)

    @pl.when(step == 0)
    def _():
        start_block(0, 0)

    @pl.when(step + 1 < pl.num_programs(0))
    def _():
        start_block(step + 1, (step + 1) % 2)

    slot = step % 2
    wait_block(step, slot)
    w0 = route_ref[:, 2:3]
    w1 = route_ref[:, 3:4]
    x = x_ref[...] + (ya_ref[slot] * w0 + yb_ref[slot] * w1)
    if final_norm:
        ms = jnp.mean(x * x, axis=-1, keepdims=True)
        x = x * lax.rsqrt(ms + NORM_EPS) * g_ref[...]
    o_ref[...] = x


def combine(x, route, y, slots, g, final_norm, tb=COMB_ROWS):
    t, d = x.shape
    tb = min(tb, t)
    grid_spec = pltpu.PrefetchScalarGridSpec(
        num_scalar_prefetch=5,
        grid=(t // tb,),
        in_specs=[pl.BlockSpec((tb, d), lambda i, *_: (i, 0)),
                  pl.BlockSpec((tb, ROUTE_LANES), lambda i, *_: (i, 0)),
                  pl.BlockSpec((1, d), lambda i, *_: (0, 0)),
                  pl.BlockSpec(memory_space=pl.ANY)],
        out_specs=pl.BlockSpec((tb, d), lambda i, *_: (i, 0)),
        scratch_shapes=[pltpu.VMEM((2, tb, d), F32), pltpu.VMEM((2, tb, d), F32),
                        pltpu.SemaphoreType.DMA((2,))],
    )
    return pl.pallas_call(
        functools.partial(_combine_kernel, tb=tb, final_norm=final_norm),
        grid_spec=grid_spec,
        out_shape=jax.ShapeDtypeStruct((t, d), F32),
        compiler_params=_params(("arbitrary",)),
        name="moe_combine",
    )(*slots, x, route, g.reshape(1, d), y)


def _route_lanes(group_part, expert_part):
    rows = group_part.shape[0]
    gap = jnp.zeros((rows, EXPERT_LANE0 - GROUP_LANE0 - N_GROUPS), F32)
    tail = jnp.zeros((rows, ROUTE_LANES - EXPERT_LANE0 - N_EXPERTS), F32)
    return jnp.concatenate([group_part, gap, expert_part, tail], axis=1)


def _moe_layout(route, counts, blk):
    t = route.shape[0]
    ri = route[:, :8].astype(jnp.int32)
    e0, e1, rank0, rank1 = ri[:, 0], ri[:, 1], ri[:, 4], ri[:, 5]
    cnt = counts[0, EXPERT_LANE0:EXPERT_LANE0 + N_EXPERTS].astype(jnp.int32)
    padded = (cnt + blk - 1) // blk * blk
    pends = jnp.cumsum(padded)
    pstarts = pends - padded
    n_blocks = (2 * t) // blk + N_EXPERTS
    block_start = jnp.arange(n_blocks, dtype=jnp.int32) * blk
    block_expert = jnp.minimum(
        jnp.sum((block_start[:, None] >= pends[None, :]).astype(jnp.int32), axis=1), N_EXPERTS - 1)
    n_used = (pends[-1] // blk).astype(jnp.int32).reshape(1)
    block_expert = block_expert[jnp.minimum(jnp.arange(n_blocks), jnp.maximum(n_used[0] - 1, 0))]
    next_expert_block = pends[block_expert] // blk
    return (e0, e1, rank0, rank1, pstarts), block_expert, next_expert_block, n_used, n_blocks


def kernel(x, mem, positions, mix_norm_g, w_in, ret_norm_g, lru_conv_w, lru_conv_b, lru_w_a, lru_b_a, lru_w_i, lru_b_i, lru_lambda, lru_norm_g, w_out, xattn_norm_g, mem_norm_g, xattn_wq, xattn_wk, xattn_wv, xattn_wo, moe_norm_g, router_group_w, router_group_b, router_expert_w, router_expert_b, expert_w_gate, expert_w_up, expert_w_down, final_norm_g):
    b, s, d = x.shape
    depth = w_in.shape[0]
    ret_width = RET_HEADS * RET_HEAD_DIM
    lru_width = lru_conv_w.shape[-1]
    assert ret_width == lru_width and ret_width + lru_width == d
    inv_freq = ROPE_BASE ** (-jnp.arange(0, RET_HEAD_DIM, 2, dtype=F32) / RET_HEAD_DIM)
    lg = jnp.log1p(-jnp.exp2(-5.0 - jnp.arange(RET_HEADS, dtype=F32)))
    lg_rows = jnp.broadcast_to(lg[:, None, None], (RET_HEADS, 1, RET_HEAD_DIM))
    blk = min(MOE_BLK, s)
    outs = []
    for bi in range(b):
        xcur = x[bi]
        cos, sin = rope_tables(positions[bi].astype(F32), inv_freq)
        for l in range(depth):
            h = normcast(xcur, mix_norm_g[l], BF16, NORM_ROWS)
            proj = matmul([h], w_in[l], F32, tm=MM_WIDE_TM, tn=MM_WIDE_TN)
            ret = retention(proj, cos, sin, lg_rows, ret_norm_g[l])
            lru = rg_lru(proj, 4 * ret_width // lru_width, 4 * ret_width // lru_width + 1,
                         lru_conv_w[l], lru_conv_b[l], lru_w_a[l], lru_b_a[l], lru_w_i[l],
                         lru_b_i[l], lru_lambda[l], lru_norm_g[l])
            xcur = matmul([ret, lru], w_out[l], F32, res=xcur)
            memn = normcast(mem[bi], mem_norm_g[l], BF16, NORM_ROWS)
            kk = matmul([memn], xattn_wk[l], BF16)
            vv = matmul([memn], xattn_wv[l], BF16)
            wqk, vo = xattn_fold(kk, vv, xattn_wq[l], xattn_wo[l])
            xcur = xattn(xcur, xattn_norm_g[l], wqk, vo)
            wr = _route_lanes(router_group_w[l], router_expert_w[l])
            br = _route_lanes(router_group_b[l][None], router_expert_b[l][None])
            route, counts = router(xcur, moe_norm_g[l], wr, br)
            slots, block_expert, next_block, n_used, n_blocks = _moe_layout(route, counts, blk)
            xs = moe_gather(xcur, moe_norm_g[l], slots, n_used, n_blocks, blk)
            y = moe_experts(xs, block_expert, next_block, n_used, expert_w_gate[l], expert_w_up[l],
                            expert_w_down[l], blk=blk)
            xcur = combine(xcur, route, y, slots, final_norm_g, final_norm=l == depth - 1)
        outs.append(xcur)
    return outs[0][None] if b == 1 else jnp.stack(outs, axis=0)
```

```python
import functools

import jax
import jax.numpy as jnp
from jax import lax
from jax.experimental import pallas as pl
from jax.experimental.pallas import tpu as pltpu

F32 = jnp.float32
BF16 = jnp.bfloat16

RET_HEADS = 8
RET_HEAD_DIM = 256
RET_CHUNK = 128
LRU_BLOCKS = 8
CONV_WIDTH = 4
RG_C = 8.0
ROPE_BASE = 10000.0
X_HEADS = 4
N_GROUPS = 4
EXPERTS_PER_GROUP = 8
N_EXPERTS = N_GROUPS * EXPERTS_PER_GROUP
NORM_EPS = 1e-6
GN_EPS = 1e-5

LANES = 128
SUBLANES = 8
VMEM_LIMIT = 56 * 1024 * 1024

NORM_ROWS = 512
MM_TM = 1024
MM_TN = 512
MM_WIDE_TM = 512
MM_WIDE_TN = 1024
RET_ROWS = 512
RET_HEADS_PER_STEP = 8
LRU_ROWS = 256
ATT_ROWS = 512
ROUTE_ROWS = 256
MOE_BLK = 256
MOE_TF = 512
MOE_TN = 4096
XF_TILE = 1024
SCALAR_UNROLL = 8
COMB_ROWS = 256
ROUTE_LANES = LANES
GROUP_LANE0 = 0
EXPERT_LANE0 = 8


def _params(sem):
    return pltpu.CompilerParams(dimension_semantics=sem, vmem_limit_bytes=VMEM_LIMIT)


def _normcast_kernel(x_ref, g_ref, o_ref):
    x = x_ref[...]
    ms = jnp.mean(x * x, axis=-1, keepdims=True)
    o_ref[...] = (x * lax.rsqrt(ms + NORM_EPS) * g_ref[...]).astype(o_ref.dtype)


def normcast(x, g, out_dtype, tm):
    m, d = x.shape
    tm = min(tm, m)
    return pl.pallas_call(
        _normcast_kernel,
        grid=(m // tm,),
        in_specs=[pl.BlockSpec((tm, d), lambda i: (i, 0)),
                  pl.BlockSpec((1, d), lambda i: (0, 0))],
        out_specs=pl.BlockSpec((tm, d), lambda i: (i, 0)),
        out_shape=jax.ShapeDtypeStruct((m, d), out_dtype),
        compiler_params=_params(("arbitrary",)),
        name="normcast",
    )(x, g.reshape(1, d))


def _cast_rows(src_ref, dst_ref, rows_per_iter=256):
    k = src_ref.shape[0]
    step = min(rows_per_iter, k)

    def body(i, carry):
        r0 = pl.multiple_of(i * step, step)
        dst_ref[pl.ds(r0, step), :] = src_ref[pl.ds(r0, step), :].astype(dst_ref.dtype)
        return carry

    lax.fori_loop(0, k // step, body, 0)


def _mm_kernel(*refs, n_a, has_res):
    a_refs = refs[:n_a]
    w_ref = refs[n_a]
    res_ref = refs[n_a + 1] if has_res else None
    o_ref = refs[n_a + 1 + int(has_res)]
    wbf_ref = refs[n_a + 2 + int(has_res)]

    @pl.when(pl.program_id(1) == 0)
    def _():
        _cast_rows(w_ref, wbf_ref)

    kp = a_refs[0].shape[1]
    acc = None
    for p, a_ref in enumerate(a_refs):
        d = jnp.dot(a_ref[...], wbf_ref[p * kp:(p + 1) * kp, :], preferred_element_type=F32)
        acc = d if acc is None else acc + d
    if has_res:
        acc = acc + res_ref[...]
    o_ref[...] = acc.astype(o_ref.dtype)


def matmul(a_parts, w, out_dtype, res=None, tm=MM_TM, tn=MM_TN):
    m, kp = a_parts[0].shape
    k, n = w.shape
    assert kp * len(a_parts) == k
    tm = min(tm, m)
    tn = min(tn, n)
    in_specs = [pl.BlockSpec((tm, kp), lambda j, i: (i, 0)) for _ in a_parts]
    in_specs.append(pl.BlockSpec((k, tn), lambda j, i: (0, j)))
    args = list(a_parts) + [w]
    if res is not None:
        in_specs.append(pl.BlockSpec((tm, tn), lambda j, i: (i, j)))
        args.append(res)
    return pl.pallas_call(
        functools.partial(_mm_kernel, n_a=len(a_parts), has_res=res is not None),
        grid=(n // tn, m // tm),
        in_specs=in_specs,
        out_specs=pl.BlockSpec((tm, tn), lambda j, i: (i, j)),
        out_shape=jax.ShapeDtypeStruct((m, n), out_dtype),
        scratch_shapes=[pltpu.VMEM((k, tn), BF16)],
        compiler_params=_params(("arbitrary", "arbitrary")),
        name="matmul",
    )(*args)


def _rope_kernel(pos_ref, invf_ref, cos_ref, sin_ref):
    ang = pos_ref[...] * invf_ref[...]
    cos_ref[...] = jnp.cos(ang)
    sin_ref[...] = jnp.sin(ang)


def rope_tables(pos_f, inv_freq, tm=512):
    s = pos_f.shape[0]
    hd = inv_freq.shape[0]
    tm = min(tm, s)
    return pl.pallas_call(
        _rope_kernel,
        grid=(s // tm,),
        in_specs=[pl.BlockSpec((tm, 1), lambda i: (i, 0)),
                  pl.BlockSpec((1, hd), lambda i: (0, 0))],
        out_specs=[pl.BlockSpec((tm, hd), lambda i: (i, 0))] * 2,
        out_shape=[jax.ShapeDtypeStruct((s, hd), F32)] * 2,
        compiler_params=_params(("arbitrary",)),
        name="rope_tables",
    )(pos_f.reshape(s, 1), inv_freq.reshape(1, hd))


def _ret_kernel(q_ref, k_ref, v_ref, g_ref, cos_ref, sin_ref, lg_ref, gn_ref, o_ref, r_ref,
                decay_ref, xi_ref, zeta_ref, *, n_chunks, hpb):
    c = RET_CHUNK
    dk = RET_HEAD_DIM
    half = dk // 2
    scale = dk ** -0.5

    @pl.when(pl.program_id(1) == 0)
    def _():
        r_ref[...] = jnp.zeros_like(r_ref)
        row = lax.broadcasted_iota(jnp.int32, (c, c), 0).astype(F32)
        col = lax.broadcasted_iota(jnp.int32, (c, c), 1).astype(F32)
        diff = row - col
        rowk = lax.broadcasted_iota(jnp.int32, (c, dk), 0).astype(F32)
        for hh in range(hpb):
            lg = lg_ref[hh]
            decay_ref[hh] = jnp.where(diff >= 0, jnp.exp(lg[:, :c] * jnp.maximum(diff, 0.0)), 0.0)
            xi_ref[hh] = jnp.exp(lg * (rowk + 1.0))
            zeta_ref[hh] = jnp.exp(lg * (c - 1.0 - rowk))

    def rope(t, cos, sin):
        t1 = t[:, :half]
        t2 = t[:, half:]
        return jnp.concatenate([t1 * cos - t2 * sin, t1 * sin + t2 * cos], axis=-1)

    def body(j, carry):
        r0 = pl.multiple_of(j * c, c)
        rows = pl.ds(r0, c)
        cos = cos_ref[rows, :]
        sin = sin_ref[rows, :]
        for hh in range(hpb):
            cs = slice(hh * dk, (hh + 1) * dk)
            qr = rope(q_ref[rows, cs], cos, sin)
            kr = rope(k_ref[rows, cs], cos, sin) * scale
            qb = qr.astype(BF16)
            kb = kr.astype(BF16)
            vb = v_ref[rows, cs].astype(BF16)
            state = r_ref[hh]
            inner = lax.dot_general(qb, kb, (((1,), (1,)), ((), ())),
                                    preferred_element_type=F32) * decay_ref[hh]
            o = (jnp.dot(inner.astype(BF16), vb, preferred_element_type=F32)
                 + jnp.dot(qb, state.astype(BF16), preferred_element_type=F32) * xi_ref[hh])
            kz = (kr * zeta_ref[hh]).astype(BF16)
            chunk_decay = jnp.exp(lg_ref[hh] * c)
            r_ref[hh] = state * chunk_decay + lax.dot_general(
                kz, vb, (((0,), (0,)), ((), ())), preferred_element_type=F32)
            mu = jnp.mean(o, axis=-1, keepdims=True)
            oc = o - mu
            var = jnp.mean(oc * oc, axis=-1, keepdims=True)
            on = oc * lax.rsqrt(var + GN_EPS) * gn_ref[hh]
            g = g_ref[rows, cs]
            o_ref[rows, cs] = (on * (g * (1.0 / (1.0 + jnp.exp(-g))))).astype(o_ref.dtype)
        return carry

    lax.fori_loop(0, n_chunks, body, 0, unroll=2)


def retention(proj, cos, sin, lg_rows, gn_g, tr=RET_ROWS, hpb=RET_HEADS_PER_STEP):
    s = proj.shape[0]
    dk = RET_HEAD_DIM
    h = RET_HEADS
    tr = min(tr, s)
    c = RET_CHUNK
    w = hpb * dk

    def col(base):
        return pl.BlockSpec((tr, w), lambda hg, ci, base=base: (ci, base // hpb + hg))

    per_head = pl.BlockSpec((hpb, 1, dk), lambda hg, ci: (hg, 0, 0))
    return pl.pallas_call(
        functools.partial(_ret_kernel, n_chunks=tr // c, hpb=hpb),
        grid=(h // hpb, s // tr),
        in_specs=[col(0), col(h), col(2 * h), col(3 * h),
                  pl.BlockSpec((tr, dk // 2), lambda hg, ci: (ci, 0)),
                  pl.BlockSpec((tr, dk // 2), lambda hg, ci: (ci, 0)),
                  per_head, per_head],
        out_specs=pl.BlockSpec((tr, w), lambda hg, ci: (ci, hg)),
        out_shape=jax.ShapeDtypeStruct((s, h * dk), BF16),
        scratch_shapes=[pltpu.VMEM((hpb, dk, dk), F32),
                        pltpu.VMEM((hpb, c, c), F32),
                        pltpu.VMEM((hpb, c, dk), F32),
                        pltpu.VMEM((hpb, c, dk), F32)],
        compiler_params=_params(("arbitrary", "arbitrary")),
        name="retention",
    )(proj, proj, proj, proj, cos, sin, lg_rows, gn_g.reshape(h, 1, dk))


def _sigmoid(x):
    return 1.0 / (1.0 + jnp.exp(-x))


def _lru_kernel(xb_ref, gb_ref, cw_ref, cb_ref, wa_ref, ba_ref, wi_ref, bi_ref, lam_ref, og_ref,
                o_ref, tail_ref, xs_ref, hs_ref, h_ref, wabf_ref, wibf_ref):
    tr, cdim = xb_ref.shape
    nb = wa_ref.shape[0]
    bd = cdim // nb
    ph = SUBLANES
    ng = tr // ph

    @pl.when(pl.program_id(0) == 0)
    def _():
        tail_ref[...] = jnp.zeros_like(tail_ref)
        h_ref[...] = jnp.zeros_like(h_ref)
        wabf_ref[...] = wa_ref[...].astype(BF16)
        wibf_ref[...] = wi_ref[...].astype(BF16)

    lam = lam_ref[...]
    sp = jnp.maximum(-lam, 0.0) + jnp.log1p(jnp.exp(-jnp.abs(lam)))
    rowg = lax.broadcasted_iota(jnp.int32, (ng, bd), 0)
    lpb = bd // LANES
    for c in range(cdim // LANES):
        xs_ref[c] = xb_ref[:, c * LANES:(c + 1) * LANES]

    def phase_rows(ref, n, p):
        return jnp.concatenate([ref[n * lpb + c, pl.ds(p, ng, stride=ph), :] for c in range(lpb)],
                               axis=1)

    for n in range(nb):
        cs = slice(n * bd, (n + 1) * bd)
        x = [phase_rows(xs_ref, n, p) for p in range(ph)]

        def prev_group(p):
            return jnp.where(rowg == 0, tail_ref[p:p + 1, cs], pltpu.roll(x[p], 1, 0))

        back = {-k: prev_group(ph - k) for k in range(1, CONV_WIDTH)}

        def xat(p):
            return x[p] if p >= 0 else back[p]

        xc = []
        for p in range(ph):
            acc = cb_ref[:, cs] + cw_ref[CONV_WIDTH - 1:CONV_WIDTH, cs] * xat(p)
            for k in range(1, CONV_WIDTH):
                acc = acc + cw_ref[CONV_WIDTH - 1 - k:CONV_WIDTH - k, cs] * xat(p - k)
            xc.append(acc)
        xg = jnp.concatenate(xc, axis=0)
        xgb = xg.astype(BF16)
        r = _sigmoid(jnp.dot(xgb, wabf_ref[n], preferred_element_type=F32) + ba_ref[:, cs])
        ig = _sigmoid(jnp.dot(xgb, wibf_ref[n], preferred_element_type=F32) + bi_ref[:, cs])
        log_a = (-RG_C * r) * sp[:, cs]
        a = jnp.exp(log_a)
        b = jnp.sqrt(-jnp.tanh(log_a) * (a * a + 1.0)) * (ig * xg)

        cum_a = [a[0:ng]]
        cum_b = [b[0:ng]]
        for p in range(1, ph):
            ap = a[p * ng:(p + 1) * ng]
            cum_b.append(ap * cum_b[-1] + b[p * ng:(p + 1) * ng])
            cum_a.append(ap * cum_a[-1])
        sa, sb = cum_a[-1], cum_b[-1]
        d = 1
        while d < ng:
            keep = rowg >= d
            sa_sh = pltpu.roll(sa, d, 0)
            sb_sh = pltpu.roll(sb, d, 0)
            sb = jnp.where(keep, sa * sb_sh + sb, sb)
            sa = jnp.where(keep, sa * sa_sh, sa)
            d *= 2
        h_in = h_ref[:, cs]
        h_end = sa * h_in + sb
        h_prev = jnp.where(rowg == 0, h_in, pltpu.roll(h_end, 1, 0))
        for p in range(ph):
            hp = cum_a[p] * h_prev + cum_b[p]
            for c in range(lpb):
                hs_ref[n * lpb + c, pl.ds(p, ng, stride=ph), :] = hp[:, c * LANES:(c + 1) * LANES]
        h_ref[:, cs] = h_end[ng - 1:ng, :]

    gb = gb_ref[...]
    gelu = 0.5 * gb * (1.0 + jnp.tanh(0.7978845608028654 * (gb + 0.044715 * (gb * gb * gb))))
    y = jnp.concatenate([hs_ref[c] for c in range(cdim // LANES)], axis=1) * gelu
    ms = jnp.mean(y * y, axis=-1, keepdims=True)
    o_ref[...] = (y * lax.rsqrt(ms + NORM_EPS) * og_ref[...]).astype(o_ref.dtype)
    tail_ref[...] = xb_ref[tr - ph:tr, :]


def rg_lru(proj, xb_block, gb_block, conv_w, conv_b, w_a, b_a, w_i, b_i, lam, out_g, tr=LRU_ROWS):
    s = proj.shape[0]
    cdim = conv_w.shape[1]
    nb, bd, _ = w_a.shape
    tr = min(tr, s)
    vec = pl.BlockSpec((1, cdim), lambda i: (0, 0))
    wspec = pl.BlockSpec((nb, bd, bd), lambda i: (0, 0, 0))
    return pl.pallas_call(
        _lru_kernel,
        grid=(s // tr,),
        in_specs=[pl.BlockSpec((tr, cdim), lambda i: (i, xb_block)),
                  pl.BlockSpec((tr, cdim), lambda i: (i, gb_block)),
                  pl.BlockSpec((CONV_WIDTH, cdim), lambda i: (0, 0)),
                  vec, wspec, vec, wspec, vec, vec, vec],
        out_specs=pl.BlockSpec((tr, cdim), lambda i: (i, 0)),
        out_shape=jax.ShapeDtypeStruct((s, cdim), BF16),
        scratch_shapes=[pltpu.VMEM((SUBLANES, cdim), F32),
                        pltpu.VMEM((cdim // LANES, tr, LANES), F32),
                        pltpu.VMEM((cdim // LANES, tr, LANES), F32),
                        pltpu.VMEM((1, cdim), F32),
                        pltpu.VMEM((nb, bd, bd), BF16),
                        pltpu.VMEM((nb, bd, bd), BF16)],
        compiler_params=_params(("arbitrary",)),
        name="rg_lru",
    )(proj, proj, conv_w, conv_b.reshape(1, cdim), w_a, b_a.reshape(1, cdim), w_i,
      b_i.reshape(1, cdim), lam.reshape(1, cdim), out_g.reshape(1, cdim))


def _wqk_kernel(wq_ref, k_ref, o_ref):
    o_ref[...] = lax.dot_general(wq_ref[...].astype(BF16), k_ref[...], (((1,), (1,)), ((), ())),
                                 preferred_element_type=F32).astype(o_ref.dtype)


def _vo_kernel(v_ref, wo_ref, o_ref):
    o_ref[...] = jnp.dot(v_ref[...], wo_ref[...].astype(BF16),
                         preferred_element_type=F32).astype(o_ref.dtype)


def xattn_fold(k, v, wq, wo, tile=XF_TILE):
    mlen, d = k.shape
    hd = d // X_HEADS
    tile = min(tile, d)
    wqk = pl.pallas_call(
        _wqk_kernel,
        grid=(X_HEADS, d // tile),
        in_specs=[pl.BlockSpec((tile, hd), lambda h, r: (r, h)),
                  pl.BlockSpec((mlen, hd), lambda h, r: (0, h))],
        out_specs=pl.BlockSpec((tile, mlen), lambda h, r: (r, h)),
        out_shape=jax.ShapeDtypeStruct((d, X_HEADS * mlen), BF16),
        compiler_params=_params(("arbitrary", "arbitrary")),
        name="xattn_wqk",
    )(wq, k)
    vo = pl.pallas_call(
        _vo_kernel,
        grid=(X_HEADS, d // tile),
        in_specs=[pl.BlockSpec((mlen, hd), lambda h, j: (0, h)),
                  pl.BlockSpec((hd, tile), lambda h, j: (h, j))],
        out_specs=pl.BlockSpec((mlen, tile), lambda h, j: (h, j)),
        out_shape=jax.ShapeDtypeStruct((X_HEADS * mlen, d), BF16),
        compiler_params=_params(("arbitrary", "arbitrary")),
        name="xattn_vo",
    )(v, wo)
    return wqk, vo


def _xattn_kernel(x_ref, g_ref, wqk_ref, vo_ref, o_ref):
    d = x_ref.shape[1]
    mlen = wqk_ref.shape[1] // X_HEADS
    scale = (d // X_HEADS) ** -0.5
    x = x_ref[...]
    ms = jnp.mean(x * x, axis=-1, keepdims=True)
    h = (x * lax.rsqrt(ms + NORM_EPS) * g_ref[...]).astype(BF16)
    s = jnp.dot(h, wqk_ref[...], preferred_element_type=F32) * scale
    ps = []
    for hh in range(X_HEADS):
        sh = s[:, hh * mlen:(hh + 1) * mlen]
        m = jnp.max(sh, axis=-1, keepdims=True)
        e = jnp.exp(sh - m)
        ps.append((e / jnp.sum(e, axis=-1, keepdims=True)).astype(BF16))
    p = jnp.concatenate(ps, axis=1)
    o_ref[...] = x + jnp.dot(p, vo_ref[...], preferred_element_type=F32)


def xattn(x, g, wqk, vo, tm=ATT_ROWS):
    s, d = x.shape
    tm = min(tm, s)
    return pl.pallas_call(
        _xattn_kernel,
        grid=(s // tm,),
        in_specs=[pl.BlockSpec((tm, d), lambda i: (i, 0)),
                  pl.BlockSpec((1, d), lambda i: (0, 0)),
                  pl.BlockSpec(wqk.shape, lambda i: (0, 0), pipeline_mode=pl.Buffered(1)),
                  pl.BlockSpec(vo.shape, lambda i: (0, 0), pipeline_mode=pl.Buffered(1))],
        out_specs=pl.BlockSpec((tm, d), lambda i: (i, 0)),
        out_shape=jax.ShapeDtypeStruct((s, d), F32),
        compiler_params=_params(("arbitrary",)),
        name="xattn",
    )(x, g.reshape(1, d), wqk, vo)


def _split_bf16(a):
    hi = a.astype(BF16)
    return hi, (a - hi.astype(F32)).astype(BF16)


def _router_kernel(x_ref, g_ref, wr_ref, br_ref, h_ref, route_ref, counts_ref, carry_ref,
                   wsplit_ref):
    tm = x_ref.shape[0]
    nl = ROUTE_LANES

    @pl.when(pl.program_id(0) == 0)
    def _():
        carry_ref[...] = jnp.zeros_like(carry_ref)
        w_hi, w_lo = _split_bf16(wr_ref[...])
        wsplit_ref[:, :nl] = w_hi
        wsplit_ref[:, nl:] = w_lo

    x = x_ref[...]
    ms = jnp.mean(x * x, axis=-1, keepdims=True)
    h = x * lax.rsqrt(ms + NORM_EPS) * g_ref[...]
    h_ref[...] = h
    h_hi, h_lo = _split_bf16(h)
    both = jnp.dot(h_hi, wsplit_ref[...], preferred_element_type=F32)
    cross = jnp.dot(h_lo, wsplit_ref[:, :nl], preferred_element_type=F32)
    logits = both[:, :nl] + (both[:, nl:] + cross) + br_ref[...]
    lane = lax.broadcasted_iota(jnp.int32, (tm, ROUTE_LANES), 1).astype(F32)
    neg = -jnp.inf
    big = float(ROUTE_LANES)

    gmask = (lane >= GROUP_LANE0) & (lane < GROUP_LANE0 + N_GROUPS)
    gl = jnp.where(gmask, logits, neg)
    gmax = jnp.max(gl, axis=-1, keepdims=True)
    gsum = jnp.sum(jnp.where(gmask, jnp.exp(gl - gmax), 0.0), axis=-1, keepdims=True)
    g_val = 1.0 / gsum
    g_idx = jnp.min(jnp.where(gl == gmax, lane, big), axis=-1, keepdims=True) - GROUP_LANE0

    lo = EXPERT_LANE0 + g_idx * EXPERTS_PER_GROUP
    emask = (lane >= lo) & (lane < lo + EXPERTS_PER_GROUP)
    el = jnp.where(emask, logits, neg)
    t1 = jnp.max(el, axis=-1, keepdims=True)
    i1 = jnp.min(jnp.where(emask & (el == t1), lane, big), axis=-1, keepdims=True)
    emask2 = emask & (lane != i1)
    el2 = jnp.where(emask2, logits, neg)
    t2 = jnp.max(el2, axis=-1, keepdims=True)
    i2 = jnp.min(jnp.where(emask2 & (el2 == t2), lane, big), axis=-1, keepdims=True)
    dexp = jnp.exp(t2 - t1)
    w0 = g_val / (1.0 + dexp)
    w1 = g_val * dexp / (1.0 + dexp)

    sel1 = lane == i1
    sel2 = lane == i2
    onehot = jnp.where(sel1 | sel2, 1.0, 0.0)
    rr = lax.broadcasted_iota(jnp.int32, (tm, tm), 0)
    cc = lax.broadcasted_iota(jnp.int32, (tm, tm), 1)
    tri = jnp.where(cc < rr, 1.0, 0.0).astype(BF16)
    prefix = jnp.dot(tri, onehot.astype(BF16), preferred_element_type=F32) + carry_ref[...]
    rank0 = jnp.sum(jnp.where(sel1, prefix, 0.0), axis=-1, keepdims=True)
    rank1 = jnp.sum(jnp.where(sel2, prefix, 0.0), axis=-1, keepdims=True)
    total = carry_ref[...] + jnp.sum(onehot, axis=0, keepdims=True)
    carry_ref[...] = total
    counts_ref[...] = total

    e0 = i1 - EXPERT_LANE0
    e1 = i2 - EXPERT_LANE0
    route = jnp.where(lane == 0, e0, 0.0)
    route = jnp.where(lane == 1, e1, route)
    route = jnp.where(lane == 2, w0, route)
    route = jnp.where(lane == 3, w1, route)
    route = jnp.where(lane == 4, rank0, route)
    route = jnp.where(lane == 5, rank1, route)
    route_ref[...] = route


def router(x, g, wr, br, tm=ROUTE_ROWS):
    t, d = x.shape
    tm = min(tm, t)
    return pl.pallas_call(
        _router_kernel,
        grid=(t // tm,),
        in_specs=[pl.BlockSpec((tm, d), lambda i: (i, 0)),
                  pl.BlockSpec((1, d), lambda i: (0, 0)),
                  pl.BlockSpec((d, ROUTE_LANES), lambda i: (0, 0)),
                  pl.BlockSpec((1, ROUTE_LANES), lambda i: (0, 0))],
        out_specs=[pl.BlockSpec((tm, d), lambda i: (i, 0)),
                   pl.BlockSpec((tm, ROUTE_LANES), lambda i: (i, 0)),
                   pl.BlockSpec((1, ROUTE_LANES), lambda i: (0, 0))],
        out_shape=[jax.ShapeDtypeStruct((t, d), F32),
                   jax.ShapeDtypeStruct((t, ROUTE_LANES), F32),
                   jax.ShapeDtypeStruct((1, ROUTE_LANES), F32)],
        scratch_shapes=[pltpu.VMEM((1, ROUTE_LANES), F32),
                        pltpu.VMEM((d, 2 * ROUTE_LANES), BF16)],
        compiler_params=_params(("arbitrary",)),
        name="router",
    )(x, g.reshape(1, d), wr, br)


def _gather_kernel(e0_ref, e1_ref, r0_ref, r1_ref, ps_ref, nu_ref, x_hbm, xs_ref,
                   rowtok_ref, buf_ref, sem, *, blk, n_tok):
    i = pl.program_id(0)
    nu = nu_ref[0]

    def row_copy(b, slot, r):
        tok = rowtok_ref[b * blk + r]
        return pltpu.make_async_copy(x_hbm.at[pl.ds(tok, 1), :],
                                     buf_ref.at[slot, pl.ds(r, 1), :], sem.at[slot])

    def for_rows(fn):
        def body(grp, carry):
            r0 = pl.multiple_of(grp * SUBLANES, SUBLANES)
            for k in range(SUBLANES):
                fn(r0 + k)
            return carry
        lax.fori_loop(0, blk // SUBLANES, body, 0)

    def start_block(b, slot):
        for_rows(lambda r: row_copy(b, slot, r).start())

    def wait_block(b, slot):
        for_rows(lambda r: row_copy(b, slot, r).wait())

    @pl.when(i == 0)
    def _():
        n_rows = rowtok_ref.shape[0]
        for base in range(0, n_rows, n_tok):
            def init(r, carry, base=base):
                rowtok_ref[base + r] = r
                return carry
            lax.fori_loop(0, min(n_tok, n_rows - base), init, 0, unroll=SCALAR_UNROLL)

        def fill(t, carry):
            rowtok_ref[ps_ref[e0_ref[t]] + r0_ref[t]] = t
            rowtok_ref[ps_ref[e1_ref[t]] + r1_ref[t]] = t
            return carry
        lax.fori_loop(0, n_tok, fill, 0, unroll=SCALAR_UNROLL)
        start_block(0, 0)

    @pl.when(i + 1 < nu)
    def _():
        start_block(i + 1, (i + 1) % 2)

    @pl.when(i < nu)
    def _():
        slot = i % 2
        wait_block(i, slot)
        xs_ref[...] = buf_ref[slot].astype(xs_ref.dtype)

    @pl.when(i >= nu)
    def _():
        xs_ref[...] = jnp.zeros_like(xs_ref)


def moe_gather(hn, slots, n_used, n_blocks, blk):
    t, d = hn.shape
    grid_spec = pltpu.PrefetchScalarGridSpec(
        num_scalar_prefetch=6,
        grid=(n_blocks,),
        in_specs=[pl.BlockSpec(memory_space=pl.ANY)],
        out_specs=pl.BlockSpec((blk, d), lambda i, *_: (i, 0)),
        scratch_shapes=[pltpu.SMEM((n_blocks * blk,), jnp.int32),
                        pltpu.VMEM((2, blk, d), F32),
                        pltpu.SemaphoreType.DMA((2,))],
    )
    return pl.pallas_call(
        functools.partial(_gather_kernel, blk=blk, n_tok=t),
        grid_spec=grid_spec,
        out_shape=jax.ShapeDtypeStruct((n_blocks * blk, d), BF16),
        compiler_params=_params(("arbitrary",)),
        name="moe_gather",
    )(*slots, n_used, hn)


def _expert_changed(be_ref, i):
    prev = be_ref[jnp.maximum(i - 1, 0)]
    return (i == 0) | (be_ref[i] != prev)


def _stream_expert_weights(w_hbms, col0, be_ref, nx_ref, nu, i, wst_ref, wbf_ref, sem, slot_ref):
    tn = wst_ref.shape[-1]

    def copies(e, slot):
        return [pltpu.make_async_copy(w.at[e, :, pl.ds(col0, tn)], wst_ref.at[slot, l], sem.at[slot])
                for l, w in enumerate(w_hbms)]

    @pl.when(i == 0)
    def _():
        slot_ref[0] = 0
        for c in copies(be_ref[0], 0):
            c.start()

    @pl.when(_expert_changed(be_ref, i))
    def _():
        slot = slot_ref[0]
        for c in copies(be_ref[i], slot):
            c.wait()
        nxt = nx_ref[i]

        @pl.when(nxt < nu)
        def _():
            for c in copies(be_ref[jnp.minimum(nxt, be_ref.shape[0] - 1)], 1 - slot):
                c.start()

        for l in range(len(w_hbms)):
            _cast_rows(wst_ref.at[slot, l], wbf_ref.at[l])
        slot_ref[0] = 1 - slot


def _moe_up_kernel(be_ref, nx_ref, nu_ref, xs_ref, wg_hbm, wu_hbm, act_ref,
                   wst_ref, wbf_ref, sem, slot_ref):
    j = pl.program_id(0)
    i = pl.program_id(1)
    nu = nu_ref[0]
    tf = act_ref.shape[1]

    @pl.when(i < nu)
    def _():
        _stream_expert_weights([wg_hbm, wu_hbm], pl.multiple_of(j * tf, tf), be_ref, nx_ref, nu, i,
                               wst_ref, wbf_ref, sem, slot_ref)
        x = xs_ref[...]
        gate = jnp.dot(x, wbf_ref[0], preferred_element_type=F32)
        up = jnp.dot(x, wbf_ref[1], preferred_element_type=F32)
        act_ref[...] = (gate * _sigmoid(gate) * up).astype(act_ref.dtype)

    @pl.when(i >= nu)
    def _():
        act_ref[...] = jnp.zeros_like(act_ref)


def _moe_down_kernel(be_ref, nx_ref, nu_ref, act_ref, wd_hbm, y_ref, wst_ref, wbf_ref, sem, slot_ref):
    j = pl.program_id(0)
    i = pl.program_id(1)
    nu = nu_ref[0]
    tn = y_ref.shape[1]

    @pl.when(i < nu)
    def _():
        _stream_expert_weights([wd_hbm], pl.multiple_of(j * tn, tn), be_ref, nx_ref, nu, i,
                               wst_ref, wbf_ref, sem, slot_ref)
        y_ref[...] = jnp.dot(act_ref[...], wbf_ref[0], preferred_element_type=F32)

    @pl.when(i >= nu)
    def _():
        y_ref[...] = jnp.zeros_like(y_ref)


def moe_experts(xs, block_expert, next_expert_block, n_used, w_gate, w_up, w_down,
                blk=MOE_BLK, tf=MOE_TF, tn=MOE_TN):
    r, dw = xs.shape
    _, d, f = w_gate.shape
    n_blocks = r // blk
    tf = min(tf, f)
    tn = min(tn, d)

    def used(i, nu):
        return jnp.minimum(i, jnp.maximum(nu[0] - 1, 0))

    def stream_scratch(n_mats, k, n):
        return [pltpu.VMEM((2, n_mats, k, n), F32),
                pltpu.VMEM((n_mats, k, n), BF16),
                pltpu.SemaphoreType.DMA((2,)),
                pltpu.SMEM((1,), jnp.int32)]

    up_spec = pltpu.PrefetchScalarGridSpec(
        num_scalar_prefetch=3,
        grid=(f // tf, n_blocks),
        in_specs=[pl.BlockSpec((blk, dw), lambda j, i, be, nx, nu: (used(i, nu), 0)),
                  pl.BlockSpec(memory_space=pl.ANY),
                  pl.BlockSpec(memory_space=pl.ANY)],
        out_specs=pl.BlockSpec((blk, tf), lambda j, i, be, nx, nu: (i, j)),
        scratch_shapes=stream_scratch(2, d, tf),
    )
    act = pl.pallas_call(
        _moe_up_kernel,
        grid_spec=up_spec,
        out_shape=jax.ShapeDtypeStruct((r, f), BF16),
        compiler_params=_params(("arbitrary", "arbitrary")),
        name="moe_up",
    )(block_expert, next_expert_block, n_used, xs, w_gate, w_up)
    down_spec = pltpu.PrefetchScalarGridSpec(
        num_scalar_prefetch=3,
        grid=(d // tn, n_blocks),
        in_specs=[pl.BlockSpec((blk, f), lambda j, i, be, nx, nu: (used(i, nu), 0)),
                  pl.BlockSpec(memory_space=pl.ANY)],
        out_specs=pl.BlockSpec((blk, tn), lambda j, i, be, nx, nu: (i, j)),
        scratch_shapes=stream_scratch(1, f, tn),
    )
    return pl.pallas_call(
        _moe_down_kernel,
        grid_spec=down_spec,
        out_shape=jax.ShapeDtypeStruct((r, d), F32),
        compiler_params=_params(("arbitrary", "arbitrary")),
        name="moe_down",
    )(block_expert, next_expert_block, n_used, act, w_down)


def _combine_kernel(e0_ref, e1_ref, r0_ref, r1_ref, ps_ref, x_ref, route_ref, g_ref, y_hbm, o_ref,
                    ya_ref, yb_ref, sem, *, tb, final_norm):
    step = pl.program_id(0)

    def copies(b, slot, i):
        t = b * tb + i
        row0 = ps_ref[e0_ref[t]] + r0_ref[t]
        row1 = ps_ref[e1_ref[t]] + r1_ref[t]
        return (pltpu.make_async_copy(y_hbm.at[pl.ds(row0, 1), :],
                                      ya_ref.at[slot, pl.ds(i, 1), :], sem.at[slot]),
                pltpu.make_async_copy(y_hbm.at[pl.ds(row1, 1), :],
                                      yb_ref.at[slot, pl.ds(i, 1), :], sem.at[slot]))

    def for_rows(fn):
        def body(grp, carry):
            r0 = pl.multiple_of(grp * SUBLANES, SUBLANES)
            for k in range(SUBLANES):
                fn(r0 + k)
            return carry
        lax.fori_loop(0, tb // SUBLANES, body, 0)

    def start_block(b, slot):
        def start(i):
            c0, c1 = copies(b, slot, i)
            c0.start()
            c1.start()
        for_rows(start)

    def wait_block(b, slot):
        def wait(i):
            c0, c1 = copies(b, slot, i)
            c0.wait()
            c1.wait()
        for_rows(wait)

    @pl.when(step == 0)
    def _():
        start_block(0, 0)

    @pl.when(step + 1 < pl.num_programs(0))
    def _():
        start_block(step + 1, (step + 1) % 2)

    slot = step % 2
    wait_block(step, slot)
    w0 = route_ref[:, 2:3]
    w1 = route_ref[:, 3:4]
    x = x_ref[...] + (ya_ref[slot] * w0 + yb_ref[slot] * w1)
    if final_norm:
        ms = jnp.mean(x * x, axis=-1, keepdims=True)
        x = x * lax.rsqrt(ms + NORM_EPS) * g_ref[...]
    o_ref[...] = x


def combine(x, route, y, slots, g, final_norm, tb=COMB_ROWS):
    t, d = x.shape
    tb = min(tb, t)
    grid_spec = pltpu.PrefetchScalarGridSpec(
        num_scalar_prefetch=5,
        grid=(t // tb,),
        in_specs=[pl.BlockSpec((tb, d), lambda i, *_: (i, 0)),
                  pl.BlockSpec((tb, ROUTE_LANES), lambda i, *_: (i, 0)),
                  pl.BlockSpec((1, d), lambda i, *_: (0, 0)),
                  pl.BlockSpec(memory_space=pl.ANY)],
        out_specs=pl.BlockSpec((tb, d), lambda i, *_: (i, 0)),
        scratch_shapes=[pltpu.VMEM((2, tb, d), F32), pltpu.VMEM((2, tb, d), F32),
                        pltpu.SemaphoreType.DMA((2,))],
    )
    return pl.pallas_call(
        functools.partial(_combine_kernel, tb=tb, final_norm=final_norm),
        grid_spec=grid_spec,
        out_shape=jax.ShapeDtypeStruct((t, d), F32),
        compiler_params=_params(("arbitrary",)),
        name="moe_combine",
    )(*slots, x, route, g.reshape(1, d), y)


def _route_lanes(group_part, expert_part):
    rows = group_part.shape[0]
    gap = jnp.zeros((rows, EXPERT_LANE0 - GROUP_LANE0 - N_GROUPS), F32)
    tail = jnp.zeros((rows, ROUTE_LANES - EXPERT_LANE0 - N_EXPERTS), F32)
    return jnp.concatenate([group_part, gap, expert_part, tail], axis=1)


def _moe_layout(route, counts, blk):
    t = route.shape[0]
    ri = route[:, :8].astype(jnp.int32)
    e0, e1, rank0, rank1 = ri[:, 0], ri[:, 1], ri[:, 4], ri[:, 5]
    cnt = counts[0, EXPERT_LANE0:EXPERT_LANE0 + N_EXPERTS].astype(jnp.int32)
    padded = (cnt + blk - 1) // blk * blk
    pends = jnp.cumsum(padded)
    pstarts = pends - padded
    n_blocks = (2 * t) // blk + N_EXPERTS
    block_start = jnp.arange(n_blocks, dtype=jnp.int32) * blk
    block_expert = jnp.minimum(
        jnp.sum((block_start[:, None] >= pends[None, :]).astype(jnp.int32), axis=1), N_EXPERTS - 1)
    n_used = (pends[-1] // blk).astype(jnp.int32).reshape(1)
    block_expert = block_expert[jnp.minimum(jnp.arange(n_blocks), jnp.maximum(n_used[0] - 1, 0))]
    next_expert_block = pends[block_expert] // blk
    return (e0, e1, rank0, rank1, pstarts), block_expert, next_expert_block, n_used, n_blocks


def kernel(x, mem, positions, mix_norm_g, w_in, ret_norm_g, lru_conv_w, lru_conv_b, lru_w_a, lru_b_a, lru_w_i, lru_b_i, lru_lambda, lru_norm_g, w_out, xattn_norm_g, mem_norm_g, xattn_wq, xattn_wk, xattn_wv, xattn_wo, moe_norm_g, router_group_w, router_group_b, router_expert_w, router_expert_b, expert_w_gate, expert_w_up, expert_w_down, final_norm_g):
    b, s, d = x.shape
    depth = w_in.shape[0]
    ret_width = RET_HEADS * RET_HEAD_DIM
    lru_width = lru_conv_w.shape[-1]
    assert ret_width == lru_width and ret_width + lru_width == d
    inv_freq = ROPE_BASE ** (-jnp.arange(0, RET_HEAD_DIM, 2, dtype=F32) / RET_HEAD_DIM)
    lg = jnp.log1p(-jnp.exp2(-5.0 - jnp.arange(RET_HEADS, dtype=F32)))
    lg_rows = jnp.broadcast_to(lg[:, None, None], (RET_HEADS, 1, RET_HEAD_DIM))
    blk = min(MOE_BLK, s)
    outs = []
    for bi in range(b):
        xcur = x[bi]
        cos, sin = rope_tables(positions[bi].astype(F32), inv_freq)
        for l in range(depth):
            h = normcast(xcur, mix_norm_g[l], BF16, NORM_ROWS)
            proj = matmul([h], w_in[l], F32, tm=MM_WIDE_TM, tn=MM_WIDE_TN)
            ret = retention(proj, cos, sin, lg_rows, ret_norm_g[l])
            lru = rg_lru(proj, 4 * ret_width // lru_width, 4 * ret_width // lru_width + 1,
                         lru_conv_w[l], lru_conv_b[l], lru_w_a[l], lru_b_a[l], lru_w_i[l],
                         lru_b_i[l], lru_lambda[l], lru_norm_g[l])
            xcur = matmul([ret, lru], w_out[l], F32, res=xcur)
            memn = normcast(mem[bi], mem_norm_g[l], BF16, NORM_ROWS)
            kk = matmul([memn], xattn_wk[l], BF16)
            vv = matmul([memn], xattn_wv[l], BF16)
            wqk, vo = xattn_fold(kk, vv, xattn_wq[l], xattn_wo[l])
            xcur = xattn(xcur, xattn_norm_g[l], wqk, vo)
            wr = _route_lanes(router_group_w[l], router_expert_w[l])
            br = _route_lanes(router_group_b[l][None], router_expert_b[l][None])
            hn, route, counts = router(xcur, moe_norm_g[l], wr, br)
            slots, block_expert, next_block, n_used, n_blocks = _moe_layout(route, counts, blk)
            xs = moe_gather(hn, slots, n_used, n_blocks, blk)
            y = moe_experts(xs, block_expert, next_block, n_used, expert_w_gate[l], expert_w_up[l],
                            expert_w_down[l], blk=blk)
            xcur = combine(xcur, route, y, slots, final_norm_g, final_norm=l == depth - 1)
        outs.append(xcur)
    return outs[0][None] if b == 1 else jnp.stack(outs, axis=0)
```

```python
import functools

import jax
import jax.numpy as jnp
from jax import lax
from jax.experimental import pallas as pl
from jax.experimental.pallas import tpu as pltpu

F32 = jnp.float32
BF16 = jnp.bfloat16

RET_HEADS = 8
RET_HEAD_DIM = 256
RET_CHUNK = 128
LRU_BLOCKS = 8
CONV_WIDTH = 4
RG_C = 8.0
ROPE_BASE = 10000.0
X_HEADS = 4
N_GROUPS = 4
EXPERTS_PER_GROUP = 8
N_EXPERTS = N_GROUPS * EXPERTS_PER_GROUP
NORM_EPS = 1e-6
GN_EPS = 1e-5

LANES = 128
SUBLANES = 8
VMEM_LIMIT = 56 * 1024 * 1024

NORM_ROWS = 512
MM_TM = 1024
MM_TN = 512
MM_STREAM_TM = 1024
MM_STREAM_TN = 1024
RET_ROWS = 512
RET_HEADS_PER_STEP = 8
LRU_ROWS = 256
ATT_ROWS = 512
ROUTE_ROWS = 256
MOE_BLK = 256
MOE_TF = 512
MOE_TN = 4096
XF_TILE = 1024
SCALAR_UNROLL = 8
COMB_ROWS = 256
ROUTE_LANES = LANES
GROUP_LANE0 = 0
EXPERT_LANE0 = 8


def _params(sem):
    return pltpu.CompilerParams(dimension_semantics=sem, vmem_limit_bytes=VMEM_LIMIT)


def _normcast_kernel(x_ref, g_ref, o_ref):
    x = x_ref[...]
    ms = jnp.mean(x * x, axis=-1, keepdims=True)
    o_ref[...] = (x * lax.rsqrt(ms + NORM_EPS) * g_ref[...]).astype(o_ref.dtype)


def normcast(x, g, out_dtype, tm):
    m, d = x.shape
    tm = min(tm, m)
    return pl.pallas_call(
        _normcast_kernel,
        grid=(m // tm,),
        in_specs=[pl.BlockSpec((tm, d), lambda i: (i, 0)),
                  pl.BlockSpec((1, d), lambda i: (0, 0))],
        out_specs=pl.BlockSpec((tm, d), lambda i: (i, 0)),
        out_shape=jax.ShapeDtypeStruct((m, d), out_dtype),
        compiler_params=_params(("arbitrary",)),
        name="normcast",
    )(x, g.reshape(1, d))


def _cast_rows(src_ref, dst_ref, rows_per_iter=256):
    k = src_ref.shape[0]
    step = min(rows_per_iter, k)

    def body(i, carry):
        r0 = pl.multiple_of(i * step, step)
        dst_ref[pl.ds(r0, step), :] = src_ref[pl.ds(r0, step), :].astype(dst_ref.dtype)
        return carry

    lax.fori_loop(0, k // step, body, 0)


def _mm_kernel(*refs, n_a, has_res):
    a_refs = refs[:n_a]
    w_ref = refs[n_a]
    res_ref = refs[n_a + 1] if has_res else None
    o_ref = refs[n_a + 1 + int(has_res)]
    wbf_ref = refs[n_a + 2 + int(has_res)]

    @pl.when(pl.program_id(1) == 0)
    def _():
        _cast_rows(w_ref, wbf_ref)

    kp = a_refs[0].shape[1]
    acc = None
    for p, a_ref in enumerate(a_refs):
        d = jnp.dot(a_ref[...], wbf_ref[p * kp:(p + 1) * kp, :], preferred_element_type=F32)
        acc = d if acc is None else acc + d
    if has_res:
        acc = acc + res_ref[...]
    o_ref[...] = acc.astype(o_ref.dtype)


def matmul(a_parts, w, out_dtype, res=None, tm=MM_TM, tn=MM_TN):
    m, kp = a_parts[0].shape
    k, n = w.shape
    assert kp * len(a_parts) == k
    tm = min(tm, m)
    tn = min(tn, n)
    in_specs = [pl.BlockSpec((tm, kp), lambda j, i: (i, 0)) for _ in a_parts]
    in_specs.append(pl.BlockSpec((k, tn), lambda j, i: (0, j)))
    args = list(a_parts) + [w]
    if res is not None:
        in_specs.append(pl.BlockSpec((tm, tn), lambda j, i: (i, j)))
        args.append(res)
    return pl.pallas_call(
        functools.partial(_mm_kernel, n_a=len(a_parts), has_res=res is not None),
        grid=(n // tn, m // tm),
        in_specs=in_specs,
        out_specs=pl.BlockSpec((tm, tn), lambda j, i: (i, j)),
        out_shape=jax.ShapeDtypeStruct((m, n), out_dtype),
        scratch_shapes=[pltpu.VMEM((k, tn), BF16)],
        compiler_params=_params(("arbitrary", "arbitrary")),
        name="matmul",
    )(*args)


def _mm_stream_kernel(*refs, n_a, has_res):
    a_refs = refs[:n_a]
    w_hbm = refs[n_a]
    res_ref = refs[n_a + 1] if has_res else None
    o_ref = refs[n_a + 1 + int(has_res)]
    wbf_ref, stage_ref, sem = refs[n_a + 2 + int(has_res):]
    j = pl.program_id(0)
    i = pl.program_id(1)
    n_j = pl.num_programs(0)
    n_i = pl.num_programs(1)
    kc, tn = stage_ref.shape[1:]
    n_chunks = wbf_ref.shape[1] // kc

    def chunk_copy(col_tile, c, slot):
        rows = pl.ds(pl.multiple_of(c * kc, kc), kc)
        cols = pl.ds(pl.multiple_of(col_tile * tn, tn), tn)
        return pltpu.make_async_copy(w_hbm.at[rows, cols], stage_ref.at[slot], sem.at[slot])

    def cast_chunk(buf, c, slot):
        rows = pl.ds(pl.multiple_of(c * kc, kc), kc)
        wbf_ref[buf, rows, :] = stage_ref[slot].astype(BF16)

    @pl.when((j == 0) & (i == 0))
    def _():
        chunk_copy(0, 0, 0).start()
        for c in range(n_chunks):
            if c + 1 < n_chunks:
                chunk_copy(0, c + 1, (c + 1) % 2).start()
            chunk_copy(0, c, c % 2).wait()
            cast_chunk(0, c, c % 2)

    cur = j % 2
    kp = a_refs[0].shape[1]
    acc = None
    for p, a_ref in enumerate(a_refs):
        d = jnp.dot(a_ref[...], wbf_ref[cur, p * kp:(p + 1) * kp, :], preferred_element_type=F32)
        acc = d if acc is None else acc + d
    if has_res:
        acc = acc + res_ref[...]
    o_ref[...] = acc.astype(o_ref.dtype)

    @pl.when(j + 1 < n_j)
    def _():
        slot = i % 2

        @pl.when(i == 0)
        def _():
            chunk_copy(j + 1, 0, 0).start()

        chunk_copy(j + 1, i, slot).wait()

        @pl.when(i + 1 < n_i)
        def _():
            chunk_copy(j + 1, i + 1, 1 - slot).start()

        cast_chunk(1 - cur, i, slot)


def matmul_streamed(a_parts, w, out_dtype, res=None, tm=MM_STREAM_TM, tn=MM_STREAM_TN):
    m, kp = a_parts[0].shape
    k, n = w.shape
    assert kp * len(a_parts) == k
    tm = min(tm, m)
    tn = min(tn, n)
    n_i = m // tm
    kc = k // n_i
    assert kc * n_i == k and kc % SUBLANES == 0
    in_specs = [pl.BlockSpec((tm, kp), lambda j, i: (i, 0)) for _ in a_parts]
    in_specs.append(pl.BlockSpec(memory_space=pl.ANY))
    args = list(a_parts) + [w]
    if res is not None:
        in_specs.append(pl.BlockSpec((tm, tn), lambda j, i: (i, j)))
        args.append(res)
    return pl.pallas_call(
        functools.partial(_mm_stream_kernel, n_a=len(a_parts), has_res=res is not None),
        grid=(n // tn, n_i),
        in_specs=in_specs,
        out_specs=pl.BlockSpec((tm, tn), lambda j, i: (i, j)),
        out_shape=jax.ShapeDtypeStruct((m, n), out_dtype),
        scratch_shapes=[pltpu.VMEM((2, k, tn), BF16),
                        pltpu.VMEM((2, kc, tn), F32),
                        pltpu.SemaphoreType.DMA((2,))],
        compiler_params=_params(("arbitrary", "arbitrary")),
        name="matmul_streamed",
    )(*args)


def _rope_kernel(pos_ref, invf_ref, cos_ref, sin_ref):
    ang = pos_ref[...] * invf_ref[...]
    cos_ref[...] = jnp.cos(ang)
    sin_ref[...] = jnp.sin(ang)


def rope_tables(pos_f, inv_freq, tm=512):
    s = pos_f.shape[0]
    hd = inv_freq.shape[0]
    tm = min(tm, s)
    return pl.pallas_call(
        _rope_kernel,
        grid=(s // tm,),
        in_specs=[pl.BlockSpec((tm, 1), lambda i: (i, 0)),
                  pl.BlockSpec((1, hd), lambda i: (0, 0))],
        out_specs=[pl.BlockSpec((tm, hd), lambda i: (i, 0))] * 2,
        out_shape=[jax.ShapeDtypeStruct((s, hd), F32)] * 2,
        compiler_params=_params(("arbitrary",)),
        name="rope_tables",
    )(pos_f.reshape(s, 1), inv_freq.reshape(1, hd))


def _ret_kernel(q_ref, k_ref, v_ref, g_ref, cos_ref, sin_ref, lg_ref, gn_ref, o_ref, r_ref,
                decay_ref, xi_ref, zeta_ref, *, n_chunks, hpb):
    c = RET_CHUNK
    dk = RET_HEAD_DIM
    half = dk // 2
    scale = dk ** -0.5

    @pl.when(pl.program_id(1) == 0)
    def _():
        r_ref[...] = jnp.zeros_like(r_ref)
        row = lax.broadcasted_iota(jnp.int32, (c, c), 0).astype(F32)
        col = lax.broadcasted_iota(jnp.int32, (c, c), 1).astype(F32)
        diff = row - col
        rowk = lax.broadcasted_iota(jnp.int32, (c, dk), 0).astype(F32)
        for hh in range(hpb):
            lg = lg_ref[hh]
            decay_ref[hh] = jnp.where(diff >= 0, jnp.exp(lg[:, :c] * jnp.maximum(diff, 0.0)), 0.0)
            xi_ref[hh] = jnp.exp(lg * (rowk + 1.0))
            zeta_ref[hh] = jnp.exp(lg * (c - 1.0 - rowk))

    def rope(t, cos, sin):
        t1 = t[:, :half]
        t2 = t[:, half:]
        return jnp.concatenate([t1 * cos - t2 * sin, t1 * sin + t2 * cos], axis=-1)

    def body(j, carry):
        r0 = pl.multiple_of(j * c, c)
        rows = pl.ds(r0, c)
        cos = cos_ref[rows, :]
        sin = sin_ref[rows, :]
        for hh in range(hpb):
            cs = slice(hh * dk, (hh + 1) * dk)
            qr = rope(q_ref[rows, cs], cos, sin)
            kr = rope(k_ref[rows, cs], cos, sin) * scale
            qb = qr.astype(BF16)
            kb = kr.astype(BF16)
            vb = v_ref[rows, cs].astype(BF16)
            state = r_ref[hh]
            inner = lax.dot_general(qb, kb, (((1,), (1,)), ((), ())),
                                    preferred_element_type=F32) * decay_ref[hh]
            o = (jnp.dot(inner.astype(BF16), vb, preferred_element_type=F32)
                 + jnp.dot(qb, state.astype(BF16), preferred_element_type=F32) * xi_ref[hh])
            kz = (kr * zeta_ref[hh]).astype(BF16)
            chunk_decay = jnp.exp(lg_ref[hh] * c)
            r_ref[hh] = state * chunk_decay + lax.dot_general(
                kz, vb, (((0,), (0,)), ((), ())), preferred_element_type=F32)
            mu = jnp.mean(o, axis=-1, keepdims=True)
            oc = o - mu
            var = jnp.mean(oc * oc, axis=-1, keepdims=True)
            on = oc * lax.rsqrt(var + GN_EPS) * gn_ref[hh]
            g = g_ref[rows, cs]
            o_ref[rows, cs] = (on * (g * (1.0 / (1.0 + jnp.exp(-g))))).astype(o_ref.dtype)
        return carry

    lax.fori_loop(0, n_chunks, body, 0, unroll=2)


def retention(proj, cos, sin, lg_rows, gn_g, tr=RET_ROWS, hpb=RET_HEADS_PER_STEP):
    s = proj.shape[0]
    dk = RET_HEAD_DIM
    h = RET_HEADS
    tr = min(tr, s)
    c = RET_CHUNK
    w = hpb * dk

    def col(base):
        return pl.BlockSpec((tr, w), lambda hg, ci, base=base: (ci, base // hpb + hg))

    per_head = pl.BlockSpec((hpb, 1, dk), lambda hg, ci: (hg, 0, 0))
    return pl.pallas_call(
        functools.partial(_ret_kernel, n_chunks=tr // c, hpb=hpb),
        grid=(h // hpb, s // tr),
        in_specs=[col(0), col(h), col(2 * h), col(3 * h),
                  pl.BlockSpec((tr, dk // 2), lambda hg, ci: (ci, 0)),
                  pl.BlockSpec((tr, dk // 2), lambda hg, ci: (ci, 0)),
                  per_head, per_head],
        out_specs=pl.BlockSpec((tr, w), lambda hg, ci: (ci, hg)),
        out_shape=jax.ShapeDtypeStruct((s, h * dk), BF16),
        scratch_shapes=[pltpu.VMEM((hpb, dk, dk), F32),
                        pltpu.VMEM((hpb, c, c), F32),
                        pltpu.VMEM((hpb, c, dk), F32),
                        pltpu.VMEM((hpb, c, dk), F32)],
        compiler_params=_params(("arbitrary", "arbitrary")),
        name="retention",
    )(proj, proj, proj, proj, cos, sin, lg_rows, gn_g.reshape(h, 1, dk))


def _sigmoid(x):
    return 1.0 / (1.0 + jnp.exp(-x))


def _lru_kernel(xb_ref, gb_ref, cw_ref, cb_ref, wa_ref, ba_ref, wi_ref, bi_ref, lam_ref, og_ref,
                o_ref, tail_ref, xs_ref, hs_ref, h_ref, wabf_ref, wibf_ref):
    tr, cdim = xb_ref.shape
    nb = wa_ref.shape[0]
    bd = cdim // nb
    ph = SUBLANES
    ng = tr // ph

    @pl.when(pl.program_id(0) == 0)
    def _():
        tail_ref[...] = jnp.zeros_like(tail_ref)
        h_ref[...] = jnp.zeros_like(h_ref)
        wabf_ref[...] = wa_ref[...].astype(BF16)
        wibf_ref[...] = wi_ref[...].astype(BF16)

    lam = lam_ref[...]
    sp = jnp.maximum(-lam, 0.0) + jnp.log1p(jnp.exp(-jnp.abs(lam)))
    rowg = lax.broadcasted_iota(jnp.int32, (ng, bd), 0)
    lpb = bd // LANES
    for c in range(cdim // LANES):
        xs_ref[c] = xb_ref[:, c * LANES:(c + 1) * LANES]

    def phase_rows(ref, n, p):
        return jnp.concatenate([ref[n * lpb + c, pl.ds(p, ng, stride=ph), :] for c in range(lpb)],
                               axis=1)

    for n in range(nb):
        cs = slice(n * bd, (n + 1) * bd)
        x = [phase_rows(xs_ref, n, p) for p in range(ph)]

        def prev_group(p):
            return jnp.where(rowg == 0, tail_ref[p:p + 1, cs], pltpu.roll(x[p], 1, 0))

        back = {-k: prev_group(ph - k) for k in range(1, CONV_WIDTH)}

        def xat(p):
            return x[p] if p >= 0 else back[p]

        xc = []
        for p in range(ph):
            acc = cb_ref[:, cs] + cw_ref[CONV_WIDTH - 1:CONV_WIDTH, cs] * xat(p)
            for k in range(1, CONV_WIDTH):
                acc = acc + cw_ref[CONV_WIDTH - 1 - k:CONV_WIDTH - k, cs] * xat(p - k)
            xc.append(acc)
        xg = jnp.concatenate(xc, axis=0)
        xgb = xg.astype(BF16)
        r = _sigmoid(jnp.dot(xgb, wabf_ref[n], preferred_element_type=F32) + ba_ref[:, cs])
        ig = _sigmoid(jnp.dot(xgb, wibf_ref[n], preferred_element_type=F32) + bi_ref[:, cs])
        log_a = (-RG_C * r) * sp[:, cs]
        a = jnp.exp(log_a)
        b = jnp.sqrt(-jnp.tanh(log_a) * (a * a + 1.0)) * (ig * xg)

        cum_a = [a[0:ng]]
        cum_b = [b[0:ng]]
        for p in range(1, ph):
            ap = a[p * ng:(p + 1) * ng]
            cum_b.append(ap * cum_b[-1] + b[p * ng:(p + 1) * ng])
            cum_a.append(ap * cum_a[-1])
        sa, sb = cum_a[-1], cum_b[-1]
        d = 1
        while d < ng:
            keep = rowg >= d
            sa_sh = pltpu.roll(sa, d, 0)
            sb_sh = pltpu.roll(sb, d, 0)
            sb = jnp.where(keep, sa * sb_sh + sb, sb)
            sa = jnp.where(keep, sa * sa_sh, sa)
            d *= 2
        h_in = h_ref[:, cs]
        h_end = sa * h_in + sb
        h_prev = jnp.where(rowg == 0, h_in, pltpu.roll(h_end, 1, 0))
        for p in range(ph):
            hp = cum_a[p] * h_prev + cum_b[p]
            for c in range(lpb):
                hs_ref[n * lpb + c, pl.ds(p, ng, stride=ph), :] = hp[:, c * LANES:(c + 1) * LANES]
        h_ref[:, cs] = h_end[ng - 1:ng, :]

    gb = gb_ref[...]
    gelu = 0.5 * gb * (1.0 + jnp.tanh(0.7978845608028654 * (gb + 0.044715 * (gb * gb * gb))))
    y = jnp.concatenate([hs_ref[c] for c in range(cdim // LANES)], axis=1) * gelu
    ms = jnp.mean(y * y, axis=-1, keepdims=True)
    o_ref[...] = (y * lax.rsqrt(ms + NORM_EPS) * og_ref[...]).astype(o_ref.dtype)
    tail_ref[...] = xb_ref[tr - ph:tr, :]


def rg_lru(proj, xb_block, gb_block, conv_w, conv_b, w_a, b_a, w_i, b_i, lam, out_g, tr=LRU_ROWS):
    s = proj.shape[0]
    cdim = conv_w.shape[1]
    nb, bd, _ = w_a.shape
    tr = min(tr, s)
    vec = pl.BlockSpec((1, cdim), lambda i: (0, 0))
    wspec = pl.BlockSpec((nb, bd, bd), lambda i: (0, 0, 0))
    return pl.pallas_call(
        _lru_kernel,
        grid=(s // tr,),
        in_specs=[pl.BlockSpec((tr, cdim), lambda i: (i, xb_block)),
                  pl.BlockSpec((tr, cdim), lambda i: (i, gb_block)),
                  pl.BlockSpec((CONV_WIDTH, cdim), lambda i: (0, 0)),
                  vec, wspec, vec, wspec, vec, vec, vec],
        out_specs=pl.BlockSpec((tr, cdim), lambda i: (i, 0)),
        out_shape=jax.ShapeDtypeStruct((s, cdim), BF16),
        scratch_shapes=[pltpu.VMEM((SUBLANES, cdim), F32),
                        pltpu.VMEM((cdim // LANES, tr, LANES), F32),
                        pltpu.VMEM((cdim // LANES, tr, LANES), F32),
                        pltpu.VMEM((1, cdim), F32),
                        pltpu.VMEM((nb, bd, bd), BF16),
                        pltpu.VMEM((nb, bd, bd), BF16)],
        compiler_params=_params(("arbitrary",)),
        name="rg_lru",
    )(proj, proj, conv_w, conv_b.reshape(1, cdim), w_a, b_a.reshape(1, cdim), w_i,
      b_i.reshape(1, cdim), lam.reshape(1, cdim), out_g.reshape(1, cdim))


def _wqk_kernel(wq_ref, k_ref, o_ref):
    o_ref[...] = lax.dot_general(wq_ref[...].astype(BF16), k_ref[...], (((1,), (1,)), ((), ())),
                                 preferred_element_type=F32).astype(o_ref.dtype)


def _vo_kernel(v_ref, wo_ref, o_ref):
    o_ref[...] = jnp.dot(v_ref[...], wo_ref[...].astype(BF16),
                         preferred_element_type=F32).astype(o_ref.dtype)


def xattn_fold(k, v, wq, wo, tile=XF_TILE):
    mlen, d = k.shape
    hd = d // X_HEADS
    tile = min(tile, d)
    wqk = pl.pallas_call(
        _wqk_kernel,
        grid=(X_HEADS, d // tile),
        in_specs=[pl.BlockSpec((tile, hd), lambda h, r: (r, h)),
                  pl.BlockSpec((mlen, hd), lambda h, r: (0, h))],
        out_specs=pl.BlockSpec((tile, mlen), lambda h, r: (r, h)),
        out_shape=jax.ShapeDtypeStruct((d, X_HEADS * mlen), BF16),
        compiler_params=_params(("arbitrary", "arbitrary")),
        name="xattn_wqk",
    )(wq, k)
    vo = pl.pallas_call(
        _vo_kernel,
        grid=(X_HEADS, d // tile),
        in_specs=[pl.BlockSpec((mlen, hd), lambda h, j: (0, h)),
                  pl.BlockSpec((hd, tile), lambda h, j: (h, j))],
        out_specs=pl.BlockSpec((mlen, tile), lambda h, j: (h, j)),
        out_shape=jax.ShapeDtypeStruct((X_HEADS * mlen, d), BF16),
        compiler_params=_params(("arbitrary", "arbitrary")),
        name="xattn_vo",
    )(v, wo)
    return wqk, vo


def _xattn_kernel(x_ref, g_ref, wqk_ref, vo_ref, o_ref):
    d = x_ref.shape[1]
    mlen = wqk_ref.shape[1] // X_HEADS
    scale = (d // X_HEADS) ** -0.5
    x = x_ref[...]
    ms = jnp.mean(x * x, axis=-1, keepdims=True)
    h = (x * lax.rsqrt(ms + NORM_EPS) * g_ref[...]).astype(BF16)
    s = jnp.dot(h, wqk_ref[...], preferred_element_type=F32) * scale
    ps = []
    for hh in range(X_HEADS):
        sh = s[:, hh * mlen:(hh + 1) * mlen]
        m = jnp.max(sh, axis=-1, keepdims=True)
        e = jnp.exp(sh - m)
        ps.append((e / jnp.sum(e, axis=-1, keepdims=True)).astype(BF16))
    p = jnp.concatenate(ps, axis=1)
    o_ref[...] = x + jnp.dot(p, vo_ref[...], preferred_element_type=F32)


def xattn(x, g, wqk, vo, tm=ATT_ROWS):
    s, d = x.shape
    tm = min(tm, s)
    return pl.pallas_call(
        _xattn_kernel,
        grid=(s // tm,),
        in_specs=[pl.BlockSpec((tm, d), lambda i: (i, 0)),
                  pl.BlockSpec((1, d), lambda i: (0, 0)),
                  pl.BlockSpec(wqk.shape, lambda i: (0, 0), pipeline_mode=pl.Buffered(1)),
                  pl.BlockSpec(vo.shape, lambda i: (0, 0), pipeline_mode=pl.Buffered(1))],
        out_specs=pl.BlockSpec((tm, d), lambda i: (i, 0)),
        out_shape=jax.ShapeDtypeStruct((s, d), F32),
        compiler_params=_params(("arbitrary",)),
        name="xattn",
    )(x, g.reshape(1, d), wqk, vo)


def _split_bf16(a):
    hi = a.astype(BF16)
    return hi, (a - hi.astype(F32)).astype(BF16)


def _router_kernel(x_ref, g_ref, wr_ref, br_ref, h_ref, route_ref, counts_ref, carry_ref,
                   wsplit_ref):
    tm = x_ref.shape[0]
    nl = ROUTE_LANES

    @pl.when(pl.program_id(0) == 0)
    def _():
        carry_ref[...] = jnp.zeros_like(carry_ref)
        w_hi, w_lo = _split_bf16(wr_ref[...])
        wsplit_ref[:, :nl] = w_hi
        wsplit_ref[:, nl:] = w_lo

    x = x_ref[...]
    ms = jnp.mean(x * x, axis=-1, keepdims=True)
    h = x * lax.rsqrt(ms + NORM_EPS) * g_ref[...]
    h_ref[...] = h
    h_hi, h_lo = _split_bf16(h)
    both = jnp.dot(h_hi, wsplit_ref[...], preferred_element_type=F32)
    cross = jnp.dot(h_lo, wsplit_ref[:, :nl], preferred_element_type=F32)
    logits = both[:, :nl] + (both[:, nl:] + cross) + br_ref[...]
    lane = lax.broadcasted_iota(jnp.int32, (tm, ROUTE_LANES), 1).astype(F32)
    neg = -jnp.inf
    big = float(ROUTE_LANES)

    gmask = (lane >= GROUP_LANE0) & (lane < GROUP_LANE0 + N_GROUPS)
    gl = jnp.where(gmask, logits, neg)
    gmax = jnp.max(gl, axis=-1, keepdims=True)
    gsum = jnp.sum(jnp.where(gmask, jnp.exp(gl - gmax), 0.0), axis=-1, keepdims=True)
    g_val = 1.0 / gsum
    g_idx = jnp.min(jnp.where(gl == gmax, lane, big), axis=-1, keepdims=True) - GROUP_LANE0

    lo = EXPERT_LANE0 + g_idx * EXPERTS_PER_GROUP
    emask = (lane >= lo) & (lane < lo + EXPERTS_PER_GROUP)
    el = jnp.where(emask, logits, neg)
    t1 = jnp.max(el, axis=-1, keepdims=True)
    i1 = jnp.min(jnp.where(emask & (el == t1), lane, big), axis=-1, keepdims=True)
    emask2 = emask & (lane != i1)
    el2 = jnp.where(emask2, logits, neg)
    t2 = jnp.max(el2, axis=-1, keepdims=True)
    i2 = jnp.min(jnp.where(emask2 & (el2 == t2), lane, big), axis=-1, keepdims=True)
    dexp = jnp.exp(t2 - t1)
    w0 = g_val / (1.0 + dexp)
    w1 = g_val * dexp / (1.0 + dexp)

    sel1 = lane == i1
    sel2 = lane == i2
    onehot = jnp.where(sel1 | sel2, 1.0, 0.0)
    rr = lax.broadcasted_iota(jnp.int32, (tm, tm), 0)
    cc = lax.broadcasted_iota(jnp.int32, (tm, tm), 1)
    tri = jnp.where(cc < rr, 1.0, 0.0).astype(BF16)
    prefix = jnp.dot(tri, onehot.astype(BF16), preferred_element_type=F32) + carry_ref[...]
    rank0 = jnp.sum(jnp.where(sel1, prefix, 0.0), axis=-1, keepdims=True)
    rank1 = jnp.sum(jnp.where(sel2, prefix, 0.0), axis=-1, keepdims=True)
    total = carry_ref[...] + jnp.sum(onehot, axis=0, keepdims=True)
    carry_ref[...] = total
    counts_ref[...] = total

    e0 = i1 - EXPERT_LANE0
    e1 = i2 - EXPERT_LANE0
    route = jnp.where(lane == 0, e0, 0.0)
    route = jnp.where(lane == 1, e1, route)
    route = jnp.where(lane == 2, w0, route)
    route = jnp.where(lane == 3, w1, route)
    route = jnp.where(lane == 4, rank0, route)
    route = jnp.where(lane == 5, rank1, route)
    route_ref[...] = route


def router(x, g, wr, br, tm=ROUTE_ROWS):
    t, d = x.shape
    tm = min(tm, t)
    return pl.pallas_call(
        _router_kernel,
        grid=(t // tm,),
        in_specs=[pl.BlockSpec((tm, d), lambda i: (i, 0)),
                  pl.BlockSpec((1, d), lambda i: (0, 0)),
                  pl.BlockSpec((d, ROUTE_LANES), lambda i: (0, 0)),
                  pl.BlockSpec((1, ROUTE_LANES), lambda i: (0, 0))],
        out_specs=[pl.BlockSpec((tm, d), lambda i: (i, 0)),
                   pl.BlockSpec((tm, ROUTE_LANES), lambda i: (i, 0)),
                   pl.BlockSpec((1, ROUTE_LANES), lambda i: (0, 0))],
        out_shape=[jax.ShapeDtypeStruct((t, d), F32),
                   jax.ShapeDtypeStruct((t, ROUTE_LANES), F32),
                   jax.ShapeDtypeStruct((1, ROUTE_LANES), F32)],
        scratch_shapes=[pltpu.VMEM((1, ROUTE_LANES), F32),
                        pltpu.VMEM((d, 2 * ROUTE_LANES), BF16)],
        compiler_params=_params(("arbitrary",)),
        name="router",
    )(x, g.reshape(1, d), wr, br)


def _gather_kernel(e0_ref, e1_ref, r0_ref, r1_ref, ps_ref, nu_ref, x_hbm, xs_ref,
                   rowtok_ref, buf_ref, sem, *, blk, n_tok):
    i = pl.program_id(0)
    nu = nu_ref[0]

    def row_copy(b, slot, r):
        tok = rowtok_ref[b * blk + r]
        return pltpu.make_async_copy(x_hbm.at[pl.ds(tok, 1), :],
                                     buf_ref.at[slot, pl.ds(r, 1), :], sem.at[slot])

    def for_rows(fn):
        def body(grp, carry):
            r0 = pl.multiple_of(grp * SUBLANES, SUBLANES)
            for k in range(SUBLANES):
                fn(r0 + k)
            return carry
        lax.fori_loop(0, blk // SUBLANES, body, 0)

    def start_block(b, slot):
        for_rows(lambda r: row_copy(b, slot, r).start())

    def wait_block(b, slot):
        for_rows(lambda r: row_copy(b, slot, r).wait())

    @pl.when(i == 0)
    def _():
        n_rows = rowtok_ref.shape[0]
        for base in range(0, n_rows, n_tok):
            def init(r, carry, base=base):
                rowtok_ref[base + r] = r
                return carry
            lax.fori_loop(0, min(n_tok, n_rows - base), init, 0, unroll=SCALAR_UNROLL)

        def fill(t, carry):
            rowtok_ref[ps_ref[e0_ref[t]] + r0_ref[t]] = t
            rowtok_ref[ps_ref[e1_ref[t]] + r1_ref[t]] = t
            return carry
        lax.fori_loop(0, n_tok, fill, 0, unroll=SCALAR_UNROLL)
        start_block(0, 0)

    @pl.when(i + 1 < nu)
    def _():
        start_block(i + 1, (i + 1) % 2)

    @pl.when(i < nu)
    def _():
        slot = i % 2
        wait_block(i, slot)
        xs_ref[...] = buf_ref[slot].astype(xs_ref.dtype)

    @pl.when(i >= nu)
    def _():
        xs_ref[...] = jnp.zeros_like(xs_ref)


def moe_gather(hn, slots, n_used, n_blocks, blk):
    t, d = hn.shape
    grid_spec = pltpu.PrefetchScalarGridSpec(
        num_scalar_prefetch=6,
        grid=(n_blocks,),
        in_specs=[pl.BlockSpec(memory_space=pl.ANY)],
        out_specs=pl.BlockSpec((blk, d), lambda i, *_: (i, 0)),
        scratch_shapes=[pltpu.SMEM((n_blocks * blk,), jnp.int32),
                        pltpu.VMEM((2, blk, d), F32),
                        pltpu.SemaphoreType.DMA((2,))],
    )
    return pl.pallas_call(
        functools.partial(_gather_kernel, blk=blk, n_tok=t),
        grid_spec=grid_spec,
        out_shape=jax.ShapeDtypeStruct((n_blocks * blk, d), BF16),
        compiler_params=_params(("arbitrary",)),
        name="moe_gather",
    )(*slots, n_used, hn)


def _expert_changed(be_ref, i):
    prev = be_ref[jnp.maximum(i - 1, 0)]
    return (i == 0) | (be_ref[i] != prev)


def _stream_expert_weights(w_hbms, col0, be_ref, nx_ref, nu, i, wst_ref, wbf_ref, sem, slot_ref):
    tn = wst_ref.shape[-1]

    def copies(e, slot):
        return [pltpu.make_async_copy(w.at[e, :, pl.ds(col0, tn)], wst_ref.at[slot, l], sem.at[slot])
                for l, w in enumerate(w_hbms)]

    @pl.when(i == 0)
    def _():
        slot_ref[0] = 0
        for c in copies(be_ref[0], 0):
            c.start()

    @pl.when(_expert_changed(be_ref, i))
    def _():
        slot = slot_ref[0]
        for c in copies(be_ref[i], slot):
            c.wait()
        nxt = nx_ref[i]

        @pl.when(nxt < nu)
        def _():
            for c in copies(be_ref[jnp.minimum(nxt, be_ref.shape[0] - 1)], 1 - slot):
                c.start()

        for l in range(len(w_hbms)):
            _cast_rows(wst_ref.at[slot, l], wbf_ref.at[l])
        slot_ref[0] = 1 - slot


def _moe_up_kernel(be_ref, nx_ref, nu_ref, xs_ref, wg_hbm, wu_hbm, act_ref,
                   wst_ref, wbf_ref, sem, slot_ref):
    j = pl.program_id(0)
    i = pl.program_id(1)
    nu = nu_ref[0]
    tf = act_ref.shape[1]

    @pl.when(i < nu)
    def _():
        _stream_expert_weights([wg_hbm, wu_hbm], pl.multiple_of(j * tf, tf), be_ref, nx_ref, nu, i,
                               wst_ref, wbf_ref, sem, slot_ref)
        x = xs_ref[...]
        gate = jnp.dot(x, wbf_ref[0], preferred_element_type=F32)
        up = jnp.dot(x, wbf_ref[1], preferred_element_type=F32)
        act_ref[...] = (gate * _sigmoid(gate) * up).astype(act_ref.dtype)

    @pl.when(i >= nu)
    def _():
        act_ref[...] = jnp.zeros_like(act_ref)


def _moe_down_kernel(be_ref, nx_ref, nu_ref, act_ref, wd_hbm, y_ref, wst_ref, wbf_ref, sem, slot_ref):
    j = pl.program_id(0)
    i = pl.program_id(1)
    nu = nu_ref[0]
    tn = y_ref.shape[1]

    @pl.when(i < nu)
    def _():
        _stream_expert_weights([wd_hbm], pl.multiple_of(j * tn, tn), be_ref, nx_ref, nu, i,
                               wst_ref, wbf_ref, sem, slot_ref)
        y_ref[...] = jnp.dot(act_ref[...], wbf_ref[0], preferred_element_type=F32)

    @pl.when(i >= nu)
    def _():
        y_ref[...] = jnp.zeros_like(y_ref)


def moe_experts(xs, block_expert, next_expert_block, n_used, w_gate, w_up, w_down,
                blk=MOE_BLK, tf=MOE_TF, tn=MOE_TN):
    r, dw = xs.shape
    _, d, f = w_gate.shape
    n_blocks = r // blk
    tf = min(tf, f)
    tn = min(tn, d)

    def used(i, nu):
        return jnp.minimum(i, jnp.maximum(nu[0] - 1, 0))

    def stream_scratch(n_mats, k, n):
        return [pltpu.VMEM((2, n_mats, k, n), F32),
                pltpu.VMEM((n_mats, k, n), BF16),
                pltpu.SemaphoreType.DMA((2,)),
                pltpu.SMEM((1,), jnp.int32)]

    up_spec = pltpu.PrefetchScalarGridSpec(
        num_scalar_prefetch=3,
        grid=(f // tf, n_blocks),
        in_specs=[pl.BlockSpec((blk, dw), lambda j, i, be, nx, nu: (used(i, nu), 0)),
                  pl.BlockSpec(memory_space=pl.ANY),
                  pl.BlockSpec(memory_space=pl.ANY)],
        out_specs=pl.BlockSpec((blk, tf), lambda j, i, be, nx, nu: (i, j)),
        scratch_shapes=stream_scratch(2, d, tf),
    )
    act = pl.pallas_call(
        _moe_up_kernel,
        grid_spec=up_spec,
        out_shape=jax.ShapeDtypeStruct((r, f), BF16),
        compiler_params=_params(("arbitrary", "arbitrary")),
        name="moe_up",
    )(block_expert, next_expert_block, n_used, xs, w_gate, w_up)
    down_spec = pltpu.PrefetchScalarGridSpec(
        num_scalar_prefetch=3,
        grid=(d // tn, n_blocks),
        in_specs=[pl.BlockSpec((blk, f), lambda j, i, be, nx, nu: (used(i, nu), 0)),
                  pl.BlockSpec(memory_space=pl.ANY)],
        out_specs=pl.BlockSpec((blk, tn), lambda j, i, be, nx, nu: (i, j)),
        scratch_shapes=stream_scratch(1, f, tn),
    )
    return pl.pallas_call(
        _moe_down_kernel,
        grid_spec=down_spec,
        out_shape=jax.ShapeDtypeStruct((r, d), F32),
        compiler_params=_params(("arbitrary", "arbitrary")),
        name="moe_down",
    )(block_expert, next_expert_block, n_used, act, w_down)


def _combine_kernel(e0_ref, e1_ref, r0_ref, r1_ref, ps_ref, x_ref, route_ref, g_ref, y_hbm, o_ref,
                    ya_ref, yb_ref, sem, *, tb, final_norm):
    step = pl.program_id(0)

    def copies(b, slot, i):
        t = b * tb + i
        row0 = ps_ref[e0_ref[t]] + r0_ref[t]
        row1 = ps_ref[e1_ref[t]] + r1_ref[t]
        return (pltpu.make_async_copy(y_hbm.at[pl.ds(row0, 1), :],
                                      ya_ref.at[slot, pl.ds(i, 1), :], sem.at[slot]),
                pltpu.make_async_copy(y_hbm.at[pl.ds(row1, 1), :],
                                      yb_ref.at[slot, pl.ds(i, 1), :], sem.at[slot]))

    def for_rows(fn):
        def body(grp, carry):
            r0 = pl.multiple_of(grp * SUBLANES, SUBLANES)
            for k in range(SUBLANES):
                fn(r0 + k)
            return carry
        lax.fori_loop(0, tb // SUBLANES, body, 0)

    def start_block(b, slot):
        def start(i):
            c0, c1 = copies(b, slot, i)
            c0.start()
            c1.start()
        for_rows(start)

    def wait_block(b, slot):
        def wait(i):
            c0, c1 = copies(b, slot, i)
            c0.wait()
            c1.wait()
        for_rows(wait)

    @pl.when(step == 0)
    def _():
        start_block(0, 0)

    @pl.when(step + 1 < pl.num_programs(0))
    def _():
        start_block(step + 1, (step + 1) % 2)

    slot = step % 2
    wait_block(step, slot)
    w0 = route_ref[:, 2:3]
    w1 = route_ref[:, 3:4]
    x = x_ref[...] + (ya_ref[slot] * w0 + yb_ref[slot] * w1)
    if final_norm:
        ms = jnp.mean(x * x, axis=-1, keepdims=True)
        x = x * lax.rsqrt(ms + NORM_EPS) * g_ref[...]
    o_ref[...] = x


def combine(x, route, y, slots, g, final_norm, tb=COMB_ROWS):
    t, d = x.shape
    tb = min(tb, t)
    grid_spec = pltpu.PrefetchScalarGridSpec(
        num_scalar_prefetch=5,
        grid=(t // tb,),
        in_specs=[pl.BlockSpec((tb, d), lambda i, *_: (i, 0)),
                  pl.BlockSpec((tb, ROUTE_LANES), lambda i, *_: (i, 0)),
                  pl.BlockSpec((1, d), lambda i, *_: (0, 0)),
                  pl.BlockSpec(memory_space=pl.ANY)],
        out_specs=pl.BlockSpec((tb, d), lambda i, *_: (i, 0)),
        scratch_shapes=[pltpu.VMEM((2, tb, d), F32), pltpu.VMEM((2, tb, d), F32),
                        pltpu.SemaphoreType.DMA((2,))],
    )
    return pl.pallas_call(
        functools.partial(_combine_kernel, tb=tb, final_norm=final_norm),
        grid_spec=grid_spec,
        out_shape=jax.ShapeDtypeStruct((t, d), F32),
        compiler_params=_params(("arbitrary",)),
        name="moe_combine",
    )(*slots, x, route, g.reshape(1, d), y)


def _route_lanes(group_part, expert_part):
    rows = group_part.shape[0]
    gap = jnp.zeros((rows, EXPERT_LANE0 - GROUP_LANE0 - N_GROUPS), F32)
    tail = jnp.zeros((rows, ROUTE_LANES - EXPERT_LANE0 - N_EXPERTS), F32)
    return jnp.concatenate([group_part, gap, expert_part, tail], axis=1)


def _moe_layout(route, counts, blk):
    t = route.shape[0]
    ri = route[:, :8].astype(jnp.int32)
    e0, e1, rank0, rank1 = ri[:, 0], ri[:, 1], ri[:, 4], ri[:, 5]
    cnt = counts[0, EXPERT_LANE0:EXPERT_LANE0 + N_EXPERTS].astype(jnp.int32)
    padded = (cnt + blk - 1) // blk * blk
    pends = jnp.cumsum(padded)
    pstarts = pends - padded
    n_blocks = (2 * t) // blk + N_EXPERTS
    block_start = jnp.arange(n_blocks, dtype=jnp.int32) * blk
    block_expert = jnp.minimum(
        jnp.sum((block_start[:, None] >= pends[None, :]).astype(jnp.int32), axis=1), N_EXPERTS - 1)
    n_used = (pends[-1] // blk).astype(jnp.int32).reshape(1)
    block_expert = block_expert[jnp.minimum(jnp.arange(n_blocks), jnp.maximum(n_used[0] - 1, 0))]
    next_expert_block = pends[block_expert] // blk
    return (e0, e1, rank0, rank1, pstarts), block_expert, next_expert_block, n_used, n_blocks


def kernel(x, mem, positions, mix_norm_g, w_in, ret_norm_g, lru_conv_w, lru_conv_b, lru_w_a, lru_b_a, lru_w_i, lru_b_i, lru_lambda, lru_norm_g, w_out, xattn_norm_g, mem_norm_g, xattn_wq, xattn_wk, xattn_wv, xattn_wo, moe_norm_g, router_group_w, router_group_b, router_expert_w, router_expert_b, expert_w_gate, expert_w_up, expert_w_down, final_norm_g):
    b, s, d = x.shape
    depth = w_in.shape[0]
    ret_width = RET_HEADS * RET_HEAD_DIM
    lru_width = lru_conv_w.shape[-1]
    assert ret_width == lru_width and ret_width + lru_width == d
    inv_freq = ROPE_BASE ** (-jnp.arange(0, RET_HEAD_DIM, 2, dtype=F32) / RET_HEAD_DIM)
    lg = jnp.log1p(-jnp.exp2(-5.0 - jnp.arange(RET_HEADS, dtype=F32)))
    lg_rows = jnp.broadcast_to(lg[:, None, None], (RET_HEADS, 1, RET_HEAD_DIM))
    blk = min(MOE_BLK, s)
    outs = []
    for bi in range(b):
        xcur = x[bi]
        cos, sin = rope_tables(positions[bi].astype(F32), inv_freq)
        for l in range(depth):
            h = normcast(xcur, mix_norm_g[l], BF16, NORM_ROWS)
            proj = matmul_streamed([h], w_in[l], F32)
            ret = retention(proj, cos, sin, lg_rows, ret_norm_g[l])
            lru = rg_lru(proj, 4 * ret_width // lru_width, 4 * ret_width // lru_width + 1,
                         lru_conv_w[l], lru_conv_b[l], lru_w_a[l], lru_b_a[l], lru_w_i[l],
                         lru_b_i[l], lru_lambda[l], lru_norm_g[l])
            xcur = matmul_streamed([ret, lru], w_out[l], F32, res=xcur, tm=MM_STREAM_TM // 2)
            memn = normcast(mem[bi], mem_norm_g[l], BF16, NORM_ROWS)
            kk = matmul([memn], xattn_wk[l], BF16)
            vv = matmul([memn], xattn_wv[l], BF16)
            wqk, vo = xattn_fold(kk, vv, xattn_wq[l], xattn_wo[l])
            xcur = xattn(xcur, xattn_norm_g[l], wqk, vo)
            wr = _route_lanes(router_group_w[l], router_expert_w[l])
            br = _route_lanes(router_group_b[l][None], router_expert_b[l][None])
            hn, route, counts = router(xcur, moe_norm_g[l], wr, br)
            slots, block_expert, next_block, n_used, n_blocks = _moe_layout(route, counts, blk)
            xs = moe_gather(hn, slots, n_used, n_blocks, blk)
            y = moe_experts(xs, block_expert, next_block, n_used, expert_w_gate[l], expert_w_up[l],
                            expert_w_down[l], blk=blk)
            xcur = combine(xcur, route, y, slots, final_norm_g, final_norm=l == depth - 1)
        outs.append(xcur)
    return outs[0][None] if b == 1 else jnp.stack(outs, axis=0)
```

```python
import functools

import jax
import jax.numpy as jnp
from jax import lax
from jax.experimental import pallas as pl
from jax.experimental.pallas import tpu as pltpu

F32 = jnp.float32
BF16 = jnp.bfloat16

RET_HEADS = 8
RET_HEAD_DIM = 256
RET_CHUNK = 128
LRU_BLOCKS = 8
CONV_WIDTH = 4
RG_C = 8.0
ROPE_BASE = 10000.0
X_HEADS = 4
N_GROUPS = 4
EXPERTS_PER_GROUP = 8
N_EXPERTS = N_GROUPS * EXPERTS_PER_GROUP
NORM_EPS = 1e-6
GN_EPS = 1e-5

LANES = 128
SUBLANES = 8
VMEM_LIMIT = 56 * 1024 * 1024

NORM_ROWS = 512
MM_TM = 1024
MM_TN = 512
MM_STREAM_TM = 1024
MM_STREAM_TN = 1024
RET_ROWS = 512
RET_HEADS_PER_STEP = 8
LRU_ROWS = 256
ATT_ROWS = 512
ROUTE_ROWS = 256
MOE_BLK = 256
MOE_UP_CHUNKS = 8
MOE_UP_SLOTS = 3
MOE_TN = 4096
XF_TILE = 1024
SCALAR_UNROLL = 8
COMB_ROWS = 256
ROUTE_LANES = LANES
GROUP_LANE0 = 0
EXPERT_LANE0 = 8


def _params(sem):
    return pltpu.CompilerParams(dimension_semantics=sem, vmem_limit_bytes=VMEM_LIMIT)


def _normcast_kernel(x_ref, g_ref, o_ref):
    x = x_ref[...]
    ms = jnp.mean(x * x, axis=-1, keepdims=True)
    o_ref[...] = (x * lax.rsqrt(ms + NORM_EPS) * g_ref[...]).astype(o_ref.dtype)


def normcast(x, g, out_dtype, tm):
    m, d = x.shape
    tm = min(tm, m)
    return pl.pallas_call(
        _normcast_kernel,
        grid=(m // tm,),
        in_specs=[pl.BlockSpec((tm, d), lambda i: (i, 0)),
                  pl.BlockSpec((1, d), lambda i: (0, 0))],
        out_specs=pl.BlockSpec((tm, d), lambda i: (i, 0)),
        out_shape=jax.ShapeDtypeStruct((m, d), out_dtype),
        compiler_params=_params(("arbitrary",)),
        name="normcast",
    )(x, g.reshape(1, d))


def _cast_rows(src_ref, dst_ref, rows_per_iter=256):
    k = src_ref.shape[0]
    step = min(rows_per_iter, k)

    def body(i, carry):
        r0 = pl.multiple_of(i * step, step)
        dst_ref[pl.ds(r0, step), :] = src_ref[pl.ds(r0, step), :].astype(dst_ref.dtype)
        return carry

    lax.fori_loop(0, k // step, body, 0)


def _mm_kernel(*refs, n_a, has_res):
    a_refs = refs[:n_a]
    w_ref = refs[n_a]
    res_ref = refs[n_a + 1] if has_res else None
    o_ref = refs[n_a + 1 + int(has_res)]
    wbf_ref = refs[n_a + 2 + int(has_res)]

    @pl.when(pl.program_id(1) == 0)
    def _():
        _cast_rows(w_ref, wbf_ref)

    kp = a_refs[0].shape[1]
    acc = None
    for p, a_ref in enumerate(a_refs):
        d = jnp.dot(a_ref[...], wbf_ref[p * kp:(p + 1) * kp, :], preferred_element_type=F32)
        acc = d if acc is None else acc + d
    if has_res:
        acc = acc + res_ref[...]
    o_ref[...] = acc.astype(o_ref.dtype)


def matmul(a_parts, w, out_dtype, res=None, tm=MM_TM, tn=MM_TN):
    m, kp = a_parts[0].shape
    k, n = w.shape
    assert kp * len(a_parts) == k
    tm = min(tm, m)
    tn = min(tn, n)
    in_specs = [pl.BlockSpec((tm, kp), lambda j, i: (i, 0)) for _ in a_parts]
    in_specs.append(pl.BlockSpec((k, tn), lambda j, i: (0, j)))
    args = list(a_parts) + [w]
    if res is not None:
        in_specs.append(pl.BlockSpec((tm, tn), lambda j, i: (i, j)))
        args.append(res)
    return pl.pallas_call(
        functools.partial(_mm_kernel, n_a=len(a_parts), has_res=res is not None),
        grid=(n // tn, m // tm),
        in_specs=in_specs,
        out_specs=pl.BlockSpec((tm, tn), lambda j, i: (i, j)),
        out_shape=jax.ShapeDtypeStruct((m, n), out_dtype),
        scratch_shapes=[pltpu.VMEM((k, tn), BF16)],
        compiler_params=_params(("arbitrary", "arbitrary")),
        name="matmul",
    )(*args)


def _mm_stream_kernel(*refs, n_a, has_res):
    a_refs = refs[:n_a]
    w_hbm = refs[n_a]
    res_ref = refs[n_a + 1] if has_res else None
    o_ref = refs[n_a + 1 + int(has_res)]
    wbf_ref, stage_ref, sem = refs[n_a + 2 + int(has_res):]
    j = pl.program_id(0)
    i = pl.program_id(1)
    n_j = pl.num_programs(0)
    n_i = pl.num_programs(1)
    kc, tn = stage_ref.shape[1:]
    n_chunks = wbf_ref.shape[1] // kc

    def chunk_copy(col_tile, c, slot):
        rows = pl.ds(pl.multiple_of(c * kc, kc), kc)
        cols = pl.ds(pl.multiple_of(col_tile * tn, tn), tn)
        return pltpu.make_async_copy(w_hbm.at[rows, cols], stage_ref.at[slot], sem.at[slot])

    def cast_chunk(buf, c, slot):
        rows = pl.ds(pl.multiple_of(c * kc, kc), kc)
        wbf_ref[buf, rows, :] = stage_ref[slot].astype(BF16)

    @pl.when((j == 0) & (i == 0))
    def _():
        chunk_copy(0, 0, 0).start()
        for c in range(n_chunks):
            if c + 1 < n_chunks:
                chunk_copy(0, c + 1, (c + 1) % 2).start()
            chunk_copy(0, c, c % 2).wait()
            cast_chunk(0, c, c % 2)

    cur = j % 2
    kp = a_refs[0].shape[1]
    acc = None
    for p, a_ref in enumerate(a_refs):
        d = jnp.dot(a_ref[...], wbf_ref[cur, p * kp:(p + 1) * kp, :], preferred_element_type=F32)
        acc = d if acc is None else acc + d
    if has_res:
        acc = acc + res_ref[...]
    o_ref[...] = acc.astype(o_ref.dtype)

    @pl.when(j + 1 < n_j)
    def _():
        slot = i % 2

        @pl.when(i == 0)
        def _():
            chunk_copy(j + 1, 0, 0).start()

        chunk_copy(j + 1, i, slot).wait()

        @pl.when(i + 1 < n_i)
        def _():
            chunk_copy(j + 1, i + 1, 1 - slot).start()

        cast_chunk(1 - cur, i, slot)


def matmul_streamed(a_parts, w, out_dtype, res=None, tm=MM_STREAM_TM, tn=MM_STREAM_TN):
    m, kp = a_parts[0].shape
    k, n = w.shape
    assert kp * len(a_parts) == k
    tm = min(tm, m)
    tn = min(tn, n)
    n_i = m // tm
    kc = k // n_i
    assert kc * n_i == k and kc % SUBLANES == 0
    in_specs = [pl.BlockSpec((tm, kp), lambda j, i: (i, 0)) for _ in a_parts]
    in_specs.append(pl.BlockSpec(memory_space=pl.ANY))
    args = list(a_parts) + [w]
    if res is not None:
        in_specs.append(pl.BlockSpec((tm, tn), lambda j, i: (i, j)))
        args.append(res)
    return pl.pallas_call(
        functools.partial(_mm_stream_kernel, n_a=len(a_parts), has_res=res is not None),
        grid=(n // tn, n_i),
        in_specs=in_specs,
        out_specs=pl.BlockSpec((tm, tn), lambda j, i: (i, j)),
        out_shape=jax.ShapeDtypeStruct((m, n), out_dtype),
        scratch_shapes=[pltpu.VMEM((2, k, tn), BF16),
                        pltpu.VMEM((2, kc, tn), F32),
                        pltpu.SemaphoreType.DMA((2,))],
        compiler_params=_params(("arbitrary", "arbitrary")),
        name="matmul_streamed",
    )(*args)


def _rope_kernel(pos_ref, invf_ref, cos_ref, sin_ref):
    ang = pos_ref[...] * invf_ref[...]
    cos_ref[...] = jnp.cos(ang)
    sin_ref[...] = jnp.sin(ang)


def rope_tables(pos_f, inv_freq, tm=512):
    s = pos_f.shape[0]
    hd = inv_freq.shape[0]
    tm = min(tm, s)
    return pl.pallas_call(
        _rope_kernel,
        grid=(s // tm,),
        in_specs=[pl.BlockSpec((tm, 1), lambda i: (i, 0)),
                  pl.BlockSpec((1, hd), lambda i: (0, 0))],
        out_specs=[pl.BlockSpec((tm, hd), lambda i: (i, 0))] * 2,
        out_shape=[jax.ShapeDtypeStruct((s, hd), F32)] * 2,
        compiler_params=_params(("arbitrary",)),
        name="rope_tables",
    )(pos_f.reshape(s, 1), inv_freq.reshape(1, hd))


def _ret_kernel(q_ref, k_ref, v_ref, g_ref, cos_ref, sin_ref, lg_ref, gn_ref, o_ref, r_ref,
                decay_ref, xi_ref, zeta_ref, *, n_chunks, hpb):
    c = RET_CHUNK
    dk = RET_HEAD_DIM
    half = dk // 2
    scale = dk ** -0.5

    @pl.when(pl.program_id(1) == 0)
    def _():
        r_ref[...] = jnp.zeros_like(r_ref)
        row = lax.broadcasted_iota(jnp.int32, (c, c), 0).astype(F32)
        col = lax.broadcasted_iota(jnp.int32, (c, c), 1).astype(F32)
        diff = row - col
        rowk = lax.broadcasted_iota(jnp.int32, (c, dk), 0).astype(F32)
        for hh in range(hpb):
            lg = lg_ref[hh]
            decay_ref[hh] = jnp.where(diff >= 0, jnp.exp(lg[:, :c] * jnp.maximum(diff, 0.0)), 0.0)
            xi_ref[hh] = jnp.exp(lg * (rowk + 1.0))
            zeta_ref[hh] = jnp.exp(lg * (c - 1.0 - rowk))

    def rope(t, cos, sin):
        t1 = t[:, :half]
        t2 = t[:, half:]
        return jnp.concatenate([t1 * cos - t2 * sin, t1 * sin + t2 * cos], axis=-1)

    def body(j, carry):
        r0 = pl.multiple_of(j * c, c)
        rows = pl.ds(r0, c)
        cos = cos_ref[rows, :]
        sin = sin_ref[rows, :]
        for hh in range(hpb):
            cs = slice(hh * dk, (hh + 1) * dk)
            qr = rope(q_ref[rows, cs], cos, sin)
            kr = rope(k_ref[rows, cs], cos, sin) * scale
            qb = qr.astype(BF16)
            kb = kr.astype(BF16)
            vb = v_ref[rows, cs].astype(BF16)
            state = r_ref[hh]
            inner = lax.dot_general(qb, kb, (((1,), (1,)), ((), ())),
                                    preferred_element_type=F32) * decay_ref[hh]
            o = (jnp.dot(inner.astype(BF16), vb, preferred_element_type=F32)
                 + jnp.dot(qb, state.astype(BF16), preferred_element_type=F32) * xi_ref[hh])
            kz = (kr * zeta_ref[hh]).astype(BF16)
            chunk_decay = jnp.exp(lg_ref[hh] * c)
            r_ref[hh] = state * chunk_decay + lax.dot_general(
                kz, vb, (((0,), (0,)), ((), ())), preferred_element_type=F32)
            mu = jnp.mean(o, axis=-1, keepdims=True)
            oc = o - mu
            var = jnp.mean(oc * oc, axis=-1, keepdims=True)
            on = oc * lax.rsqrt(var + GN_EPS) * gn_ref[hh]
            g = g_ref[rows, cs]
            o_ref[rows, cs] = (on * (g * (1.0 / (1.0 + jnp.exp(-g))))).astype(o_ref.dtype)
        return carry

    lax.fori_loop(0, n_chunks, body, 0, unroll=2)


def retention(proj, cos, sin, lg_rows, gn_g, tr=RET_ROWS, hpb=RET_HEADS_PER_STEP):
    s = proj.shape[0]
    dk = RET_HEAD_DIM
    h = RET_HEADS
    tr = min(tr, s)
    c = RET_CHUNK
    w = hpb * dk

    def col(base):
        return pl.BlockSpec((tr, w), lambda hg, ci, base=base: (ci, base // hpb + hg))

    per_head = pl.BlockSpec((hpb, 1, dk), lambda hg, ci: (hg, 0, 0))
    return pl.pallas_call(
        functools.partial(_ret_kernel, n_chunks=tr // c, hpb=hpb),
        grid=(h // hpb, s // tr),
        in_specs=[col(0), col(h), col(2 * h), col(3 * h),
                  pl.BlockSpec((tr, dk // 2), lambda hg, ci: (ci, 0)),
                  pl.BlockSpec((tr, dk // 2), lambda hg, ci: (ci, 0)),
                  per_head, per_head],
        out_specs=pl.BlockSpec((tr, w), lambda hg, ci: (ci, hg)),
        out_shape=jax.ShapeDtypeStruct((s, h * dk), BF16),
        scratch_shapes=[pltpu.VMEM((hpb, dk, dk), F32),
                        pltpu.VMEM((hpb, c, c), F32),
                        pltpu.VMEM((hpb, c, dk), F32),
                        pltpu.VMEM((hpb, c, dk), F32)],
        compiler_params=_params(("arbitrary", "arbitrary")),
        name="retention",
    )(proj, proj, proj, proj, cos, sin, lg_rows, gn_g.reshape(h, 1, dk))


def _sigmoid(x):
    return 1.0 / (1.0 + jnp.exp(-x))


def _lru_kernel(xb_ref, gb_ref, cw_ref, cb_ref, wa_ref, ba_ref, wi_ref, bi_ref, lam_ref, og_ref,
                o_ref, tail_ref, xs_ref, hs_ref, h_ref, wabf_ref, wibf_ref):
    tr, cdim = xb_ref.shape
    nb = wa_ref.shape[0]
    bd = cdim // nb
    ph = SUBLANES
    ng = tr // ph

    @pl.when(pl.program_id(0) == 0)
    def _():
        tail_ref[...] = jnp.zeros_like(tail_ref)
        h_ref[...] = jnp.zeros_like(h_ref)
        wabf_ref[...] = wa_ref[...].astype(BF16)
        wibf_ref[...] = wi_ref[...].astype(BF16)

    lam = lam_ref[...]
    sp = jnp.maximum(-lam, 0.0) + jnp.log1p(jnp.exp(-jnp.abs(lam)))
    rowg = lax.broadcasted_iota(jnp.int32, (ng, bd), 0)
    lpb = bd // LANES
    for c in range(cdim // LANES):
        xs_ref[c] = xb_ref[:, c * LANES:(c + 1) * LANES]

    def phase_rows(ref, n, p):
        return jnp.concatenate([ref[n * lpb + c, pl.ds(p, ng, stride=ph), :] for c in range(lpb)],
                               axis=1)

    for n in range(nb):
        cs = slice(n * bd, (n + 1) * bd)
        x = [phase_rows(xs_ref, n, p) for p in range(ph)]

        def prev_group(p):
            return jnp.where(rowg == 0, tail_ref[p:p + 1, cs], pltpu.roll(x[p], 1, 0))

        back = {-k: prev_group(ph - k) for k in range(1, CONV_WIDTH)}

        def xat(p):
            return x[p] if p >= 0 else back[p]

        xc = []
        for p in range(ph):
            acc = cb_ref[:, cs] + cw_ref[CONV_WIDTH - 1:CONV_WIDTH, cs] * xat(p)
            for k in range(1, CONV_WIDTH):
                acc = acc + cw_ref[CONV_WIDTH - 1 - k:CONV_WIDTH - k, cs] * xat(p - k)
            xc.append(acc)
        xg = jnp.concatenate(xc, axis=0)
        xgb = xg.astype(BF16)
        r = _sigmoid(jnp.dot(xgb, wabf_ref[n], preferred_element_type=F32) + ba_ref[:, cs])
        ig = _sigmoid(jnp.dot(xgb, wibf_ref[n], preferred_element_type=F32) + bi_ref[:, cs])
        log_a = (-RG_C * r) * sp[:, cs]
        a = jnp.exp(log_a)
        b = jnp.sqrt(-jnp.tanh(log_a) * (a * a + 1.0)) * (ig * xg)

        cum_a = [a[0:ng]]
        cum_b = [b[0:ng]]
        for p in range(1, ph):
            ap = a[p * ng:(p + 1) * ng]
            cum_b.append(ap * cum_b[-1] + b[p * ng:(p + 1) * ng])
            cum_a.append(ap * cum_a[-1])
        sa, sb = cum_a[-1], cum_b[-1]
        d = 1
        while d < ng:
            keep = rowg >= d
            sa_sh = pltpu.roll(sa, d, 0)
            sb_sh = pltpu.roll(sb, d, 0)
            sb = jnp.where(keep, sa * sb_sh + sb, sb)
            sa = jnp.where(keep, sa * sa_sh, sa)
            d *= 2
        h_in = h_ref[:, cs]
        h_end = sa * h_in + sb
        h_prev = jnp.where(rowg == 0, h_in, pltpu.roll(h_end, 1, 0))
        for p in range(ph):
            hp = cum_a[p] * h_prev + cum_b[p]
            for c in range(lpb):
                hs_ref[n * lpb + c, pl.ds(p, ng, stride=ph), :] = hp[:, c * LANES:(c + 1) * LANES]
        h_ref[:, cs] = h_end[ng - 1:ng, :]

    gb = gb_ref[...]
    gelu = 0.5 * gb * (1.0 + jnp.tanh(0.7978845608028654 * (gb + 0.044715 * (gb * gb * gb))))
    y = jnp.concatenate([hs_ref[c] for c in range(cdim // LANES)], axis=1) * gelu
    ms = jnp.mean(y * y, axis=-1, keepdims=True)
    o_ref[...] = (y * lax.rsqrt(ms + NORM_EPS) * og_ref[...]).astype(o_ref.dtype)
    tail_ref[...] = xb_ref[tr - ph:tr, :]


def rg_lru(proj, xb_block, gb_block, conv_w, conv_b, w_a, b_a, w_i, b_i, lam, out_g, tr=LRU_ROWS):
    s = proj.shape[0]
    cdim = conv_w.shape[1]
    nb, bd, _ = w_a.shape
    tr = min(tr, s)
    vec = pl.BlockSpec((1, cdim), lambda i: (0, 0))
    wspec = pl.BlockSpec((nb, bd, bd), lambda i: (0, 0, 0))
    return pl.pallas_call(
        _lru_kernel,
        grid=(s // tr,),
        in_specs=[pl.BlockSpec((tr, cdim), lambda i: (i, xb_block)),
                  pl.BlockSpec((tr, cdim), lambda i: (i, gb_block)),
                  pl.BlockSpec((CONV_WIDTH, cdim), lambda i: (0, 0)),
                  vec, wspec, vec, wspec, vec, vec, vec],
        out_specs=pl.BlockSpec((tr, cdim), lambda i: (i, 0)),
        out_shape=jax.ShapeDtypeStruct((s, cdim), BF16),
        scratch_shapes=[pltpu.VMEM((SUBLANES, cdim), F32),
                        pltpu.VMEM((cdim // LANES, tr, LANES), F32),
                        pltpu.VMEM((cdim // LANES, tr, LANES), F32),
                        pltpu.VMEM((1, cdim), F32),
                        pltpu.VMEM((nb, bd, bd), BF16),
                        pltpu.VMEM((nb, bd, bd), BF16)],
        compiler_params=_params(("arbitrary",)),
        name="rg_lru",
    )(proj, proj, conv_w, conv_b.reshape(1, cdim), w_a, b_a.reshape(1, cdim), w_i,
      b_i.reshape(1, cdim), lam.reshape(1, cdim), out_g.reshape(1, cdim))


def _wqk_kernel(wq_ref, k_ref, o_ref):
    o_ref[...] = lax.dot_general(wq_ref[...].astype(BF16), k_ref[...], (((1,), (1,)), ((), ())),
                                 preferred_element_type=F32).astype(o_ref.dtype)


def _vo_kernel(v_ref, wo_ref, o_ref):
    o_ref[...] = jnp.dot(v_ref[...], wo_ref[...].astype(BF16),
                         preferred_element_type=F32).astype(o_ref.dtype)


def xattn_fold(k, v, wq, wo, tile=XF_TILE):
    mlen, d = k.shape
    hd = d // X_HEADS
    tile = min(tile, d)
    wqk = pl.pallas_call(
        _wqk_kernel,
        grid=(X_HEADS, d // tile),
        in_specs=[pl.BlockSpec((tile, hd), lambda h, r: (r, h)),
                  pl.BlockSpec((mlen, hd), lambda h, r: (0, h))],
        out_specs=pl.BlockSpec((tile, mlen), lambda h, r: (r, h)),
        out_shape=jax.ShapeDtypeStruct((d, X_HEADS * mlen), BF16),
        compiler_params=_params(("arbitrary", "arbitrary")),
        name="xattn_wqk",
    )(wq, k)
    vo = pl.pallas_call(
        _vo_kernel,
        grid=(X_HEADS, d // tile),
        in_specs=[pl.BlockSpec((mlen, hd), lambda h, j: (0, h)),
                  pl.BlockSpec((hd, tile), lambda h, j: (h, j))],
        out_specs=pl.BlockSpec((mlen, tile), lambda h, j: (h, j)),
        out_shape=jax.ShapeDtypeStruct((X_HEADS * mlen, d), BF16),
        compiler_params=_params(("arbitrary", "arbitrary")),
        name="xattn_vo",
    )(v, wo)
    return wqk, vo


def _xattn_kernel(x_ref, g_ref, wqk_ref, vo_ref, o_ref):
    d = x_ref.shape[1]
    mlen = wqk_ref.shape[1] // X_HEADS
    scale = (d // X_HEADS) ** -0.5
    x = x_ref[...]
    ms = jnp.mean(x * x, axis=-1, keepdims=True)
    h = (x * lax.rsqrt(ms + NORM_EPS) * g_ref[...]).astype(BF16)
    s = jnp.dot(h, wqk_ref[...], preferred_element_type=F32) * scale
    ps = []
    for hh in range(X_HEADS):
        sh = s[:, hh * mlen:(hh + 1) * mlen]
        m = jnp.max(sh, axis=-1, keepdims=True)
        e = jnp.exp(sh - m)
        ps.append((e / jnp.sum(e, axis=-1, keepdims=True)).astype(BF16))
    p = jnp.concatenate(ps, axis=1)
    o_ref[...] = x + jnp.dot(p, vo_ref[...], preferred_element_type=F32)


def xattn(x, g, wqk, vo, tm=ATT_ROWS):
    s, d = x.shape
    tm = min(tm, s)
    return pl.pallas_call(
        _xattn_kernel,
        grid=(s // tm,),
        in_specs=[pl.BlockSpec((tm, d), lambda i: (i, 0)),
                  pl.BlockSpec((1, d), lambda i: (0, 0)),
                  pl.BlockSpec(wqk.shape, lambda i: (0, 0), pipeline_mode=pl.Buffered(1)),
                  pl.BlockSpec(vo.shape, lambda i: (0, 0), pipeline_mode=pl.Buffered(1))],
        out_specs=pl.BlockSpec((tm, d), lambda i: (i, 0)),
        out_shape=jax.ShapeDtypeStruct((s, d), F32),
        compiler_params=_params(("arbitrary",)),
        name="xattn",
    )(x, g.reshape(1, d), wqk, vo)


def _split_bf16(a):
    hi = a.astype(BF16)
    return hi, (a - hi.astype(F32)).astype(BF16)


def _router_kernel(x_ref, g_ref, wr_ref, br_ref, h_ref, route_ref, counts_ref, carry_ref,
                   wsplit_ref):
    tm = x_ref.shape[0]
    nl = ROUTE_LANES

    @pl.when(pl.program_id(0) == 0)
    def _():
        carry_ref[...] = jnp.zeros_like(carry_ref)
        w_hi, w_lo = _split_bf16(wr_ref[...])
        wsplit_ref[:, :nl] = w_hi
        wsplit_ref[:, nl:] = w_lo

    x = x_ref[...]
    ms = jnp.mean(x * x, axis=-1, keepdims=True)
    h = x * lax.rsqrt(ms + NORM_EPS) * g_ref[...]
    h_ref[...] = h
    h_hi, h_lo = _split_bf16(h)
    both = jnp.dot(h_hi, wsplit_ref[...], preferred_element_type=F32)
    cross = jnp.dot(h_lo, wsplit_ref[:, :nl], preferred_element_type=F32)
    logits = both[:, :nl] + (both[:, nl:] + cross) + br_ref[...]
    lane = lax.broadcasted_iota(jnp.int32, (tm, ROUTE_LANES), 1).astype(F32)
    neg = -jnp.inf
    big = float(ROUTE_LANES)

    gmask = (lane >= GROUP_LANE0) & (lane < GROUP_LANE0 + N_GROUPS)
    gl = jnp.where(gmask, logits, neg)
    gmax = jnp.max(gl, axis=-1, keepdims=True)
    gsum = jnp.sum(jnp.where(gmask, jnp.exp(gl - gmax), 0.0), axis=-1, keepdims=True)
    g_val = 1.0 / gsum
    g_idx = jnp.min(jnp.where(gl == gmax, lane, big), axis=-1, keepdims=True) - GROUP_LANE0

    lo = EXPERT_LANE0 + g_idx * EXPERTS_PER_GROUP
    emask = (lane >= lo) & (lane < lo + EXPERTS_PER_GROUP)
    el = jnp.where(emask, logits, neg)
    t1 = jnp.max(el, axis=-1, keepdims=True)
    i1 = jnp.min(jnp.where(emask & (el == t1), lane, big), axis=-1, keepdims=True)
    emask2 = emask & (lane != i1)
    el2 = jnp.where(emask2, logits, neg)
    t2 = jnp.max(el2, axis=-1, keepdims=True)
    i2 = jnp.min(jnp.where(emask2 & (el2 == t2), lane, big), axis=-1, keepdims=True)
    dexp = jnp.exp(t2 - t1)
    w0 = g_val / (1.0 + dexp)
    w1 = g_val * dexp / (1.0 + dexp)

    sel1 = lane == i1
    sel2 = lane == i2
    onehot = jnp.where(sel1 | sel2, 1.0, 0.0)
    rr = lax.broadcasted_iota(jnp.int32, (tm, tm), 0)
    cc = lax.broadcasted_iota(jnp.int32, (tm, tm), 1)
    tri = jnp.where(cc < rr, 1.0, 0.0).astype(BF16)
    prefix = jnp.dot(tri, onehot.astype(BF16), preferred_element_type=F32) + carry_ref[...]
    rank0 = jnp.sum(jnp.where(sel1, prefix, 0.0), axis=-1, keepdims=True)
    rank1 = jnp.sum(jnp.where(sel2, prefix, 0.0), axis=-1, keepdims=True)
    total = carry_ref[...] + jnp.sum(onehot, axis=0, keepdims=True)
    carry_ref[...] = total
    counts_ref[...] = total

    e0 = i1 - EXPERT_LANE0
    e1 = i2 - EXPERT_LANE0
    route = jnp.where(lane == 0, e0, 0.0)
    route = jnp.where(lane == 1, e1, route)
    route = jnp.where(lane == 2, w0, route)
    route = jnp.where(lane == 3, w1, route)
    route = jnp.where(lane == 4, rank0, route)
    route = jnp.where(lane == 5, rank1, route)
    route_ref[...] = route


def router(x, g, wr, br, tm=ROUTE_ROWS):
    t, d = x.shape
    tm = min(tm, t)
    return pl.pallas_call(
        _router_kernel,
        grid=(t // tm,),
        in_specs=[pl.BlockSpec((tm, d), lambda i: (i, 0)),
                  pl.BlockSpec((1, d), lambda i: (0, 0)),
                  pl.BlockSpec((d, ROUTE_LANES), lambda i: (0, 0)),
                  pl.BlockSpec((1, ROUTE_LANES), lambda i: (0, 0))],
        out_specs=[pl.BlockSpec((tm, d), lambda i: (i, 0)),
                   pl.BlockSpec((tm, ROUTE_LANES), lambda i: (i, 0)),
                   pl.BlockSpec((1, ROUTE_LANES), lambda i: (0, 0))],
        out_shape=[jax.ShapeDtypeStruct((t, d), F32),
                   jax.ShapeDtypeStruct((t, ROUTE_LANES), F32),
                   jax.ShapeDtypeStruct((1, ROUTE_LANES), F32)],
        scratch_shapes=[pltpu.VMEM((1, ROUTE_LANES), F32),
                        pltpu.VMEM((d, 2 * ROUTE_LANES), BF16)],
        compiler_params=_params(("arbitrary",)),
        name="router",
    )(x, g.reshape(1, d), wr, br)


def _gather_kernel(e0_ref, e1_ref, r0_ref, r1_ref, ps_ref, nu_ref, x_hbm, xs_ref,
                   rowtok_ref, buf_ref, sem, *, blk, n_tok):
    i = pl.program_id(0)
    nu = nu_ref[0]

    def row_copy(b, slot, r):
        tok = rowtok_ref[b * blk + r]
        return pltpu.make_async_copy(x_hbm.at[pl.ds(tok, 1), :],
                                     buf_ref.at[slot, pl.ds(r, 1), :], sem.at[slot])

    def for_rows(fn):
        def body(grp, carry):
            r0 = pl.multiple_of(grp * SUBLANES, SUBLANES)
            for k in range(SUBLANES):
                fn(r0 + k)
            return carry
        lax.fori_loop(0, blk // SUBLANES, body, 0)

    def start_block(b, slot):
        for_rows(lambda r: row_copy(b, slot, r).start())

    def wait_block(b, slot):
        for_rows(lambda r: row_copy(b, slot, r).wait())

    @pl.when(i == 0)
    def _():
        n_rows = rowtok_ref.shape[0]
        for base in range(0, n_rows, n_tok):
            def init(r, carry, base=base):
                rowtok_ref[base + r] = r
                return carry
            lax.fori_loop(0, min(n_tok, n_rows - base), init, 0, unroll=SCALAR_UNROLL)

        def fill(t, carry):
            rowtok_ref[ps_ref[e0_ref[t]] + r0_ref[t]] = t
            rowtok_ref[ps_ref[e1_ref[t]] + r1_ref[t]] = t
            return carry
        lax.fori_loop(0, n_tok, fill, 0, unroll=SCALAR_UNROLL)
        start_block(0, 0)

    @pl.when(i + 1 < nu)
    def _():
        start_block(i + 1, (i + 1) % 2)

    @pl.when(i < nu)
    def _():
        slot = i % 2
        wait_block(i, slot)
        xs_ref[...] = buf_ref[slot].astype(xs_ref.dtype)

    @pl.when(i >= nu)
    def _():
        xs_ref[...] = jnp.zeros_like(xs_ref)


def moe_gather(hn, slots, n_used, n_blocks, blk):
    t, d = hn.shape
    grid_spec = pltpu.PrefetchScalarGridSpec(
        num_scalar_prefetch=6,
        grid=(n_blocks,),
        in_specs=[pl.BlockSpec(memory_space=pl.ANY)],
        out_specs=pl.BlockSpec((blk, d), lambda i, *_: (i, 0)),
        scratch_shapes=[pltpu.SMEM((n_blocks * blk,), jnp.int32),
                        pltpu.VMEM((2, blk, d), F32),
                        pltpu.SemaphoreType.DMA((2,))],
    )
    return pl.pallas_call(
        functools.partial(_gather_kernel, blk=blk, n_tok=t),
        grid_spec=grid_spec,
        out_shape=jax.ShapeDtypeStruct((n_blocks * blk, d), BF16),
        compiler_params=_params(("arbitrary",)),
        name="moe_gather",
    )(*slots, n_used, hn)


def _expert_changed(be_ref, i):
    prev = be_ref[jnp.maximum(i - 1, 0)]
    return (i == 0) | (be_ref[i] != prev)


def _stream_expert_weights(w_hbms, col0, be_ref, nx_ref, nu, i, wst_ref, wbf_ref, sem, slot_ref):
    tn = wst_ref.shape[-1]

    def copies(e, slot):
        return [pltpu.make_async_copy(w.at[e, :, pl.ds(col0, tn)], wst_ref.at[slot, l], sem.at[slot])
                for l, w in enumerate(w_hbms)]

    @pl.when(i == 0)
    def _():
        slot_ref[0] = 0
        for c in copies(be_ref[0], 0):
            c.start()

    @pl.when(_expert_changed(be_ref, i))
    def _():
        slot = slot_ref[0]
        for c in copies(be_ref[i], slot):
            c.wait()
        nxt = nx_ref[i]

        @pl.when(nxt < nu)
        def _():
            for c in copies(be_ref[jnp.minimum(nxt, be_ref.shape[0] - 1)], 1 - slot):
                c.start()

        for l in range(len(w_hbms)):
            _cast_rows(wst_ref.at[slot, l], wbf_ref.at[l])
        slot_ref[0] = 1 - slot


def _moe_up_kernel(be_ref, nx_ref, cs_ref, ce_ref, nu_ref, xs_ref, wg_hbm, wu_hbm, act_ref,
                   wbf_ref, stage_ref, sem, cur_ref):
    i = pl.program_id(0)
    nu = nu_ref[0]
    n_slots, _, kc, _ = stage_ref.shape
    n_chunks = wbf_ref.shape[2] // kc
    mats = (wg_hbm, wu_hbm)

    def chunk_rows(c):
        return pl.ds(pl.multiple_of(c * kc, kc), kc)

    def chunk_copies(e, c, slot):
        return [pltpu.make_async_copy(w.at[e, chunk_rows(c), :], stage_ref.at[slot, l], sem.at[slot])
                for l, w in enumerate(mats)]

    def start(e, c, slot):
        for cp in chunk_copies(e, c, slot):
            cp.start()

    def wait(e, c, slot):
        for cp in chunk_copies(e, c, slot):
            cp.wait()

    def cast(buf, c, slot):
        for l in range(len(mats)):
            wbf_ref[buf, l, chunk_rows(c), :] = stage_ref[slot, l].astype(BF16)

    @pl.when(i < nu)
    def _():
        first = _expert_changed(be_ref, i)

        @pl.when(i == 0)
        def _():
            cur_ref[0] = 0
            e = be_ref[0]
            for c in range(min(n_slots, n_chunks)):
                start(e, c, c)
            for c in range(n_chunks):
                wait(e, c, c % n_slots)
                cast(0, c, c % n_slots)
                if c + n_slots < n_chunks:
                    start(e, c + n_slots, c % n_slots)

        @pl.when(first & (i > 0))
        def _():
            cur_ref[0] = 1 - cur_ref[0]

        cur = cur_ref[0]
        nxt = nx_ref[i]
        has_next = nxt < nu
        e_next = be_ref[jnp.minimum(nxt, be_ref.shape[0] - 1)]

        @pl.when(first & has_next)
        def _():
            for c in range(min(n_slots, n_chunks)):
                start(e_next, c, c)

        @pl.when(has_next)
        def _():
            def body(c, carry):
                slot = lax.rem(c, n_slots)
                wait(e_next, c, slot)
                cast(1 - cur, c, slot)

                @pl.when(c + n_slots < n_chunks)
                def _():
                    start(e_next, c + n_slots, slot)
                return carry

            lax.fori_loop(cs_ref[i], ce_ref[i], body, 0)

        x = xs_ref[...]
        gate = jnp.dot(x, wbf_ref[cur, 0], preferred_element_type=F32)
        up = jnp.dot(x, wbf_ref[cur, 1], preferred_element_type=F32)
        act_ref[...] = (gate * _sigmoid(gate) * up).astype(act_ref.dtype)

    @pl.when(i >= nu)
    def _():
        act_ref[...] = jnp.zeros_like(act_ref)


def _moe_down_kernel(be_ref, nx_ref, nu_ref, act_ref, wd_hbm, y_ref, wst_ref, wbf_ref, sem, slot_ref):
    j = pl.program_id(0)
    i = pl.program_id(1)
    nu = nu_ref[0]
    tn = y_ref.shape[1]

    @pl.when(i < nu)
    def _():
        _stream_expert_weights([wd_hbm], pl.multiple_of(j * tn, tn), be_ref, nx_ref, nu, i,
                               wst_ref, wbf_ref, sem, slot_ref)
        y_ref[...] = jnp.dot(act_ref[...], wbf_ref[0], preferred_element_type=F32)

    @pl.when(i >= nu)
    def _():
        y_ref[...] = jnp.zeros_like(y_ref)


def moe_experts(xs, block_expert, next_expert_block, chunk_lo, chunk_hi, n_used, w_gate, w_up, w_down,
                blk=MOE_BLK, tn=MOE_TN):
    r, dw = xs.shape
    _, d, f = w_gate.shape
    n_blocks = r // blk
    tn = min(tn, d)
    kc = d // MOE_UP_CHUNKS

    def used(i, nu):
        return jnp.minimum(i, jnp.maximum(nu[0] - 1, 0))

    def stream_scratch(n_mats, k, n):
        return [pltpu.VMEM((2, n_mats, k, n), F32),
                pltpu.VMEM((n_mats, k, n), BF16),
                pltpu.SemaphoreType.DMA((2,)),
                pltpu.SMEM((1,), jnp.int32)]

    up_spec = pltpu.PrefetchScalarGridSpec(
        num_scalar_prefetch=5,
        grid=(n_blocks,),
        in_specs=[pl.BlockSpec((blk, dw), lambda i, be, nx, cs, ce, nu: (used(i, nu), 0)),
                  pl.BlockSpec(memory_space=pl.ANY),
                  pl.BlockSpec(memory_space=pl.ANY)],
        out_specs=pl.BlockSpec((blk, f), lambda i, be, nx, cs, ce, nu: (i, 0)),
        scratch_shapes=[pltpu.VMEM((2, 2, d, f), BF16),
                        pltpu.VMEM((MOE_UP_SLOTS, 2, kc, f), F32),
                        pltpu.SemaphoreType.DMA((MOE_UP_SLOTS,)),
                        pltpu.SMEM((1,), jnp.int32)],
    )
    act = pl.pallas_call(
        _moe_up_kernel,
        grid_spec=up_spec,
        out_shape=jax.ShapeDtypeStruct((r, f), BF16),
        compiler_params=_params(("arbitrary",)),
        name="moe_up",
    )(block_expert, next_expert_block, chunk_lo, chunk_hi, n_used, xs, w_gate, w_up)
    down_spec = pltpu.PrefetchScalarGridSpec(
        num_scalar_prefetch=3,
        grid=(d // tn, n_blocks),
        in_specs=[pl.BlockSpec((blk, f), lambda j, i, be, nx, nu: (used(i, nu), 0)),
                  pl.BlockSpec(memory_space=pl.ANY)],
        out_specs=pl.BlockSpec((blk, tn), lambda j, i, be, nx, nu: (i, j)),
        scratch_shapes=stream_scratch(1, f, tn),
    )
    return pl.pallas_call(
        _moe_down_kernel,
        grid_spec=down_spec,
        out_shape=jax.ShapeDtypeStruct((r, d), F32),
        compiler_params=_params(("arbitrary", "arbitrary")),
        name="moe_down",
    )(block_expert, next_expert_block, n_used, act, w_down)


def _combine_kernel(e0_ref, e1_ref, r0_ref, r1_ref, ps_ref, x_ref, route_ref, g_ref, y_hbm, o_ref,
                    ya_ref, yb_ref, sem, *, tb, final_norm):
    step = pl.program_id(0)

    def copies(b, slot, i):
        t = b * tb + i
        row0 = ps_ref[e0_ref[t]] + r0_ref[t]
        row1 = ps_ref[e1_ref[t]] + r1_ref[t]
        return (pltpu.make_async_copy(y_hbm.at[pl.ds(row0, 1), :],
                                      ya_ref.at[slot, pl.ds(i, 1), :], sem.at[slot]),
                pltpu.make_async_copy(y_hbm.at[pl.ds(row1, 1), :],
                                      yb_ref.at[slot, pl.ds(i, 1), :], sem.at[slot]))

    def for_rows(fn):
        def body(grp, carry):
            r0 = pl.multiple_of(grp * SUBLANES, SUBLANES)
            for k in range(SUBLANES):
                fn(r0 + k)
            return carry
        lax.fori_loop(0, tb // SUBLANES, body, 0)

    def start_block(b, slot):
        def start(i):
            c0, c1 = copies(b, slot, i)
            c0.start()
            c1.start()
        for_rows(start)

    def wait_block(b, slot):
        def wait(i):
            c0, c1 = copies(b, slot, i)
            c0.wait()
            c1.wait()
        for_rows(wait)

    @pl.when(step == 0)
    def _():
        start_block(0, 0)

    @pl.when(step + 1 < pl.num_programs(0))
    def _():
        start_block(step + 1, (step + 1) % 2)

    slot = step % 2
    wait_block(step, slot)
    w0 = route_ref[:, 2:3]
    w1 = route_ref[:, 3:4]
    x = x_ref[...] + (ya_ref[slot] * w0 + yb_ref[slot] * w1)
    if final_norm:
        ms = jnp.mean(x * x, axis=-1, keepdims=True)
        x = x * lax.rsqrt(ms + NORM_EPS) * g_ref[...]
    o_ref[...] = x


def combine(x, route, y, slots, g, final_norm, tb=COMB_ROWS):
    t, d = x.shape
    tb = min(tb, t)
    grid_spec = pltpu.PrefetchScalarGridSpec(
        num_scalar_prefetch=5,
        grid=(t // tb,),
        in_specs=[pl.BlockSpec((tb, d), lambda i, *_: (i, 0)),
                  pl.BlockSpec((tb, ROUTE_LANES), lambda i, *_: (i, 0)),
                  pl.BlockSpec((1, d), lambda i, *_: (0, 0)),
                  pl.BlockSpec(memory_space=pl.ANY)],
        out_specs=pl.BlockSpec((tb, d), lambda i, *_: (i, 0)),
        scratch_shapes=[pltpu.VMEM((2, tb, d), F32), pltpu.VMEM((2, tb, d), F32),
                        pltpu.SemaphoreType.DMA((2,))],
    )
    return pl.pallas_call(
        functools.partial(_combine_kernel, tb=tb, final_norm=final_norm),
        grid_spec=grid_spec,
        out_shape=jax.ShapeDtypeStruct((t, d), F32),
        compiler_params=_params(("arbitrary",)),
        name="moe_combine",
    )(*slots, x, route, g.reshape(1, d), y)


def _route_lanes(group_part, expert_part):
    rows = group_part.shape[0]
    gap = jnp.zeros((rows, EXPERT_LANE0 - GROUP_LANE0 - N_GROUPS), F32)
    tail = jnp.zeros((rows, ROUTE_LANES - EXPERT_LANE0 - N_EXPERTS), F32)
    return jnp.concatenate([group_part, gap, expert_part, tail], axis=1)


def _moe_layout(route, counts, blk):
    t = route.shape[0]
    ri = route[:, :8].astype(jnp.int32)
    e0, e1, rank0, rank1 = ri[:, 0], ri[:, 1], ri[:, 4], ri[:, 5]
    cnt = counts[0, EXPERT_LANE0:EXPERT_LANE0 + N_EXPERTS].astype(jnp.int32)
    padded = (cnt + blk - 1) // blk * blk
    pends = jnp.cumsum(padded)
    pstarts = pends - padded
    n_blocks = (2 * t) // blk + N_EXPERTS
    block_start = jnp.arange(n_blocks, dtype=jnp.int32) * blk
    block_expert = jnp.minimum(
        jnp.sum((block_start[:, None] >= pends[None, :]).astype(jnp.int32), axis=1), N_EXPERTS - 1)
    n_used = (pends[-1] // blk).astype(jnp.int32).reshape(1)
    block_expert = block_expert[jnp.minimum(jnp.arange(n_blocks), jnp.maximum(n_used[0] - 1, 0))]
    next_expert_block = pends[block_expert] // blk
    first_block = pstarts[block_expert] // blk
    k = jnp.arange(n_blocks, dtype=jnp.int32) - first_block
    n = jnp.maximum(next_expert_block - first_block, 1)
    chunk_lo = (MOE_UP_CHUNKS * k) // n
    chunk_hi = (MOE_UP_CHUNKS * (k + 1)) // n
    return ((e0, e1, rank0, rank1, pstarts), block_expert, next_expert_block, chunk_lo, chunk_hi,
            n_used, n_blocks)


def kernel(x, mem, positions, mix_norm_g, w_in, ret_norm_g, lru_conv_w, lru_conv_b, lru_w_a, lru_b_a, lru_w_i, lru_b_i, lru_lambda, lru_norm_g, w_out, xattn_norm_g, mem_norm_g, xattn_wq, xattn_wk, xattn_wv, xattn_wo, moe_norm_g, router_group_w, router_group_b, router_expert_w, router_expert_b, expert_w_gate, expert_w_up, expert_w_down, final_norm_g):
    b, s, d = x.shape
    depth = w_in.shape[0]
    ret_width = RET_HEADS * RET_HEAD_DIM
    lru_width = lru_conv_w.shape[-1]
    assert ret_width == lru_width and ret_width + lru_width == d
    inv_freq = ROPE_BASE ** (-jnp.arange(0, RET_HEAD_DIM, 2, dtype=F32) / RET_HEAD_DIM)
    lg = jnp.log1p(-jnp.exp2(-5.0 - jnp.arange(RET_HEADS, dtype=F32)))
    lg_rows = jnp.broadcast_to(lg[:, None, None], (RET_HEADS, 1, RET_HEAD_DIM))
    blk = min(MOE_BLK, s)
    outs = []
    for bi in range(b):
        xcur = x[bi]
        cos, sin = rope_tables(positions[bi].astype(F32), inv_freq)
        for l in range(depth):
            h = normcast(xcur, mix_norm_g[l], BF16, NORM_ROWS)
            proj = matmul_streamed([h], w_in[l], F32)
            ret = retention(proj, cos, sin, lg_rows, ret_norm_g[l])
            lru = rg_lru(proj, 4 * ret_width // lru_width, 4 * ret_width // lru_width + 1,
                         lru_conv_w[l], lru_conv_b[l], lru_w_a[l], lru_b_a[l], lru_w_i[l],
                         lru_b_i[l], lru_lambda[l], lru_norm_g[l])
            xcur = matmul_streamed([ret, lru], w_out[l], F32, res=xcur, tm=MM_STREAM_TM // 2)
            memn = normcast(mem[bi], mem_norm_g[l], BF16, NORM_ROWS)
            kk = matmul([memn], xattn_wk[l], BF16)
            vv = matmul([memn], xattn_wv[l], BF16)
            wqk, vo = xattn_fold(kk, vv, xattn_wq[l], xattn_wo[l])
            xcur = xattn(xcur, xattn_norm_g[l], wqk, vo)
            wr = _route_lanes(router_group_w[l], router_expert_w[l])
            br = _route_lanes(router_group_b[l][None], router_expert_b[l][None])
            hn, route, counts = router(xcur, moe_norm_g[l], wr, br)
            slots, block_expert, next_block, chunk_lo, chunk_hi, n_used, n_blocks = _moe_layout(
                route, counts, blk)
            xs = moe_gather(hn, slots, n_used, n_blocks, blk)
            y = moe_experts(xs, block_expert, next_block, chunk_lo, chunk_hi, n_used,
                            expert_w_gate[l], expert_w_up[l],
                            expert_w_down[l], blk=blk)
            xcur = combine(xcur, route, y, slots, final_norm_g, final_norm=l == depth - 1)
        outs.append(xcur)
    return outs[0][None] if b == 1 else jnp.stack(outs, axis=0)
```

```python
import functools

import jax
import jax.numpy as jnp
from jax import lax
from jax.experimental import pallas as pl
from jax.experimental.pallas import tpu as pltpu

F32 = jnp.float32
BF16 = jnp.bfloat16

RET_HEADS = 8
RET_HEAD_DIM = 256
RET_CHUNK = 128
LRU_BLOCKS = 8
CONV_WIDTH = 4
RG_C = 8.0
ROPE_BASE = 10000.0
X_HEADS = 4
N_GROUPS = 4
EXPERTS_PER_GROUP = 8
N_EXPERTS = N_GROUPS * EXPERTS_PER_GROUP
NORM_EPS = 1e-6
GN_EPS = 1e-5

LANES = 128
SUBLANES = 8
VMEM_LIMIT = 56 * 1024 * 1024

NORM_ROWS = 512
MM_TM = 1024
MM_TN = 512
MM_STREAM_TM = 1024
MM_STREAM_TN = 1024
RET_ROWS = 512
RET_HEADS_PER_STEP = 8
LRU_ROWS = 256
ATT_ROWS = 512
ROUTE_ROWS = 256
MOE_BLK = 256
MOE_UP_CHUNKS = 16
MOE_UP_SLOTS = 6
MOE_TN = 4096
XF_TILE = 1024
SCALAR_UNROLL = 8
COMB_ROWS = 256
ROUTE_LANES = LANES
GROUP_LANE0 = 0
EXPERT_LANE0 = 8


def _params(sem):
    return pltpu.CompilerParams(dimension_semantics=sem, vmem_limit_bytes=VMEM_LIMIT)


def _normcast_kernel(x_ref, g_ref, o_ref):
    x = x_ref[...]
    ms = jnp.mean(x * x, axis=-1, keepdims=True)
    o_ref[...] = (x * lax.rsqrt(ms + NORM_EPS) * g_ref[...]).astype(o_ref.dtype)


def normcast(x, g, out_dtype, tm):
    m, d = x.shape
    tm = min(tm, m)
    return pl.pallas_call(
        _normcast_kernel,
        grid=(m // tm,),
        in_specs=[pl.BlockSpec((tm, d), lambda i: (i, 0)),
                  pl.BlockSpec((1, d), lambda i: (0, 0))],
        out_specs=pl.BlockSpec((tm, d), lambda i: (i, 0)),
        out_shape=jax.ShapeDtypeStruct((m, d), out_dtype),
        compiler_params=_params(("arbitrary",)),
        name="normcast",
    )(x, g.reshape(1, d))


def _cast_rows(src_ref, dst_ref, rows_per_iter=256):
    k = src_ref.shape[0]
    step = min(rows_per_iter, k)

    def body(i, carry):
        r0 = pl.multiple_of(i * step, step)
        dst_ref[pl.ds(r0, step), :] = src_ref[pl.ds(r0, step), :].astype(dst_ref.dtype)
        return carry

    lax.fori_loop(0, k // step, body, 0)


def _mm_kernel(*refs, n_a, has_res):
    a_refs = refs[:n_a]
    w_ref = refs[n_a]
    res_ref = refs[n_a + 1] if has_res else None
    o_ref = refs[n_a + 1 + int(has_res)]
    wbf_ref = refs[n_a + 2 + int(has_res)]

    @pl.when(pl.program_id(1) == 0)
    def _():
        _cast_rows(w_ref, wbf_ref)

    kp = a_refs[0].shape[1]
    acc = None
    for p, a_ref in enumerate(a_refs):
        d = jnp.dot(a_ref[...], wbf_ref[p * kp:(p + 1) * kp, :], preferred_element_type=F32)
        acc = d if acc is None else acc + d
    if has_res:
        acc = acc + res_ref[...]
    o_ref[...] = acc.astype(o_ref.dtype)


def matmul(a_parts, w, out_dtype, res=None, tm=MM_TM, tn=MM_TN):
    m, kp = a_parts[0].shape
    k, n = w.shape
    assert kp * len(a_parts) == k
    tm = min(tm, m)
    tn = min(tn, n)
    in_specs = [pl.BlockSpec((tm, kp), lambda j, i: (i, 0)) for _ in a_parts]
    in_specs.append(pl.BlockSpec((k, tn), lambda j, i: (0, j)))
    args = list(a_parts) + [w]
    if res is not None:
        in_specs.append(pl.BlockSpec((tm, tn), lambda j, i: (i, j)))
        args.append(res)
    return pl.pallas_call(
        functools.partial(_mm_kernel, n_a=len(a_parts), has_res=res is not None),
        grid=(n // tn, m // tm),
        in_specs=in_specs,
        out_specs=pl.BlockSpec((tm, tn), lambda j, i: (i, j)),
        out_shape=jax.ShapeDtypeStruct((m, n), out_dtype),
        scratch_shapes=[pltpu.VMEM((k, tn), BF16)],
        compiler_params=_params(("arbitrary", "arbitrary")),
        name="matmul",
    )(*args)


def _mm_stream_kernel(*refs, n_a, has_res):
    a_refs = refs[:n_a]
    w_hbm = refs[n_a]
    res_ref = refs[n_a + 1] if has_res else None
    o_ref = refs[n_a + 1 + int(has_res)]
    wbf_ref, stage_ref, sem = refs[n_a + 2 + int(has_res):]
    j = pl.program_id(0)
    i = pl.program_id(1)
    n_j = pl.num_programs(0)
    n_i = pl.num_programs(1)
    kc, tn = stage_ref.shape[1:]
    n_chunks = wbf_ref.shape[1] // kc

    def chunk_copy(col_tile, c, slot):
        rows = pl.ds(pl.multiple_of(c * kc, kc), kc)
        cols = pl.ds(pl.multiple_of(col_tile * tn, tn), tn)
        return pltpu.make_async_copy(w_hbm.at[rows, cols], stage_ref.at[slot], sem.at[slot])

    def cast_chunk(buf, c, slot):
        rows = pl.ds(pl.multiple_of(c * kc, kc), kc)
        wbf_ref[buf, rows, :] = stage_ref[slot].astype(BF16)

    @pl.when((j == 0) & (i == 0))
    def _():
        chunk_copy(0, 0, 0).start()
        for c in range(n_chunks):
            if c + 1 < n_chunks:
                chunk_copy(0, c + 1, (c + 1) % 2).start()
            chunk_copy(0, c, c % 2).wait()
            cast_chunk(0, c, c % 2)

    cur = j % 2
    kp = a_refs[0].shape[1]
    acc = None
    for p, a_ref in enumerate(a_refs):
        d = jnp.dot(a_ref[...], wbf_ref[cur, p * kp:(p + 1) * kp, :], preferred_element_type=F32)
        acc = d if acc is None else acc + d
    if has_res:
        acc = acc + res_ref[...]
    o_ref[...] = acc.astype(o_ref.dtype)

    @pl.when(j + 1 < n_j)
    def _():
        slot = i % 2

        @pl.when(i == 0)
        def _():
            chunk_copy(j + 1, 0, 0).start()

        chunk_copy(j + 1, i, slot).wait()

        @pl.when(i + 1 < n_i)
        def _():
            chunk_copy(j + 1, i + 1, 1 - slot).start()

        cast_chunk(1 - cur, i, slot)


def matmul_streamed(a_parts, w, out_dtype, res=None, tm=MM_STREAM_TM, tn=MM_STREAM_TN):
    m, kp = a_parts[0].shape
    k, n = w.shape
    assert kp * len(a_parts) == k
    tm = min(tm, m)
    tn = min(tn, n)
    n_i = m // tm
    kc = k // n_i
    assert kc * n_i == k and kc % SUBLANES == 0
    in_specs = [pl.BlockSpec((tm, kp), lambda j, i: (i, 0)) for _ in a_parts]
    in_specs.append(pl.BlockSpec(memory_space=pl.ANY))
    args = list(a_parts) + [w]
    if res is not None:
        in_specs.append(pl.BlockSpec((tm, tn), lambda j, i: (i, j)))
        args.append(res)
    return pl.pallas_call(
        functools.partial(_mm_stream_kernel, n_a=len(a_parts), has_res=res is not None),
        grid=(n // tn, n_i),
        in_specs=in_specs,
        out_specs=pl.BlockSpec((tm, tn), lambda j, i: (i, j)),
        out_shape=jax.ShapeDtypeStruct((m, n), out_dtype),
        scratch_shapes=[pltpu.VMEM((2, k, tn), BF16),
                        pltpu.VMEM((2, kc, tn), F32),
                        pltpu.SemaphoreType.DMA((2,))],
        compiler_params=_params(("arbitrary", "arbitrary")),
        name="matmul_streamed",
    )(*args)


def _rope_kernel(pos_ref, invf_ref, cos_ref, sin_ref):
    ang = pos_ref[...] * invf_ref[...]
    cos_ref[...] = jnp.cos(ang)
    sin_ref[...] = jnp.sin(ang)


def rope_tables(pos_f, inv_freq, tm=512):
    s = pos_f.shape[0]
    hd = inv_freq.shape[0]
    tm = min(tm, s)
    return pl.pallas_call(
        _rope_kernel,
        grid=(s // tm,),
        in_specs=[pl.BlockSpec((tm, 1), lambda i: (i, 0)),
                  pl.BlockSpec((1, hd), lambda i: (0, 0))],
        out_specs=[pl.BlockSpec((tm, hd), lambda i: (i, 0))] * 2,
        out_shape=[jax.ShapeDtypeStruct((s, hd), F32)] * 2,
        compiler_params=_params(("arbitrary",)),
        name="rope_tables",
    )(pos_f.reshape(s, 1), inv_freq.reshape(1, hd))


def _ret_kernel(q_ref, k_ref, v_ref, g_ref, cos_ref, sin_ref, lg_ref, gn_ref, o_ref, r_ref,
                decay_ref, xi_ref, zeta_ref, *, n_chunks, hpb):
    c = RET_CHUNK
    dk = RET_HEAD_DIM
    half = dk // 2
    scale = dk ** -0.5

    @pl.when(pl.program_id(1) == 0)
    def _():
        r_ref[...] = jnp.zeros_like(r_ref)
        row = lax.broadcasted_iota(jnp.int32, (c, c), 0).astype(F32)
        col = lax.broadcasted_iota(jnp.int32, (c, c), 1).astype(F32)
        diff = row - col
        rowk = lax.broadcasted_iota(jnp.int32, (c, dk), 0).astype(F32)
        for hh in range(hpb):
            lg = lg_ref[hh]
            decay_ref[hh] = jnp.where(diff >= 0, jnp.exp(lg[:, :c] * jnp.maximum(diff, 0.0)), 0.0)
            xi_ref[hh] = jnp.exp(lg * (rowk + 1.0))
            zeta_ref[hh] = jnp.exp(lg * (c - 1.0 - rowk))

    def rope(t, cos, sin):
        t1 = t[:, :half]
        t2 = t[:, half:]
        return jnp.concatenate([t1 * cos - t2 * sin, t1 * sin + t2 * cos], axis=-1)

    def body(j, carry):
        r0 = pl.multiple_of(j * c, c)
        rows = pl.ds(r0, c)
        cos = cos_ref[rows, :]
        sin = sin_ref[rows, :]
        for hh in range(hpb):
            cs = slice(hh * dk, (hh + 1) * dk)
            qr = rope(q_ref[rows, cs], cos, sin)
            kr = rope(k_ref[rows, cs], cos, sin) * scale
            qb = qr.astype(BF16)
            kb = kr.astype(BF16)
            vb = v_ref[rows, cs].astype(BF16)
            state = r_ref[hh]
            inner = lax.dot_general(qb, kb, (((1,), (1,)), ((), ())),
                                    preferred_element_type=F32) * decay_ref[hh]
            o = (jnp.dot(inner.astype(BF16), vb, preferred_element_type=F32)
                 + jnp.dot(qb, state.astype(BF16), preferred_element_type=F32) * xi_ref[hh])
            kz = (kr * zeta_ref[hh]).astype(BF16)
            chunk_decay = jnp.exp(lg_ref[hh] * c)
            r_ref[hh] = state * chunk_decay + lax.dot_general(
                kz, vb, (((0,), (0,)), ((), ())), preferred_element_type=F32)
            mu = jnp.mean(o, axis=-1, keepdims=True)
            oc = o - mu
            var = jnp.mean(oc * oc, axis=-1, keepdims=True)
            on = oc * lax.rsqrt(var + GN_EPS) * gn_ref[hh]
            g = g_ref[rows, cs]
            o_ref[rows, cs] = (on * (g * (1.0 / (1.0 + jnp.exp(-g))))).astype(o_ref.dtype)
        return carry

    lax.fori_loop(0, n_chunks, body, 0, unroll=2)


def retention(proj, cos, sin, lg_rows, gn_g, tr=RET_ROWS, hpb=RET_HEADS_PER_STEP):
    s = proj.shape[0]
    dk = RET_HEAD_DIM
    h = RET_HEADS
    tr = min(tr, s)
    c = RET_CHUNK
    w = hpb * dk

    def col(base):
        return pl.BlockSpec((tr, w), lambda hg, ci, base=base: (ci, base // hpb + hg))

    per_head = pl.BlockSpec((hpb, 1, dk), lambda hg, ci: (hg, 0, 0))
    return pl.pallas_call(
        functools.partial(_ret_kernel, n_chunks=tr // c, hpb=hpb),
        grid=(h // hpb, s // tr),
        in_specs=[col(0), col(h), col(2 * h), col(3 * h),
                  pl.BlockSpec((tr, dk // 2), lambda hg, ci: (ci, 0)),
                  pl.BlockSpec((tr, dk // 2), lambda hg, ci: (ci, 0)),
                  per_head, per_head],
        out_specs=pl.BlockSpec((tr, w), lambda hg, ci: (ci, hg)),
        out_shape=jax.ShapeDtypeStruct((s, h * dk), BF16),
        scratch_shapes=[pltpu.VMEM((hpb, dk, dk), F32),
                        pltpu.VMEM((hpb, c, c), F32),
                        pltpu.VMEM((hpb, c, dk), F32),
                        pltpu.VMEM((hpb, c, dk), F32)],
        compiler_params=_params(("arbitrary", "arbitrary")),
        name="retention",
    )(proj, proj, proj, proj, cos, sin, lg_rows, gn_g.reshape(h, 1, dk))


def _sigmoid(x):
    return 1.0 / (1.0 + jnp.exp(-x))


def _lru_kernel(xb_ref, gb_ref, cw_ref, cb_ref, wa_ref, ba_ref, wi_ref, bi_ref, lam_ref, og_ref,
                o_ref, tail_ref, xs_ref, hs_ref, h_ref, wabf_ref, wibf_ref):
    tr, cdim = xb_ref.shape
    nb = wa_ref.shape[0]
    bd = cdim // nb
    ph = SUBLANES
    ng = tr // ph

    @pl.when(pl.program_id(0) == 0)
    def _():
        tail_ref[...] = jnp.zeros_like(tail_ref)
        h_ref[...] = jnp.zeros_like(h_ref)
        wabf_ref[...] = wa_ref[...].astype(BF16)
        wibf_ref[...] = wi_ref[...].astype(BF16)

    lam = lam_ref[...]
    sp = jnp.maximum(-lam, 0.0) + jnp.log1p(jnp.exp(-jnp.abs(lam)))
    rowg = lax.broadcasted_iota(jnp.int32, (ng, bd), 0)
    lpb = bd // LANES
    for c in range(cdim // LANES):
        xs_ref[c] = xb_ref[:, c * LANES:(c + 1) * LANES]

    def phase_rows(ref, n, p):
        return jnp.concatenate([ref[n * lpb + c, pl.ds(p, ng, stride=ph), :] for c in range(lpb)],
                               axis=1)

    for n in range(nb):
        cs = slice(n * bd, (n + 1) * bd)
        x = [phase_rows(xs_ref, n, p) for p in range(ph)]

        def prev_group(p):
            return jnp.where(rowg == 0, tail_ref[p:p + 1, cs], pltpu.roll(x[p], 1, 0))

        back = {-k: prev_group(ph - k) for k in range(1, CONV_WIDTH)}

        def xat(p):
            return x[p] if p >= 0 else back[p]

        xc = []
        for p in range(ph):
            acc = cb_ref[:, cs] + cw_ref[CONV_WIDTH - 1:CONV_WIDTH, cs] * xat(p)
            for k in range(1, CONV_WIDTH):
                acc = acc + cw_ref[CONV_WIDTH - 1 - k:CONV_WIDTH - k, cs] * xat(p - k)
            xc.append(acc)
        xg = jnp.concatenate(xc, axis=0)
        xgb = xg.astype(BF16)
        r = _sigmoid(jnp.dot(xgb, wabf_ref[n], preferred_element_type=F32) + ba_ref[:, cs])
        ig = _sigmoid(jnp.dot(xgb, wibf_ref[n], preferred_element_type=F32) + bi_ref[:, cs])
        log_a = (-RG_C * r) * sp[:, cs]
        a = jnp.exp(log_a)
        b = jnp.sqrt(-jnp.tanh(log_a) * (a * a + 1.0)) * (ig * xg)

        cum_a = [a[0:ng]]
        cum_b = [b[0:ng]]
        for p in range(1, ph):
            ap = a[p * ng:(p + 1) * ng]
            cum_b.append(ap * cum_b[-1] + b[p * ng:(p + 1) * ng])
            cum_a.append(ap * cum_a[-1])
        sa, sb = cum_a[-1], cum_b[-1]
        d = 1
        while d < ng:
            keep = rowg >= d
            sa_sh = pltpu.roll(sa, d, 0)
            sb_sh = pltpu.roll(sb, d, 0)
            sb = jnp.where(keep, sa * sb_sh + sb, sb)
            sa = jnp.where(keep, sa * sa_sh, sa)
            d *= 2
        h_in = h_ref[:, cs]
        h_end = sa * h_in + sb
        h_prev = jnp.where(rowg == 0, h_in, pltpu.roll(h_end, 1, 0))
        for p in range(ph):
            hp = cum_a[p] * h_prev + cum_b[p]
            for c in range(lpb):
                hs_ref[n * lpb + c, pl.ds(p, ng, stride=ph), :] = hp[:, c * LANES:(c + 1) * LANES]
        h_ref[:, cs] = h_end[ng - 1:ng, :]

    gb = gb_ref[...]
    gelu = 0.5 * gb * (1.0 + jnp.tanh(0.7978845608028654 * (gb + 0.044715 * (gb * gb * gb))))
    y = jnp.concatenate([hs_ref[c] for c in range(cdim // LANES)], axis=1) * gelu
    ms = jnp.mean(y * y, axis=-1, keepdims=True)
    o_ref[...] = (y * lax.rsqrt(ms + NORM_EPS) * og_ref[...]).astype(o_ref.dtype)
    tail_ref[...] = xb_ref[tr - ph:tr, :]


def rg_lru(proj, xb_block, gb_block, conv_w, conv_b, w_a, b_a, w_i, b_i, lam, out_g, tr=LRU_ROWS):
    s = proj.shape[0]
    cdim = conv_w.shape[1]
    nb, bd, _ = w_a.shape
    tr = min(tr, s)
    vec = pl.BlockSpec((1, cdim), lambda i: (0, 0))
    wspec = pl.BlockSpec((nb, bd, bd), lambda i: (0, 0, 0))
    return pl.pallas_call(
        _lru_kernel,
        grid=(s // tr,),
        in_specs=[pl.BlockSpec((tr, cdim), lambda i: (i, xb_block)),
                  pl.BlockSpec((tr, cdim), lambda i: (i, gb_block)),
                  pl.BlockSpec((CONV_WIDTH, cdim), lambda i: (0, 0)),
                  vec, wspec, vec, wspec, vec, vec, vec],
        out_specs=pl.BlockSpec((tr, cdim), lambda i: (i, 0)),
        out_shape=jax.ShapeDtypeStruct((s, cdim), BF16),
        scratch_shapes=[pltpu.VMEM((SUBLANES, cdim), F32),
                        pltpu.VMEM((cdim // LANES, tr, LANES), F32),
                        pltpu.VMEM((cdim // LANES, tr, LANES), F32),
                        pltpu.VMEM((1, cdim), F32),
                        pltpu.VMEM((nb, bd, bd), BF16),
                        pltpu.VMEM((nb, bd, bd), BF16)],
        compiler_params=_params(("arbitrary",)),
        name="rg_lru",
    )(proj, proj, conv_w, conv_b.reshape(1, cdim), w_a, b_a.reshape(1, cdim), w_i,
      b_i.reshape(1, cdim), lam.reshape(1, cdim), out_g.reshape(1, cdim))


def _wqk_kernel(wq_ref, k_ref, o_ref):
    o_ref[...] = lax.dot_general(wq_ref[...].astype(BF16), k_ref[...], (((1,), (1,)), ((), ())),
                                 preferred_element_type=F32).astype(o_ref.dtype)


def _vo_kernel(v_ref, wo_ref, o_ref):
    o_ref[...] = jnp.dot(v_ref[...], wo_ref[...].astype(BF16),
                         preferred_element_type=F32).astype(o_ref.dtype)


def xattn_fold(k, v, wq, wo, tile=XF_TILE):
    mlen, d = k.shape
    hd = d // X_HEADS
    tile = min(tile, d)
    wqk = pl.pallas_call(
        _wqk_kernel,
        grid=(X_HEADS, d // tile),
        in_specs=[pl.BlockSpec((tile, hd), lambda h, r: (r, h)),
                  pl.BlockSpec((mlen, hd), lambda h, r: (0, h))],
        out_specs=pl.BlockSpec((tile, mlen), lambda h, r: (r, h)),
        out_shape=jax.ShapeDtypeStruct((d, X_HEADS * mlen), BF16),
        compiler_params=_params(("arbitrary", "arbitrary")),
        name="xattn_wqk",
    )(wq, k)
    vo = pl.pallas_call(
        _vo_kernel,
        grid=(X_HEADS, d // tile),
        in_specs=[pl.BlockSpec((mlen, hd), lambda h, j: (0, h)),
                  pl.BlockSpec((hd, tile), lambda h, j: (h, j))],
        out_specs=pl.BlockSpec((mlen, tile), lambda h, j: (h, j)),
        out_shape=jax.ShapeDtypeStruct((X_HEADS * mlen, d), BF16),
        compiler_params=_params(("arbitrary", "arbitrary")),
        name="xattn_vo",
    )(v, wo)
    return wqk, vo


def _xattn_kernel(x_ref, g_ref, wqk_ref, vo_ref, o_ref):
    d = x_ref.shape[1]
    mlen = wqk_ref.shape[1] // X_HEADS
    scale = (d // X_HEADS) ** -0.5
    x = x_ref[...]
    ms = jnp.mean(x * x, axis=-1, keepdims=True)
    h = (x * lax.rsqrt(ms + NORM_EPS) * g_ref[...]).astype(BF16)
    s = jnp.dot(h, wqk_ref[...], preferred_element_type=F32) * scale
    ps = []
    for hh in range(X_HEADS):
        sh = s[:, hh * mlen:(hh + 1) * mlen]
        m = jnp.max(sh, axis=-1, keepdims=True)
        e = jnp.exp(sh - m)
        ps.append((e / jnp.sum(e, axis=-1, keepdims=True)).astype(BF16))
    p = jnp.concatenate(ps, axis=1)
    o_ref[...] = x + jnp.dot(p, vo_ref[...], preferred_element_type=F32)


def xattn(x, g, wqk, vo, tm=ATT_ROWS):
    s, d = x.shape
    tm = min(tm, s)
    return pl.pallas_call(
        _xattn_kernel,
        grid=(s // tm,),
        in_specs=[pl.BlockSpec((tm, d), lambda i: (i, 0)),
                  pl.BlockSpec((1, d), lambda i: (0, 0)),
                  pl.BlockSpec(wqk.shape, lambda i: (0, 0), pipeline_mode=pl.Buffered(1)),
                  pl.BlockSpec(vo.shape, lambda i: (0, 0), pipeline_mode=pl.Buffered(1))],
        out_specs=pl.BlockSpec((tm, d), lambda i: (i, 0)),
        out_shape=jax.ShapeDtypeStruct((s, d), F32),
        compiler_params=_params(("arbitrary",)),
        name="xattn",
    )(x, g.reshape(1, d), wqk, vo)


def _split_bf16(a):
    hi = a.astype(BF16)
    return hi, (a - hi.astype(F32)).astype(BF16)


def _router_kernel(x_ref, g_ref, wr_ref, br_ref, h_ref, route_ref, counts_ref, carry_ref,
                   wsplit_ref):
    tm = x_ref.shape[0]
    nl = ROUTE_LANES

    @pl.when(pl.program_id(0) == 0)
    def _():
        carry_ref[...] = jnp.zeros_like(carry_ref)
        w_hi, w_lo = _split_bf16(wr_ref[...])
        wsplit_ref[:, :nl] = w_hi
        wsplit_ref[:, nl:] = w_lo

    x = x_ref[...]
    ms = jnp.mean(x * x, axis=-1, keepdims=True)
    h = x * lax.rsqrt(ms + NORM_EPS) * g_ref[...]
    h_ref[...] = h
    h_hi, h_lo = _split_bf16(h)
    both = jnp.dot(h_hi, wsplit_ref[...], preferred_element_type=F32)
    cross = jnp.dot(h_lo, wsplit_ref[:, :nl], preferred_element_type=F32)
    logits = both[:, :nl] + (both[:, nl:] + cross) + br_ref[...]
    lane = lax.broadcasted_iota(jnp.int32, (tm, ROUTE_LANES), 1).astype(F32)
    neg = -jnp.inf
    big = float(ROUTE_LANES)

    gmask = (lane >= GROUP_LANE0) & (lane < GROUP_LANE0 + N_GROUPS)
    gl = jnp.where(gmask, logits, neg)
    gmax = jnp.max(gl, axis=-1, keepdims=True)
    gsum = jnp.sum(jnp.where(gmask, jnp.exp(gl - gmax), 0.0), axis=-1, keepdims=True)
    g_val = 1.0 / gsum
    g_idx = jnp.min(jnp.where(gl == gmax, lane, big), axis=-1, keepdims=True) - GROUP_LANE0

    lo = EXPERT_LANE0 + g_idx * EXPERTS_PER_GROUP
    emask = (lane >= lo) & (lane < lo + EXPERTS_PER_GROUP)
    el = jnp.where(emask, logits, neg)
    t1 = jnp.max(el, axis=-1, keepdims=True)
    i1 = jnp.min(jnp.where(emask & (el == t1), lane, big), axis=-1, keepdims=True)
    emask2 = emask & (lane != i1)
    el2 = jnp.where(emask2, logits, neg)
    t2 = jnp.max(el2, axis=-1, keepdims=True)
    i2 = jnp.min(jnp.where(emask2 & (el2 == t2), lane, big), axis=-1, keepdims=True)
    dexp = jnp.exp(t2 - t1)
    w0 = g_val / (1.0 + dexp)
    w1 = g_val * dexp / (1.0 + dexp)

    sel1 = lane == i1
    sel2 = lane == i2
    onehot = jnp.where(sel1 | sel2, 1.0, 0.0)
    rr = lax.broadcasted_iota(jnp.int32, (tm, tm), 0)
    cc = lax.broadcasted_iota(jnp.int32, (tm, tm), 1)
    tri = jnp.where(cc < rr, 1.0, 0.0).astype(BF16)
    prefix = jnp.dot(tri, onehot.astype(BF16), preferred_element_type=F32) + carry_ref[...]
    rank0 = jnp.sum(jnp.where(sel1, prefix, 0.0), axis=-1, keepdims=True)
    rank1 = jnp.sum(jnp.where(sel2, prefix, 0.0), axis=-1, keepdims=True)
    total = carry_ref[...] + jnp.sum(onehot, axis=0, keepdims=True)
    carry_ref[...] = total
    counts_ref[...] = total

    e0 = i1 - EXPERT_LANE0
    e1 = i2 - EXPERT_LANE0
    route = jnp.where(lane == 0, e0, 0.0)
    route = jnp.where(lane == 1, e1, route)
    route = jnp.where(lane == 2, w0, route)
    route = jnp.where(lane == 3, w1, route)
    route = jnp.where(lane == 4, rank0, route)
    route = jnp.where(lane == 5, rank1, route)
    route_ref[...] = route


def router(x, g, wr, br, tm=ROUTE_ROWS):
    t, d = x.shape
    tm = min(tm, t)
    return pl.pallas_call(
        _router_kernel,
        grid=(t // tm,),
        in_specs=[pl.BlockSpec((tm, d), lambda i: (i, 0)),
                  pl.BlockSpec((1, d), lambda i: (0, 0)),
                  pl.BlockSpec((d, ROUTE_LANES), lambda i: (0, 0)),
                  pl.BlockSpec((1, ROUTE_LANES), lambda i: (0, 0))],
        out_specs=[pl.BlockSpec((tm, d), lambda i: (i, 0)),
                   pl.BlockSpec((tm, ROUTE_LANES), lambda i: (i, 0)),
                   pl.BlockSpec((1, ROUTE_LANES), lambda i: (0, 0))],
        out_shape=[jax.ShapeDtypeStruct((t, d), F32),
                   jax.ShapeDtypeStruct((t, ROUTE_LANES), F32),
                   jax.ShapeDtypeStruct((1, ROUTE_LANES), F32)],
        scratch_shapes=[pltpu.VMEM((1, ROUTE_LANES), F32),
                        pltpu.VMEM((d, 2 * ROUTE_LANES), BF16)],
        compiler_params=_params(("arbitrary",)),
        name="router",
    )(x, g.reshape(1, d), wr, br)


def _gather_kernel(e0_ref, e1_ref, r0_ref, r1_ref, ps_ref, nu_ref, x_hbm, xs_ref,
                   rowtok_ref, buf_ref, sem, *, blk, n_tok):
    i = pl.program_id(0)
    nu = nu_ref[0]

    def row_copy(b, slot, r):
        tok = rowtok_ref[b * blk + r]
        return pltpu.make_async_copy(x_hbm.at[pl.ds(tok, 1), :],
                                     buf_ref.at[slot, pl.ds(r, 1), :], sem.at[slot])

    def for_rows(fn):
        def body(grp, carry):
            r0 = pl.multiple_of(grp * SUBLANES, SUBLANES)
            for k in range(SUBLANES):
                fn(r0 + k)
            return carry
        lax.fori_loop(0, blk // SUBLANES, body, 0)

    def start_block(b, slot):
        for_rows(lambda r: row_copy(b, slot, r).start())

    def wait_block(b, slot):
        for_rows(lambda r: row_copy(b, slot, r).wait())

    @pl.when(i == 0)
    def _():
        n_rows = rowtok_ref.shape[0]
        for base in range(0, n_rows, n_tok):
            def init(r, carry, base=base):
                rowtok_ref[base + r] = r
                return carry
            lax.fori_loop(0, min(n_tok, n_rows - base), init, 0, unroll=SCALAR_UNROLL)

        def fill(t, carry):
            rowtok_ref[ps_ref[e0_ref[t]] + r0_ref[t]] = t
            rowtok_ref[ps_ref[e1_ref[t]] + r1_ref[t]] = t
            return carry
        lax.fori_loop(0, n_tok, fill, 0, unroll=SCALAR_UNROLL)
        start_block(0, 0)

    @pl.when(i + 1 < nu)
    def _():
        start_block(i + 1, (i + 1) % 2)

    @pl.when(i < nu)
    def _():
        slot = i % 2
        wait_block(i, slot)
        xs_ref[...] = buf_ref[slot].astype(xs_ref.dtype)

    @pl.when(i >= nu)
    def _():
        xs_ref[...] = jnp.zeros_like(xs_ref)


def moe_gather(hn, slots, n_used, n_blocks, blk):
    t, d = hn.shape
    grid_spec = pltpu.PrefetchScalarGridSpec(
        num_scalar_prefetch=6,
        grid=(n_blocks,),
        in_specs=[pl.BlockSpec(memory_space=pl.ANY)],
        out_specs=pl.BlockSpec((blk, d), lambda i, *_: (i, 0)),
        scratch_shapes=[pltpu.SMEM((n_blocks * blk,), jnp.int32),
                        pltpu.VMEM((2, blk, d), F32),
                        pltpu.SemaphoreType.DMA((2,))],
    )
    return pl.pallas_call(
        functools.partial(_gather_kernel, blk=blk, n_tok=t),
        grid_spec=grid_spec,
        out_shape=jax.ShapeDtypeStruct((n_blocks * blk, d), BF16),
        compiler_params=_params(("arbitrary",)),
        name="moe_gather",
    )(*slots, n_used, hn)


def _expert_changed(be_ref, i):
    prev = be_ref[jnp.maximum(i - 1, 0)]
    return (i == 0) | (be_ref[i] != prev)


def _stream_expert_weights(w_hbms, col0, be_ref, nx_ref, nu, i, wst_ref, wbf_ref, sem, slot_ref):
    tn = wst_ref.shape[-1]

    def copies(e, slot):
        return [pltpu.make_async_copy(w.at[e, :, pl.ds(col0, tn)], wst_ref.at[slot, l], sem.at[slot])
                for l, w in enumerate(w_hbms)]

    @pl.when(i == 0)
    def _():
        slot_ref[0] = 0
        for c in copies(be_ref[0], 0):
            c.start()

    @pl.when(_expert_changed(be_ref, i))
    def _():
        slot = slot_ref[0]
        for c in copies(be_ref[i], slot):
            c.wait()
        nxt = nx_ref[i]

        @pl.when(nxt < nu)
        def _():
            for c in copies(be_ref[jnp.minimum(nxt, be_ref.shape[0] - 1)], 1 - slot):
                c.start()

        for l in range(len(w_hbms)):
            _cast_rows(wst_ref.at[slot, l], wbf_ref.at[l])
        slot_ref[0] = 1 - slot


def _moe_up_kernel(be_ref, nx_ref, cs_ref, ce_ref, nu_ref, xs_ref, wg_hbm, wu_hbm, act_ref,
                   wbf_ref, stage_ref, sem, cur_ref):
    i = pl.program_id(0)
    nu = nu_ref[0]
    n_slots, _, kc, _ = stage_ref.shape
    n_chunks = wbf_ref.shape[2] // kc
    mats = (wg_hbm, wu_hbm)

    def chunk_rows(c):
        return pl.ds(pl.multiple_of(c * kc, kc), kc)

    def chunk_copies(e, c, slot):
        return [pltpu.make_async_copy(w.at[e, chunk_rows(c), :], stage_ref.at[slot, l], sem.at[slot])
                for l, w in enumerate(mats)]

    def start(e, c, slot):
        for cp in chunk_copies(e, c, slot):
            cp.start()

    def wait(e, c, slot):
        for cp in chunk_copies(e, c, slot):
            cp.wait()

    def cast(buf, c, slot):
        for l in range(len(mats)):
            wbf_ref[buf, l, chunk_rows(c), :] = stage_ref[slot, l].astype(BF16)

    @pl.when(i < nu)
    def _():
        first = _expert_changed(be_ref, i)

        @pl.when(i == 0)
        def _():
            cur_ref[0] = 0
            e = be_ref[0]
            for c in range(min(n_slots, n_chunks)):
                start(e, c, c)
            for c in range(n_chunks):
                wait(e, c, c % n_slots)
                cast(0, c, c % n_slots)
                if c + n_slots < n_chunks:
                    start(e, c + n_slots, c % n_slots)

        @pl.when(first & (i > 0))
        def _():
            cur_ref[0] = 1 - cur_ref[0]

        cur = cur_ref[0]
        nxt = nx_ref[i]
        has_next = nxt < nu
        e_next = be_ref[jnp.minimum(nxt, be_ref.shape[0] - 1)]

        @pl.when(first & has_next)
        def _():
            for c in range(min(n_slots, n_chunks)):
                start(e_next, c, c)

        x = xs_ref[...]
        gate = jnp.dot(x, wbf_ref[cur, 0], preferred_element_type=F32)
        up = jnp.dot(x, wbf_ref[cur, 1], preferred_element_type=F32)
        act_ref[...] = (gate * _sigmoid(gate) * up).astype(act_ref.dtype)

        @pl.when(has_next)
        def _():
            def body(c, carry):
                slot = lax.rem(c, n_slots)
                wait(e_next, c, slot)
                cast(1 - cur, c, slot)

                @pl.when(c + n_slots < n_chunks)
                def _():
                    start(e_next, c + n_slots, slot)
                return carry

            lax.fori_loop(cs_ref[i], ce_ref[i], body, 0)

    @pl.when(i >= nu)
    def _():
        act_ref[...] = jnp.zeros_like(act_ref)


def _moe_down_kernel(be_ref, nx_ref, nu_ref, act_ref, wd_hbm, y_ref, wst_ref, wbf_ref, sem, slot_ref):
    j = pl.program_id(0)
    i = pl.program_id(1)
    nu = nu_ref[0]
    tn = y_ref.shape[1]

    @pl.when(i < nu)
    def _():
        _stream_expert_weights([wd_hbm], pl.multiple_of(j * tn, tn), be_ref, nx_ref, nu, i,
                               wst_ref, wbf_ref, sem, slot_ref)
        y_ref[...] = jnp.dot(act_ref[...], wbf_ref[0], preferred_element_type=F32)

    @pl.when(i >= nu)
    def _():
        y_ref[...] = jnp.zeros_like(y_ref)


def moe_experts(xs, block_expert, next_expert_block, chunk_lo, chunk_hi, n_used, w_gate, w_up, w_down,
                blk=MOE_BLK, tn=MOE_TN):
    r, dw = xs.shape
    _, d, f = w_gate.shape
    n_blocks = r // blk
    tn = min(tn, d)
    kc = d // MOE_UP_CHUNKS

    def used(i, nu):
        return jnp.minimum(i, jnp.maximum(nu[0] - 1, 0))

    def stream_scratch(n_mats, k, n):
        return [pltpu.VMEM((2, n_mats, k, n), F32),
                pltpu.VMEM((n_mats, k, n), BF16),
                pltpu.SemaphoreType.DMA((2,)),
                pltpu.SMEM((1,), jnp.int32)]

    up_spec = pltpu.PrefetchScalarGridSpec(
        num_scalar_prefetch=5,
        grid=(n_blocks,),
        in_specs=[pl.BlockSpec((blk, dw), lambda i, be, nx, cs, ce, nu: (used(i, nu), 0)),
                  pl.BlockSpec(memory_space=pl.ANY),
                  pl.BlockSpec(memory_space=pl.ANY)],
        out_specs=pl.BlockSpec((blk, f), lambda i, be, nx, cs, ce, nu: (i, 0)),
        scratch_shapes=[pltpu.VMEM((2, 2, d, f), BF16),
                        pltpu.VMEM((MOE_UP_SLOTS, 2, kc, f), F32),
                        pltpu.SemaphoreType.DMA((MOE_UP_SLOTS,)),
                        pltpu.SMEM((1,), jnp.int32)],
    )
    act = pl.pallas_call(
        _moe_up_kernel,
        grid_spec=up_spec,
        out_shape=jax.ShapeDtypeStruct((r, f), BF16),
        compiler_params=_params(("arbitrary",)),
        name="moe_up",
    )(block_expert, next_expert_block, chunk_lo, chunk_hi, n_used, xs, w_gate, w_up)
    down_spec = pltpu.PrefetchScalarGridSpec(
        num_scalar_prefetch=3,
        grid=(d // tn, n_blocks),
        in_specs=[pl.BlockSpec((blk, f), lambda j, i, be, nx, nu: (used(i, nu), 0)),
                  pl.BlockSpec(memory_space=pl.ANY)],
        out_specs=pl.BlockSpec((blk, tn), lambda j, i, be, nx, nu: (i, j)),
        scratch_shapes=stream_scratch(1, f, tn),
    )
    return pl.pallas_call(
        _moe_down_kernel,
        grid_spec=down_spec,
        out_shape=jax.ShapeDtypeStruct((r, d), F32),
        compiler_params=_params(("arbitrary", "arbitrary")),
        name="moe_down",
    )(block_expert, next_expert_block, n_used, act, w_down)


def _combine_kernel(e0_ref, e1_ref, r0_ref, r1_ref, ps_ref, x_ref, route_ref, g_ref, y_hbm, o_ref,
                    ya_ref, yb_ref, sem, *, tb, final_norm):
    step = pl.program_id(0)

    def copies(b, slot, i):
        t = b * tb + i
        row0 = ps_ref[e0_ref[t]] + r0_ref[t]
        row1 = ps_ref[e1_ref[t]] + r1_ref[t]
        return (pltpu.make_async_copy(y_hbm.at[pl.ds(row0, 1), :],
                                      ya_ref.at[slot, pl.ds(i, 1), :], sem.at[slot]),
                pltpu.make_async_copy(y_hbm.at[pl.ds(row1, 1), :],
                                      yb_ref.at[slot, pl.ds(i, 1), :], sem.at[slot]))

    def for_rows(fn):
        def body(grp, carry):
            r0 = pl.multiple_of(grp * SUBLANES, SUBLANES)
            for k in range(SUBLANES):
                fn(r0 + k)
            return carry
        lax.fori_loop(0, tb // SUBLANES, body, 0)

    def start_block(b, slot):
        def start(i):
            c0, c1 = copies(b, slot, i)
            c0.start()
            c1.start()
        for_rows(start)

    def wait_block(b, slot):
        def wait(i):
            c0, c1 = copies(b, slot, i)
            c0.wait()
            c1.wait()
        for_rows(wait)

    @pl.when(step == 0)
    def _():
        start_block(0, 0)

    @pl.when(step + 1 < pl.num_programs(0))
    def _():
        start_block(step + 1, (step + 1) % 2)

    slot = step % 2
    wait_block(step, slot)
    w0 = route_ref[:, 2:3]
    w1 = route_ref[:, 3:4]
    x = x_ref[...] + (ya_ref[slot] * w0 + yb_ref[slot] * w1)
    if final_norm:
        ms = jnp.mean(x * x, axis=-1, keepdims=True)
        x = x * lax.rsqrt(ms + NORM_EPS) * g_ref[...]
    o_ref[...] = x


def combine(x, route, y, slots, g, final_norm, tb=COMB_ROWS):
    t, d = x.shape
    tb = min(tb, t)
    grid_spec = pltpu.PrefetchScalarGridSpec(
        num_scalar_prefetch=5,
        grid=(t // tb,),
        in_specs=[pl.BlockSpec((tb, d), lambda i, *_: (i, 0)),
                  pl.BlockSpec((tb, ROUTE_LANES), lambda i, *_: (i, 0)),
                  pl.BlockSpec((1, d), lambda i, *_: (0, 0)),
                  pl.BlockSpec(memory_space=pl.ANY)],
        out_specs=pl.BlockSpec((tb, d), lambda i, *_: (i, 0)),
        scratch_shapes=[pltpu.VMEM((2, tb, d), F32), pltpu.VMEM((2, tb, d), F32),
                        pltpu.SemaphoreType.DMA((2,))],
    )
    return pl.pallas_call(
        functools.partial(_combine_kernel, tb=tb, final_norm=final_norm),
        grid_spec=grid_spec,
        out_shape=jax.ShapeDtypeStruct((t, d), F32),
        compiler_params=_params(("arbitrary",)),
        name="moe_combine",
    )(*slots, x, route, g.reshape(1, d), y)


def _route_lanes(group_part, expert_part):
    rows = group_part.shape[0]
    gap = jnp.zeros((rows, EXPERT_LANE0 - GROUP_LANE0 - N_GROUPS), F32)
    tail = jnp.zeros((rows, ROUTE_LANES - EXPERT_LANE0 - N_EXPERTS), F32)
    return jnp.concatenate([group_part, gap, expert_part, tail], axis=1)


def _moe_layout(route, counts, blk):
    t = route.shape[0]
    ri = route[:, :8].astype(jnp.int32)
    e0, e1, rank0, rank1 = ri[:, 0], ri[:, 1], ri[:, 4], ri[:, 5]
    cnt = counts[0, EXPERT_LANE0:EXPERT_LANE0 + N_EXPERTS].astype(jnp.int32)
    padded = (cnt + blk - 1) // blk * blk
    pends = jnp.cumsum(padded)
    pstarts = pends - padded
    n_blocks = (2 * t) // blk + N_EXPERTS
    block_start = jnp.arange(n_blocks, dtype=jnp.int32) * blk
    block_expert = jnp.minimum(
        jnp.sum((block_start[:, None] >= pends[None, :]).astype(jnp.int32), axis=1), N_EXPERTS - 1)
    n_used = (pends[-1] // blk).astype(jnp.int32).reshape(1)
    block_expert = block_expert[jnp.minimum(jnp.arange(n_blocks), jnp.maximum(n_used[0] - 1, 0))]
    next_expert_block = pends[block_expert] // blk
    first_block = pstarts[block_expert] // blk
    k = jnp.arange(n_blocks, dtype=jnp.int32) - first_block
    n = jnp.maximum(next_expert_block - first_block, 1)
    chunk_lo = (MOE_UP_CHUNKS * k) // n
    chunk_hi = (MOE_UP_CHUNKS * (k + 1)) // n
    return ((e0, e1, rank0, rank1, pstarts), block_expert, next_expert_block, chunk_lo, chunk_hi,
            n_used, n_blocks)


def kernel(x, mem, positions, mix_norm_g, w_in, ret_norm_g, lru_conv_w, lru_conv_b, lru_w_a, lru_b_a, lru_w_i, lru_b_i, lru_lambda, lru_norm_g, w_out, xattn_norm_g, mem_norm_g, xattn_wq, xattn_wk, xattn_wv, xattn_wo, moe_norm_g, router_group_w, router_group_b, router_expert_w, router_expert_b, expert_w_gate, expert_w_up, expert_w_down, final_norm_g):
    b, s, d = x.shape
    depth = w_in.shape[0]
    ret_width = RET_HEADS * RET_HEAD_DIM
    lru_width = lru_conv_w.shape[-1]
    assert ret_width == lru_width and ret_width + lru_width == d
    inv_freq = ROPE_BASE ** (-jnp.arange(0, RET_HEAD_DIM, 2, dtype=F32) / RET_HEAD_DIM)
    lg = jnp.log1p(-jnp.exp2(-5.0 - jnp.arange(RET_HEADS, dtype=F32)))
    lg_rows = jnp.broadcast_to(lg[:, None, None], (RET_HEADS, 1, RET_HEAD_DIM))
    blk = min(MOE_BLK, s)
    outs = []
    for bi in range(b):
        xcur = x[bi]
        cos, sin = rope_tables(positions[bi].astype(F32), inv_freq)
        for l in range(depth):
            h = normcast(xcur, mix_norm_g[l], BF16, NORM_ROWS)
            proj = matmul_streamed([h], w_in[l], F32)
            ret = retention(proj, cos, sin, lg_rows, ret_norm_g[l])
            lru = rg_lru(proj, 4 * ret_width // lru_width, 4 * ret_width // lru_width + 1,
                         lru_conv_w[l], lru_conv_b[l], lru_w_a[l], lru_b_a[l], lru_w_i[l],
                         lru_b_i[l], lru_lambda[l], lru_norm_g[l])
            xcur = matmul_streamed([ret, lru], w_out[l], F32, res=xcur, tm=MM_STREAM_TM // 2)
            memn = normcast(mem[bi], mem_norm_g[l], BF16, NORM_ROWS)
            kk = matmul([memn], xattn_wk[l], BF16)
            vv = matmul([memn], xattn_wv[l], BF16)
            wqk, vo = xattn_fold(kk, vv, xattn_wq[l], xattn_wo[l])
            xcur = xattn(xcur, xattn_norm_g[l], wqk, vo)
            wr = _route_lanes(router_group_w[l], router_expert_w[l])
            br = _route_lanes(router_group_b[l][None], router_expert_b[l][None])
            hn, route, counts = router(xcur, moe_norm_g[l], wr, br)
            slots, block_expert, next_block, chunk_lo, chunk_hi, n_used, n_blocks = _moe_layout(
                route, counts, blk)
            xs = moe_gather(hn, slots, n_used, n_blocks, blk)
            y = moe_experts(xs, block_expert, next_block, chunk_lo, chunk_hi, n_used,
                            expert_w_gate[l], expert_w_up[l],
                            expert_w_down[l], blk=blk)
            xcur = combine(xcur, route, y, slots, final_norm_g, final_norm=l == depth - 1)
        outs.append(xcur)
    return outs[0][None] if b == 1 else jnp.stack(outs, axis=0)
```

```python
import functools

import jax
import jax.numpy as jnp
from jax import lax
from jax.experimental import pallas as pl
from jax.experimental.pallas import tpu as pltpu

F32 = jnp.float32
BF16 = jnp.bfloat16

RET_HEADS = 8
RET_HEAD_DIM = 256
RET_CHUNK = 128
LRU_BLOCKS = 8
CONV_WIDTH = 4
RG_C = 8.0
ROPE_BASE = 10000.0
X_HEADS = 4
N_GROUPS = 4
EXPERTS_PER_GROUP = 8
N_EXPERTS = N_GROUPS * EXPERTS_PER_GROUP
NORM_EPS = 1e-6
GN_EPS = 1e-5

LANES = 128
SUBLANES = 8
VMEM_LIMIT = 56 * 1024 * 1024

NORM_ROWS = 512
MM_TM = 1024
MM_TN = 512
MM_STREAM_TM = 1024
MM_STREAM_TN = 1024
RET_ROWS = 512
RET_HEADS_PER_STEP = 8
LRU_ROWS = 256
ATT_ROWS = 512
ROUTE_ROWS = 256
MOE_BLK = 256
MOE_TF = 512
MOE_TN = 4096
WEIGHT_DMA_SPLIT = 4
XF_TILE = 1024
SCALAR_UNROLL = 8
COMB_ROWS = 256
ROUTE_LANES = LANES
GROUP_LANE0 = 0
EXPERT_LANE0 = 8


def _params(sem):
    return pltpu.CompilerParams(dimension_semantics=sem, vmem_limit_bytes=VMEM_LIMIT)


def _normcast_kernel(x_ref, g_ref, o_ref):
    x = x_ref[...]
    ms = jnp.mean(x * x, axis=-1, keepdims=True)
    o_ref[...] = (x * lax.rsqrt(ms + NORM_EPS) * g_ref[...]).astype(o_ref.dtype)


def normcast(x, g, out_dtype, tm):
    m, d = x.shape
    tm = min(tm, m)
    return pl.pallas_call(
        _normcast_kernel,
        grid=(m // tm,),
        in_specs=[pl.BlockSpec((tm, d), lambda i: (i, 0)),
                  pl.BlockSpec((1, d), lambda i: (0, 0))],
        out_specs=pl.BlockSpec((tm, d), lambda i: (i, 0)),
        out_shape=jax.ShapeDtypeStruct((m, d), out_dtype),
        compiler_params=_params(("arbitrary",)),
        name="normcast",
    )(x, g.reshape(1, d))


def _cast_rows(src_ref, dst_ref, rows_per_iter=256):
    k = src_ref.shape[0]
    step = min(rows_per_iter, k)

    def body(i, carry):
        r0 = pl.multiple_of(i * step, step)
        dst_ref[pl.ds(r0, step), :] = src_ref[pl.ds(r0, step), :].astype(dst_ref.dtype)
        return carry

    lax.fori_loop(0, k // step, body, 0)


def _mm_kernel(*refs, n_a, has_res):
    a_refs = refs[:n_a]
    w_ref = refs[n_a]
    res_ref = refs[n_a + 1] if has_res else None
    o_ref = refs[n_a + 1 + int(has_res)]
    wbf_ref = refs[n_a + 2 + int(has_res)]

    @pl.when(pl.program_id(1) == 0)
    def _():
        _cast_rows(w_ref, wbf_ref)

    kp = a_refs[0].shape[1]
    acc = None
    for p, a_ref in enumerate(a_refs):
        d = jnp.dot(a_ref[...], wbf_ref[p * kp:(p + 1) * kp, :], preferred_element_type=F32)
        acc = d if acc is None else acc + d
    if has_res:
        acc = acc + res_ref[...]
    o_ref[...] = acc.astype(o_ref.dtype)


def matmul(a_parts, w, out_dtype, res=None, tm=MM_TM, tn=MM_TN):
    m, kp = a_parts[0].shape
    k, n = w.shape
    assert kp * len(a_parts) == k
    tm = min(tm, m)
    tn = min(tn, n)
    in_specs = [pl.BlockSpec((tm, kp), lambda j, i: (i, 0)) for _ in a_parts]
    in_specs.append(pl.BlockSpec((k, tn), lambda j, i: (0, j)))
    args = list(a_parts) + [w]
    if res is not None:
        in_specs.append(pl.BlockSpec((tm, tn), lambda j, i: (i, j)))
        args.append(res)
    return pl.pallas_call(
        functools.partial(_mm_kernel, n_a=len(a_parts), has_res=res is not None),
        grid=(n // tn, m // tm),
        in_specs=in_specs,
        out_specs=pl.BlockSpec((tm, tn), lambda j, i: (i, j)),
        out_shape=jax.ShapeDtypeStruct((m, n), out_dtype),
        scratch_shapes=[pltpu.VMEM((k, tn), BF16)],
        compiler_params=_params(("arbitrary", "arbitrary")),
        name="matmul",
    )(*args)


def _mm_stream_kernel(*refs, n_a, has_res):
    a_refs = refs[:n_a]
    w_hbm = refs[n_a]
    res_ref = refs[n_a + 1] if has_res else None
    o_ref = refs[n_a + 1 + int(has_res)]
    wbf_ref, stage_ref, sem = refs[n_a + 2 + int(has_res):]
    j = pl.program_id(0)
    i = pl.program_id(1)
    n_j = pl.num_programs(0)
    n_i = pl.num_programs(1)
    kc, tn = stage_ref.shape[1:]
    n_chunks = wbf_ref.shape[1] // kc

    def chunk_copy(col_tile, c, slot):
        rows = pl.ds(pl.multiple_of(c * kc, kc), kc)
        cols = pl.ds(pl.multiple_of(col_tile * tn, tn), tn)
        return pltpu.make_async_copy(w_hbm.at[rows, cols], stage_ref.at[slot], sem.at[slot])

    def cast_chunk(buf, c, slot):
        rows = pl.ds(pl.multiple_of(c * kc, kc), kc)
        wbf_ref[buf, rows, :] = stage_ref[slot].astype(BF16)

    @pl.when((j == 0) & (i == 0))
    def _():
        chunk_copy(0, 0, 0).start()
        for c in range(n_chunks):
            if c + 1 < n_chunks:
                chunk_copy(0, c + 1, (c + 1) % 2).start()
            chunk_copy(0, c, c % 2).wait()
            cast_chunk(0, c, c % 2)

    cur = j % 2
    kp = a_refs[0].shape[1]
    acc = None
    for p, a_ref in enumerate(a_refs):
        d = jnp.dot(a_ref[...], wbf_ref[cur, p * kp:(p + 1) * kp, :], preferred_element_type=F32)
        acc = d if acc is None else acc + d
    if has_res:
        acc = acc + res_ref[...]
    o_ref[...] = acc.astype(o_ref.dtype)

    @pl.when(j + 1 < n_j)
    def _():
        slot = i % 2

        @pl.when(i == 0)
        def _():
            chunk_copy(j + 1, 0, 0).start()

        chunk_copy(j + 1, i, slot).wait()

        @pl.when(i + 1 < n_i)
        def _():
            chunk_copy(j + 1, i + 1, 1 - slot).start()

        cast_chunk(1 - cur, i, slot)


def matmul_streamed(a_parts, w, out_dtype, res=None, tm=MM_STREAM_TM, tn=MM_STREAM_TN):
    m, kp = a_parts[0].shape
    k, n = w.shape
    assert kp * len(a_parts) == k
    tm = min(tm, m)
    tn = min(tn, n)
    n_i = m // tm
    kc = k // n_i
    assert kc * n_i == k and kc % SUBLANES == 0
    in_specs = [pl.BlockSpec((tm, kp), lambda j, i: (i, 0)) for _ in a_parts]
    in_specs.append(pl.BlockSpec(memory_space=pl.ANY))
    args = list(a_parts) + [w]
    if res is not None:
        in_specs.append(pl.BlockSpec((tm, tn), lambda j, i: (i, j)))
        args.append(res)
    return pl.pallas_call(
        functools.partial(_mm_stream_kernel, n_a=len(a_parts), has_res=res is not None),
        grid=(n // tn, n_i),
        in_specs=in_specs,
        out_specs=pl.BlockSpec((tm, tn), lambda j, i: (i, j)),
        out_shape=jax.ShapeDtypeStruct((m, n), out_dtype),
        scratch_shapes=[pltpu.VMEM((2, k, tn), BF16),
                        pltpu.VMEM((2, kc, tn), F32),
                        pltpu.SemaphoreType.DMA((2,))],
        compiler_params=_params(("arbitrary", "arbitrary")),
        name="matmul_streamed",
    )(*args)


def _rope_kernel(pos_ref, invf_ref, cos_ref, sin_ref):
    ang = pos_ref[...] * invf_ref[...]
    cos_ref[...] = jnp.cos(ang)
    sin_ref[...] = jnp.sin(ang)


def rope_tables(pos_f, inv_freq, tm=512):
    s = pos_f.shape[0]
    hd = inv_freq.shape[0]
    tm = min(tm, s)
    return pl.pallas_call(
        _rope_kernel,
        grid=(s // tm,),
        in_specs=[pl.BlockSpec((tm, 1), lambda i: (i, 0)),
                  pl.BlockSpec((1, hd), lambda i: (0, 0))],
        out_specs=[pl.BlockSpec((tm, hd), lambda i: (i, 0))] * 2,
        out_shape=[jax.ShapeDtypeStruct((s, hd), F32)] * 2,
        compiler_params=_params(("arbitrary",)),
        name="rope_tables",
    )(pos_f.reshape(s, 1), inv_freq.reshape(1, hd))


def _ret_kernel(q_ref, k_ref, v_ref, g_ref, cos_ref, sin_ref, lg_ref, gn_ref, o_ref, r_ref,
                decay_ref, xi_ref, zeta_ref, *, n_chunks, hpb):
    c = RET_CHUNK
    dk = RET_HEAD_DIM
    half = dk // 2
    scale = dk ** -0.5

    @pl.when(pl.program_id(1) == 0)
    def _():
        r_ref[...] = jnp.zeros_like(r_ref)
        row = lax.broadcasted_iota(jnp.int32, (c, c), 0).astype(F32)
        col = lax.broadcasted_iota(jnp.int32, (c, c), 1).astype(F32)
        diff = row - col
        rowk = lax.broadcasted_iota(jnp.int32, (c, dk), 0).astype(F32)
        for hh in range(hpb):
            lg = lg_ref[hh]
            decay_ref[hh] = jnp.where(diff >= 0, jnp.exp(lg[:, :c] * jnp.maximum(diff, 0.0)), 0.0)
            xi_ref[hh] = jnp.exp(lg * (rowk + 1.0))
            zeta_ref[hh] = jnp.exp(lg * (c - 1.0 - rowk))

    def rope(t, cos, sin):
        t1 = t[:, :half]
        t2 = t[:, half:]
        return jnp.concatenate([t1 * cos - t2 * sin, t1 * sin + t2 * cos], axis=-1)

    def body(j, carry):
        r0 = pl.multiple_of(j * c, c)
        rows = pl.ds(r0, c)
        cos = cos_ref[rows, :]
        sin = sin_ref[rows, :]
        for hh in range(hpb):
            cs = slice(hh * dk, (hh + 1) * dk)
            qr = rope(q_ref[rows, cs], cos, sin)
            kr = rope(k_ref[rows, cs], cos, sin) * scale
            qb = qr.astype(BF16)
            kb = kr.astype(BF16)
            vb = v_ref[rows, cs].astype(BF16)
            state = r_ref[hh]
            inner = lax.dot_general(qb, kb, (((1,), (1,)), ((), ())),
                                    preferred_element_type=F32) * decay_ref[hh]
            o = (jnp.dot(inner.astype(BF16), vb, preferred_element_type=F32)
                 + jnp.dot(qb, state.astype(BF16), preferred_element_type=F32) * xi_ref[hh])
            kz = (kr * zeta_ref[hh]).astype(BF16)
            chunk_decay = jnp.exp(lg_ref[hh] * c)
            r_ref[hh] = state * chunk_decay + lax.dot_general(
                kz, vb, (((0,), (0,)), ((), ())), preferred_element_type=F32)
            mu = jnp.mean(o, axis=-1, keepdims=True)
            oc = o - mu
            var = jnp.mean(oc * oc, axis=-1, keepdims=True)
            on = oc * lax.rsqrt(var + GN_EPS) * gn_ref[hh]
            g = g_ref[rows, cs]
            o_ref[rows, cs] = (on * (g * (1.0 / (1.0 + jnp.exp(-g))))).astype(o_ref.dtype)
        return carry

    lax.fori_loop(0, n_chunks, body, 0, unroll=2)


def retention(proj, cos, sin, lg_rows, gn_g, tr=RET_ROWS, hpb=RET_HEADS_PER_STEP):
    s = proj.shape[0]
    dk = RET_HEAD_DIM
    h = RET_HEADS
    tr = min(tr, s)
    c = RET_CHUNK
    w = hpb * dk

    def col(base):
        return pl.BlockSpec((tr, w), lambda hg, ci, base=base: (ci, base // hpb + hg))

    per_head = pl.BlockSpec((hpb, 1, dk), lambda hg, ci: (hg, 0, 0))
    return pl.pallas_call(
        functools.partial(_ret_kernel, n_chunks=tr // c, hpb=hpb),
        grid=(h // hpb, s // tr),
        in_specs=[col(0), col(h), col(2 * h), col(3 * h),
                  pl.BlockSpec((tr, dk // 2), lambda hg, ci: (ci, 0)),
                  pl.BlockSpec((tr, dk // 2), lambda hg, ci: (ci, 0)),
                  per_head, per_head],
        out_specs=pl.BlockSpec((tr, w), lambda hg, ci: (ci, hg)),
        out_shape=jax.ShapeDtypeStruct((s, h * dk), BF16),
        scratch_shapes=[pltpu.VMEM((hpb, dk, dk), F32),
                        pltpu.VMEM((hpb, c, c), F32),
                        pltpu.VMEM((hpb, c, dk), F32),
                        pltpu.VMEM((hpb, c, dk), F32)],
        compiler_params=_params(("arbitrary", "arbitrary")),
        name="retention",
    )(proj, proj, proj, proj, cos, sin, lg_rows, gn_g.reshape(h, 1, dk))


def _sigmoid(x):
    return 1.0 / (1.0 + jnp.exp(-x))


def _lru_kernel(xb_ref, gb_ref, cw_ref, cb_ref, wa_ref, ba_ref, wi_ref, bi_ref, lam_ref, og_ref,
                o_ref, tail_ref, xs_ref, hs_ref, h_ref, wabf_ref, wibf_ref):
    tr, cdim = xb_ref.shape
    nb = wa_ref.shape[0]
    bd = cdim // nb
    ph = SUBLANES
    ng = tr // ph

    @pl.when(pl.program_id(0) == 0)
    def _():
        tail_ref[...] = jnp.zeros_like(tail_ref)
        h_ref[...] = jnp.zeros_like(h_ref)
        wabf_ref[...] = wa_ref[...].astype(BF16)
        wibf_ref[...] = wi_ref[...].astype(BF16)

    lam = lam_ref[...]
    sp = jnp.maximum(-lam, 0.0) + jnp.log1p(jnp.exp(-jnp.abs(lam)))
    rowg = lax.broadcasted_iota(jnp.int32, (ng, bd), 0)
    lpb = bd // LANES
    for c in range(cdim // LANES):
        xs_ref[c] = xb_ref[:, c * LANES:(c + 1) * LANES]

    def phase_rows(ref, n, p):
        return jnp.concatenate([ref[n * lpb + c, pl.ds(p, ng, stride=ph), :] for c in range(lpb)],
                               axis=1)

    for n in range(nb):
        cs = slice(n * bd, (n + 1) * bd)
        x = [phase_rows(xs_ref, n, p) for p in range(ph)]

        def prev_group(p):
            return jnp.where(rowg == 0, tail_ref[p:p + 1, cs], pltpu.roll(x[p], 1, 0))

        back = {-k: prev_group(ph - k) for k in range(1, CONV_WIDTH)}

        def xat(p):
            return x[p] if p >= 0 else back[p]

        xc = []
        for p in range(ph):
            acc = cb_ref[:, cs] + cw_ref[CONV_WIDTH - 1:CONV_WIDTH, cs] * xat(p)
            for k in range(1, CONV_WIDTH):
                acc = acc + cw_ref[CONV_WIDTH - 1 - k:CONV_WIDTH - k, cs] * xat(p - k)
            xc.append(acc)
        xg = jnp.concatenate(xc, axis=0)
        xgb = xg.astype(BF16)
        r = _sigmoid(jnp.dot(xgb, wabf_ref[n], preferred_element_type=F32) + ba_ref[:, cs])
        ig = _sigmoid(jnp.dot(xgb, wibf_ref[n], preferred_element_type=F32) + bi_ref[:, cs])
        log_a = (-RG_C * r) * sp[:, cs]
        a = jnp.exp(log_a)
        b = jnp.sqrt(-jnp.tanh(log_a) * (a * a + 1.0)) * (ig * xg)

        cum_a = [a[0:ng]]
        cum_b = [b[0:ng]]
        for p in range(1, ph):
            ap = a[p * ng:(p + 1) * ng]
            cum_b.append(ap * cum_b[-1] + b[p * ng:(p + 1) * ng])
            cum_a.append(ap * cum_a[-1])
        sa, sb = cum_a[-1], cum_b[-1]
        d = 1
        while d < ng:
            keep = rowg >= d
            sa_sh = pltpu.roll(sa, d, 0)
            sb_sh = pltpu.roll(sb, d, 0)
            sb = jnp.where(keep, sa * sb_sh + sb, sb)
            sa = jnp.where(keep, sa * sa_sh, sa)
            d *= 2
        h_in = h_ref[:, cs]
        h_end = sa * h_in + sb
        h_prev = jnp.where(rowg == 0, h_in, pltpu.roll(h_end, 1, 0))
        for p in range(ph):
            hp = cum_a[p] * h_prev + cum_b[p]
            for c in range(lpb):
                hs_ref[n * lpb + c, pl.ds(p, ng, stride=ph), :] = hp[:, c * LANES:(c + 1) * LANES]
        h_ref[:, cs] = h_end[ng - 1:ng, :]

    gb = gb_ref[...]
    gelu = 0.5 * gb * (1.0 + jnp.tanh(0.7978845608028654 * (gb + 0.044715 * (gb * gb * gb))))
    y = jnp.concatenate([hs_ref[c] for c in range(cdim // LANES)], axis=1) * gelu
    ms = jnp.mean(y * y, axis=-1, keepdims=True)
    o_ref[...] = (y * lax.rsqrt(ms + NORM_EPS) * og_ref[...]).astype(o_ref.dtype)
    tail_ref[...] = xb_ref[tr - ph:tr, :]


def rg_lru(proj, xb_block, gb_block, conv_w, conv_b, w_a, b_a, w_i, b_i, lam, out_g, tr=LRU_ROWS):
    s = proj.shape[0]
    cdim = conv_w.shape[1]
    nb, bd, _ = w_a.shape
    tr = min(tr, s)
    vec = pl.BlockSpec((1, cdim), lambda i: (0, 0))
    wspec = pl.BlockSpec((nb, bd, bd), lambda i: (0, 0, 0))
    return pl.pallas_call(
        _lru_kernel,
        grid=(s // tr,),
        in_specs=[pl.BlockSpec((tr, cdim), lambda i: (i, xb_block)),
                  pl.BlockSpec((tr, cdim), lambda i: (i, gb_block)),
                  pl.BlockSpec((CONV_WIDTH, cdim), lambda i: (0, 0)),
                  vec, wspec, vec, wspec, vec, vec, vec],
        out_specs=pl.BlockSpec((tr, cdim), lambda i: (i, 0)),
        out_shape=jax.ShapeDtypeStruct((s, cdim), BF16),
        scratch_shapes=[pltpu.VMEM((SUBLANES, cdim), F32),
                        pltpu.VMEM((cdim // LANES, tr, LANES), F32),
                        pltpu.VMEM((cdim // LANES, tr, LANES), F32),
                        pltpu.VMEM((1, cdim), F32),
                        pltpu.VMEM((nb, bd, bd), BF16),
                        pltpu.VMEM((nb, bd, bd), BF16)],
        compiler_params=_params(("arbitrary",)),
        name="rg_lru",
    )(proj, proj, conv_w, conv_b.reshape(1, cdim), w_a, b_a.reshape(1, cdim), w_i,
      b_i.reshape(1, cdim), lam.reshape(1, cdim), out_g.reshape(1, cdim))


def _wqk_kernel(wq_ref, k_ref, o_ref):
    o_ref[...] = lax.dot_general(wq_ref[...].astype(BF16), k_ref[...], (((1,), (1,)), ((), ())),
                                 preferred_element_type=F32).astype(o_ref.dtype)


def _vo_kernel(v_ref, wo_ref, o_ref):
    o_ref[...] = jnp.dot(v_ref[...], wo_ref[...].astype(BF16),
                         preferred_element_type=F32).astype(o_ref.dtype)


def xattn_fold(k, v, wq, wo, tile=XF_TILE):
    mlen, d = k.shape
    hd = d // X_HEADS
    tile = min(tile, d)
    wqk = pl.pallas_call(
        _wqk_kernel,
        grid=(X_HEADS, d // tile),
        in_specs=[pl.BlockSpec((tile, hd), lambda h, r: (r, h)),
                  pl.BlockSpec((mlen, hd), lambda h, r: (0, h))],
        out_specs=pl.BlockSpec((tile, mlen), lambda h, r: (r, h)),
        out_shape=jax.ShapeDtypeStruct((d, X_HEADS * mlen), BF16),
        compiler_params=_params(("arbitrary", "arbitrary")),
        name="xattn_wqk",
    )(wq, k)
    vo = pl.pallas_call(
        _vo_kernel,
        grid=(X_HEADS, d // tile),
        in_specs=[pl.BlockSpec((mlen, hd), lambda h, j: (0, h)),
                  pl.BlockSpec((hd, tile), lambda h, j: (h, j))],
        out_specs=pl.BlockSpec((mlen, tile), lambda h, j: (h, j)),
        out_shape=jax.ShapeDtypeStruct((X_HEADS * mlen, d), BF16),
        compiler_params=_params(("arbitrary", "arbitrary")),
        name="xattn_vo",
    )(v, wo)
    return wqk, vo


def _xattn_kernel(x_ref, g_ref, wqk_ref, vo_ref, o_ref):
    d = x_ref.shape[1]
    mlen = wqk_ref.shape[1] // X_HEADS
    scale = (d // X_HEADS) ** -0.5
    x = x_ref[...]
    ms = jnp.mean(x * x, axis=-1, keepdims=True)
    h = (x * lax.rsqrt(ms + NORM_EPS) * g_ref[...]).astype(BF16)
    s = jnp.dot(h, wqk_ref[...], preferred_element_type=F32) * scale
    ps = []
    for hh in range(X_HEADS):
        sh = s[:, hh * mlen:(hh + 1) * mlen]
        m = jnp.max(sh, axis=-1, keepdims=True)
        e = jnp.exp(sh - m)
        ps.append((e / jnp.sum(e, axis=-1, keepdims=True)).astype(BF16))
    p = jnp.concatenate(ps, axis=1)
    o_ref[...] = x + jnp.dot(p, vo_ref[...], preferred_element_type=F32)


def xattn(x, g, wqk, vo, tm=ATT_ROWS):
    s, d = x.shape
    tm = min(tm, s)
    return pl.pallas_call(
        _xattn_kernel,
        grid=(s // tm,),
        in_specs=[pl.BlockSpec((tm, d), lambda i: (i, 0)),
                  pl.BlockSpec((1, d), lambda i: (0, 0)),
                  pl.BlockSpec(wqk.shape, lambda i: (0, 0), pipeline_mode=pl.Buffered(1)),
                  pl.BlockSpec(vo.shape, lambda i: (0, 0), pipeline_mode=pl.Buffered(1))],
        out_specs=pl.BlockSpec((tm, d), lambda i: (i, 0)),
        out_shape=jax.ShapeDtypeStruct((s, d), F32),
        compiler_params=_params(("arbitrary",)),
        name="xattn",
    )(x, g.reshape(1, d), wqk, vo)


def _split_bf16(a):
    hi = a.astype(BF16)
    return hi, (a - hi.astype(F32)).astype(BF16)


def _router_kernel(x_ref, g_ref, wr_ref, br_ref, h_ref, route_ref, counts_ref, carry_ref,
                   wsplit_ref):
    tm = x_ref.shape[0]
    nl = ROUTE_LANES

    @pl.when(pl.program_id(0) == 0)
    def _():
        carry_ref[...] = jnp.zeros_like(carry_ref)
        w_hi, w_lo = _split_bf16(wr_ref[...])
        wsplit_ref[:, :nl] = w_hi
        wsplit_ref[:, nl:] = w_lo

    x = x_ref[...]
    ms = jnp.mean(x * x, axis=-1, keepdims=True)
    h = x * lax.rsqrt(ms + NORM_EPS) * g_ref[...]
    h_ref[...] = h
    h_hi, h_lo = _split_bf16(h)
    both = jnp.dot(h_hi, wsplit_ref[...], preferred_element_type=F32)
    cross = jnp.dot(h_lo, wsplit_ref[:, :nl], preferred_element_type=F32)
    logits = both[:, :nl] + (both[:, nl:] + cross) + br_ref[...]
    lane = lax.broadcasted_iota(jnp.int32, (tm, ROUTE_LANES), 1).astype(F32)
    neg = -jnp.inf
    big = float(ROUTE_LANES)

    gmask = (lane >= GROUP_LANE0) & (lane < GROUP_LANE0 + N_GROUPS)
    gl = jnp.where(gmask, logits, neg)
    gmax = jnp.max(gl, axis=-1, keepdims=True)
    gsum = jnp.sum(jnp.where(gmask, jnp.exp(gl - gmax), 0.0), axis=-1, keepdims=True)
    g_val = 1.0 / gsum
    g_idx = jnp.min(jnp.where(gl == gmax, lane, big), axis=-1, keepdims=True) - GROUP_LANE0

    lo = EXPERT_LANE0 + g_idx * EXPERTS_PER_GROUP
    emask = (lane >= lo) & (lane < lo + EXPERTS_PER_GROUP)
    el = jnp.where(emask, logits, neg)
    t1 = jnp.max(el, axis=-1, keepdims=True)
    i1 = jnp.min(jnp.where(emask & (el == t1), lane, big), axis=-1, keepdims=True)
    emask2 = emask & (lane != i1)
    el2 = jnp.where(emask2, logits, neg)
    t2 = jnp.max(el2, axis=-1, keepdims=True)
    i2 = jnp.min(jnp.where(emask2 & (el2 == t2), lane, big), axis=-1, keepdims=True)
    dexp = jnp.exp(t2 - t1)
    w0 = g_val / (1.0 + dexp)
    w1 = g_val * dexp / (1.0 + dexp)

    sel1 = lane == i1
    sel2 = lane == i2
    onehot = jnp.where(sel1 | sel2, 1.0, 0.0)
    rr = lax.broadcasted_iota(jnp.int32, (tm, tm), 0)
    cc = lax.broadcasted_iota(jnp.int32, (tm, tm), 1)
    tri = jnp.where(cc < rr, 1.0, 0.0).astype(BF16)
    prefix = jnp.dot(tri, onehot.astype(BF16), preferred_element_type=F32) + carry_ref[...]
    rank0 = jnp.sum(jnp.where(sel1, prefix, 0.0), axis=-1, keepdims=True)
    rank1 = jnp.sum(jnp.where(sel2, prefix, 0.0), axis=-1, keepdims=True)
    total = carry_ref[...] + jnp.sum(onehot, axis=0, keepdims=True)
    carry_ref[...] = total
    counts_ref[...] = total

    e0 = i1 - EXPERT_LANE0
    e1 = i2 - EXPERT_LANE0
    route = jnp.where(lane == 0, e0, 0.0)
    route = jnp.where(lane == 1, e1, route)
    route = jnp.where(lane == 2, w0, route)
    route = jnp.where(lane == 3, w1, route)
    route = jnp.where(lane == 4, rank0, route)
    route = jnp.where(lane == 5, rank1, route)
    route_ref[...] = route


def router(x, g, wr, br, tm=ROUTE_ROWS):
    t, d = x.shape
    tm = min(tm, t)
    return pl.pallas_call(
        _router_kernel,
        grid=(t // tm,),
        in_specs=[pl.BlockSpec((tm, d), lambda i: (i, 0)),
                  pl.BlockSpec((1, d), lambda i: (0, 0)),
                  pl.BlockSpec((d, ROUTE_LANES), lambda i: (0, 0)),
                  pl.BlockSpec((1, ROUTE_LANES), lambda i: (0, 0))],
        out_specs=[pl.BlockSpec((tm, d), lambda i: (i, 0)),
                   pl.BlockSpec((tm, ROUTE_LANES), lambda i: (i, 0)),
                   pl.BlockSpec((1, ROUTE_LANES), lambda i: (0, 0))],
        out_shape=[jax.ShapeDtypeStruct((t, d), F32),
                   jax.ShapeDtypeStruct((t, ROUTE_LANES), F32),
                   jax.ShapeDtypeStruct((1, ROUTE_LANES), F32)],
        scratch_shapes=[pltpu.VMEM((1, ROUTE_LANES), F32),
                        pltpu.VMEM((d, 2 * ROUTE_LANES), BF16)],
        compiler_params=_params(("arbitrary",)),
        name="router",
    )(x, g.reshape(1, d), wr, br)


def _gather_kernel(e0_ref, e1_ref, r0_ref, r1_ref, ps_ref, bg_ref, nu_ref, x_hbm, xs_ref,
                   rowtok_ref, buf_ref, sem, *, blk, n_tok):
    i = pl.program_id(0)
    nu = nu_ref[0]

    def row_copy(b, slot, r):
        tok = rowtok_ref[b * blk + r]
        return pltpu.make_async_copy(x_hbm.at[pl.ds(tok, 1), :],
                                     buf_ref.at[slot, pl.ds(r, 1), :], sem.at[slot])

    def for_rows(b, fn):
        def body(grp, carry):
            r0 = pl.multiple_of(grp * SUBLANES, SUBLANES)
            for k in range(SUBLANES):
                fn(r0 + k)
            return carry
        lax.fori_loop(0, bg_ref[b], body, 0)

    def start_block(b, slot):
        for_rows(b, lambda r: row_copy(b, slot, r).start())

    def wait_block(b, slot):
        for_rows(b, lambda r: row_copy(b, slot, r).wait())

    @pl.when(i == 0)
    def _():
        n_rows = rowtok_ref.shape[0]
        for base in range(0, n_rows, n_tok):
            def init(r, carry, base=base):
                rowtok_ref[base + r] = r
                return carry
            lax.fori_loop(0, min(n_tok, n_rows - base), init, 0, unroll=SCALAR_UNROLL)

        def fill(t, carry):
            rowtok_ref[ps_ref[e0_ref[t]] + r0_ref[t]] = t
            rowtok_ref[ps_ref[e1_ref[t]] + r1_ref[t]] = t
            return carry
        lax.fori_loop(0, n_tok, fill, 0, unroll=SCALAR_UNROLL)
        buf_ref[...] = jnp.zeros_like(buf_ref)
        start_block(0, 0)

    @pl.when(i + 1 < nu)
    def _():
        start_block(i + 1, (i + 1) % 2)

    @pl.when(i < nu)
    def _():
        slot = i % 2
        wait_block(i, slot)
        xs_ref[...] = buf_ref[slot].astype(xs_ref.dtype)

    @pl.when(i >= nu)
    def _():
        xs_ref[...] = jnp.zeros_like(xs_ref)


def moe_gather(hn, slots, block_tiles, n_used, n_blocks, blk):
    t, d = hn.shape
    grid_spec = pltpu.PrefetchScalarGridSpec(
        num_scalar_prefetch=7,
        grid=(n_blocks,),
        in_specs=[pl.BlockSpec(memory_space=pl.ANY)],
        out_specs=pl.BlockSpec((blk, d), lambda i, *_: (i, 0)),
        scratch_shapes=[pltpu.SMEM((n_blocks * blk,), jnp.int32),
                        pltpu.VMEM((2, blk, d), F32),
                        pltpu.SemaphoreType.DMA((2,))],
    )
    return pl.pallas_call(
        functools.partial(_gather_kernel, blk=blk, n_tok=t),
        grid_spec=grid_spec,
        out_shape=jax.ShapeDtypeStruct((n_blocks * blk, d), BF16),
        compiler_params=_params(("arbitrary",)),
        name="moe_gather",
    )(*slots, block_tiles, n_used, hn)


def _expert_changed(be_ref, i):
    prev = be_ref[jnp.maximum(i - 1, 0)]
    return (i == 0) | (be_ref[i] != prev)


def _stream_expert_weights(w_hbms, col0, be_ref, nx_ref, nu, i, wst_ref, wbf_ref, sem, slot_ref):
    k, tn = wst_ref.shape[-2:]
    kq = k // WEIGHT_DMA_SPLIT

    def copies(e, slot):
        return [pltpu.make_async_copy(w.at[e, pl.ds(q * kq, kq), pl.ds(col0, tn)],
                                      wst_ref.at[slot, l, pl.ds(q * kq, kq), :], sem.at[slot])
                for l, w in enumerate(w_hbms) for q in range(WEIGHT_DMA_SPLIT)]

    @pl.when(i == 0)
    def _():
        slot_ref[0] = 0
        for c in copies(be_ref[0], 0):
            c.start()

    @pl.when(_expert_changed(be_ref, i))
    def _():
        slot = slot_ref[0]
        for c in copies(be_ref[i], slot):
            c.wait()
        nxt = nx_ref[i]

        @pl.when(nxt < nu)
        def _():
            for c in copies(be_ref[jnp.minimum(nxt, be_ref.shape[0] - 1)], 1 - slot):
                c.start()

        for l in range(len(w_hbms)):
            _cast_rows(wst_ref.at[slot, l], wbf_ref.at[l])
        slot_ref[0] = 1 - slot


def _moe_up_kernel(be_ref, nx_ref, nu_ref, xs_ref, wg_hbm, wu_hbm, act_ref,
                   wst_ref, wbf_ref, sem, slot_ref):
    j = pl.program_id(0)
    i = pl.program_id(1)
    nu = nu_ref[0]
    tf = act_ref.shape[1]

    @pl.when(i < nu)
    def _():
        _stream_expert_weights([wg_hbm, wu_hbm], pl.multiple_of(j * tf, tf), be_ref, nx_ref, nu, i,
                               wst_ref, wbf_ref, sem, slot_ref)
        x = xs_ref[...]
        gate = jnp.dot(x, wbf_ref[0], preferred_element_type=F32)
        up = jnp.dot(x, wbf_ref[1], preferred_element_type=F32)
        act_ref[...] = (gate * _sigmoid(gate) * up).astype(act_ref.dtype)

    @pl.when(i >= nu)
    def _():
        act_ref[...] = jnp.zeros_like(act_ref)


def _moe_down_kernel(be_ref, nx_ref, nu_ref, act_ref, wd_hbm, y_ref, wst_ref, wbf_ref, sem, slot_ref):
    j = pl.program_id(0)
    i = pl.program_id(1)
    nu = nu_ref[0]
    tn = y_ref.shape[1]

    @pl.when(i < nu)
    def _():
        _stream_expert_weights([wd_hbm], pl.multiple_of(j * tn, tn), be_ref, nx_ref, nu, i,
                               wst_ref, wbf_ref, sem, slot_ref)
        y_ref[...] = jnp.dot(act_ref[...], wbf_ref[0], preferred_element_type=F32)

    @pl.when(i >= nu)
    def _():
        y_ref[...] = jnp.zeros_like(y_ref)


def moe_experts(xs, block_expert, next_expert_block, n_used, w_gate, w_up, w_down,
                blk=MOE_BLK, tf=MOE_TF, tn=MOE_TN):
    r, dw = xs.shape
    _, d, f = w_gate.shape
    n_blocks = r // blk
    tf = min(tf, f)
    tn = min(tn, d)

    def used(i, nu):
        return jnp.minimum(i, jnp.maximum(nu[0] - 1, 0))

    def stream_scratch(n_mats, k, n):
        return [pltpu.VMEM((2, n_mats, k, n), F32),
                pltpu.VMEM((n_mats, k, n), BF16),
                pltpu.SemaphoreType.DMA((2,)),
                pltpu.SMEM((1,), jnp.int32)]

    up_spec = pltpu.PrefetchScalarGridSpec(
        num_scalar_prefetch=3,
        grid=(f // tf, n_blocks),
        in_specs=[pl.BlockSpec((blk, dw), lambda j, i, be, nx, nu: (used(i, nu), 0)),
                  pl.BlockSpec(memory_space=pl.ANY),
                  pl.BlockSpec(memory_space=pl.ANY)],
        out_specs=pl.BlockSpec((blk, tf), lambda j, i, be, nx, nu: (i, j)),
        scratch_shapes=stream_scratch(2, d, tf),
    )
    act = pl.pallas_call(
        _moe_up_kernel,
        grid_spec=up_spec,
        out_shape=jax.ShapeDtypeStruct((r, f), BF16),
        compiler_params=_params(("arbitrary", "arbitrary")),
        name="moe_up",
    )(block_expert, next_expert_block, n_used, xs, w_gate, w_up)
    down_spec = pltpu.PrefetchScalarGridSpec(
        num_scalar_prefetch=3,
        grid=(d // tn, n_blocks),
        in_specs=[pl.BlockSpec((blk, f), lambda j, i, be, nx, nu: (used(i, nu), 0)),
                  pl.BlockSpec(memory_space=pl.ANY)],
        out_specs=pl.BlockSpec((blk, tn), lambda j, i, be, nx, nu: (i, j)),
        scratch_shapes=stream_scratch(1, f, tn),
    )
    return pl.pallas_call(
        _moe_down_kernel,
        grid_spec=down_spec,
        out_shape=jax.ShapeDtypeStruct((r, d), F32),
        compiler_params=_params(("arbitrary", "arbitrary")),
        name="moe_down",
    )(block_expert, next_expert_block, n_used, act, w_down)


def _combine_kernel(e0_ref, e1_ref, r0_ref, r1_ref, ps_ref, x_ref, route_ref, g_ref, y_hbm, o_ref,
                    ya_ref, yb_ref, sem, *, tb, final_norm):
    step = pl.program_id(0)

    def copies(b, slot, i):
        t = b * tb + i
        row0 = ps_ref[e0_ref[t]] + r0_ref[t]
        row1 = ps_ref[e1_ref[t]] + r1_ref[t]
        return (pltpu.make_async_copy(y_hbm.at[pl.ds(row0, 1), :],
                                      ya_ref.at[slot, pl.ds(i, 1), :], sem.at[slot]),
                pltpu.make_async_copy(y_hbm.at[pl.ds(row1, 1), :],
                                      yb_ref.at[slot, pl.ds(i, 1), :], sem.at[slot]))

    def for_rows(fn):
        def body(grp, carry):
            r0 = pl.multiple_of(grp * SUBLANES, SUBLANES)
            for k in range(SUBLANES):
                fn(r0 + k)
            return carry
        lax.fori_loop(0, tb // SUBLANES, body, 0)

    def start_block(b, slot):
        def start(i):
            c0, c1 = copies(b, slot, i)
            c0.start()
            c1.start()
        for_rows(start)

    def wait_block(b, slot):
        def wait(i):
            c0, c1 = copies(b, slot, i)
            c0.wait()
            c1.wait()
        for_rows(wait)

    @pl.when(step == 0)
    def _():
        start_block(0, 0)

    @pl.when(step + 1 < pl.num_programs(0))
    def _():
        start_block(step + 1, (step + 1) % 2)

    slot = step % 2
    wait_block(step, slot)
    w0 = route_ref[:, 2:3]
    w1 = route_ref[:, 3:4]
    x = x_ref[...] + (ya_ref[slot] * w0 + yb_ref[slot] * w1)
    if final_norm:
        ms = jnp.mean(x * x, axis=-1, keepdims=True)
        x = x * lax.rsqrt(ms + NORM_EPS) * g_ref[...]
    o_ref[...] = x


def combine(x, route, y, slots, g, final_norm, tb=COMB_ROWS):
    t, d = x.shape
    tb = min(tb, t)
    grid_spec = pltpu.PrefetchScalarGridSpec(
        num_scalar_prefetch=5,
        grid=(t // tb,),
        in_specs=[pl.BlockSpec((tb, d), lambda i, *_: (i, 0)),
                  pl.BlockSpec((tb, ROUTE_LANES), lambda i, *_: (i, 0)),
                  pl.BlockSpec((1, d), lambda i, *_: (0, 0)),
                  pl.BlockSpec(memory_space=pl.ANY)],
        out_specs=pl.BlockSpec((tb, d), lambda i, *_: (i, 0)),
        scratch_shapes=[pltpu.VMEM((2, tb, d), F32), pltpu.VMEM((2, tb, d), F32),
                        pltpu.SemaphoreType.DMA((2,))],
    )
    return pl.pallas_call(
        functools.partial(_combine_kernel, tb=tb, final_norm=final_norm),
        grid_spec=grid_spec,
        out_shape=jax.ShapeDtypeStruct((t, d), F32),
        compiler_params=_params(("arbitrary",)),
        name="moe_combine",
    )(*slots, x, route, g.reshape(1, d), y)


def _route_lanes(group_part, expert_part):
    rows = group_part.shape[0]
    gap = jnp.zeros((rows, EXPERT_LANE0 - GROUP_LANE0 - N_GROUPS), F32)
    tail = jnp.zeros((rows, ROUTE_LANES - EXPERT_LANE0 - N_EXPERTS), F32)
    return jnp.concatenate([group_part, gap, expert_part, tail], axis=1)


def _moe_layout(route, counts, blk):
    t = route.shape[0]
    ri = route[:, :8].astype(jnp.int32)
    e0, e1, rank0, rank1 = ri[:, 0], ri[:, 1], ri[:, 4], ri[:, 5]
    cnt = counts[0, EXPERT_LANE0:EXPERT_LANE0 + N_EXPERTS].astype(jnp.int32)
    padded = (cnt + blk - 1) // blk * blk
    pends = jnp.cumsum(padded)
    pstarts = pends - padded
    n_blocks = (2 * t) // blk + N_EXPERTS
    block_start = jnp.arange(n_blocks, dtype=jnp.int32) * blk
    block_expert = jnp.minimum(
        jnp.sum((block_start[:, None] >= pends[None, :]).astype(jnp.int32), axis=1), N_EXPERTS - 1)
    n_used = (pends[-1] // blk).astype(jnp.int32).reshape(1)
    block_expert = block_expert[jnp.minimum(jnp.arange(n_blocks), jnp.maximum(n_used[0] - 1, 0))]
    next_expert_block = pends[block_expert] // blk
    first_block = pstarts[block_expert] // blk
    rows_left = cnt[block_expert] - (jnp.arange(n_blocks, dtype=jnp.int32) - first_block) * blk
    block_tiles = (jnp.clip(rows_left, 0, blk) + SUBLANES - 1) // SUBLANES
    return ((e0, e1, rank0, rank1, pstarts), block_expert, next_expert_block, block_tiles,
            n_used, n_blocks)


def kernel(x, mem, positions, mix_norm_g, w_in, ret_norm_g, lru_conv_w, lru_conv_b, lru_w_a, lru_b_a, lru_w_i, lru_b_i, lru_lambda, lru_norm_g, w_out, xattn_norm_g, mem_norm_g, xattn_wq, xattn_wk, xattn_wv, xattn_wo, moe_norm_g, router_group_w, router_group_b, router_expert_w, router_expert_b, expert_w_gate, expert_w_up, expert_w_down, final_norm_g):
    b, s, d = x.shape
    depth = w_in.shape[0]
    ret_width = RET_HEADS * RET_HEAD_DIM
    lru_width = lru_conv_w.shape[-1]
    assert ret_width == lru_width and ret_width + lru_width == d
    inv_freq = ROPE_BASE ** (-jnp.arange(0, RET_HEAD_DIM, 2, dtype=F32) / RET_HEAD_DIM)
    lg = jnp.log1p(-jnp.exp2(-5.0 - jnp.arange(RET_HEADS, dtype=F32)))
    lg_rows = jnp.broadcast_to(lg[:, None, None], (RET_HEADS, 1, RET_HEAD_DIM))
    blk = min(MOE_BLK, s)
    outs = []
    for bi in range(b):
        xcur = x[bi]
        cos, sin = rope_tables(positions[bi].astype(F32), inv_freq)
        for l in range(depth):
            h = normcast(xcur, mix_norm_g[l], BF16, NORM_ROWS)
            proj = matmul_streamed([h], w_in[l], F32)
            ret = retention(proj, cos, sin, lg_rows, ret_norm_g[l])
            lru = rg_lru(proj, 4 * ret_width // lru_width, 4 * ret_width // lru_width + 1,
                         lru_conv_w[l], lru_conv_b[l], lru_w_a[l], lru_b_a[l], lru_w_i[l],
                         lru_b_i[l], lru_lambda[l], lru_norm_g[l])
            xcur = matmul_streamed([ret, lru], w_out[l], F32, res=xcur, tm=MM_STREAM_TM // 2)
            memn = normcast(mem[bi], mem_norm_g[l], BF16, NORM_ROWS)
            kk = matmul([memn], xattn_wk[l], BF16)
            vv = matmul([memn], xattn_wv[l], BF16)
            wqk, vo = xattn_fold(kk, vv, xattn_wq[l], xattn_wo[l])
            xcur = xattn(xcur, xattn_norm_g[l], wqk, vo)
            wr = _route_lanes(router_group_w[l], router_expert_w[l])
            br = _route_lanes(router_group_b[l][None], router_expert_b[l][None])
            hn, route, counts = router(xcur, moe_norm_g[l], wr, br)
            slots, block_expert, next_block, block_tiles, n_used, n_blocks = _moe_layout(
                route, counts, blk)
            xs = moe_gather(hn, slots, block_tiles, n_used, n_blocks, blk)
            y = moe_experts(xs, block_expert, next_block, n_used, expert_w_gate[l], expert_w_up[l],
                            expert_w_down[l], blk=blk)
            xcur = combine(xcur, route, y, slots, final_norm_g, final_norm=l == depth - 1)
        outs.append(xcur)
    return outs[0][None] if b == 1 else jnp.stack(outs, axis=0)
```

```python
import functools

import jax
import jax.numpy as jnp
from jax import lax
from jax.experimental import pallas as pl
from jax.experimental.pallas import tpu as pltpu

F32 = jnp.float32
BF16 = jnp.bfloat16

RET_HEADS = 8
RET_HEAD_DIM = 256
RET_CHUNK = 128
LRU_BLOCKS = 8
CONV_WIDTH = 4
RG_C = 8.0
ROPE_BASE = 10000.0
X_HEADS = 4
N_GROUPS = 4
EXPERTS_PER_GROUP = 8
N_EXPERTS = N_GROUPS * EXPERTS_PER_GROUP
NORM_EPS = 1e-6
GN_EPS = 1e-5

LANES = 128
SUBLANES = 8
VMEM_LIMIT = 56 * 1024 * 1024

NORM_ROWS = 512
MM_TM = 1024
MM_TN = 512
MM_STREAM_TM = 1024
MM_STREAM_TN = 1024
RET_ROWS = 512
RET_HEADS_PER_STEP = 8
LRU_ROWS = 512
ATT_ROWS = 512
ROUTE_ROWS = 256
MOE_BLK = 256
MOE_TF = 512
MOE_TN = 4096
XF_TILE = 1024
SCALAR_UNROLL = 8
COMB_ROWS = 256
ROUTE_LANES = LANES
GROUP_LANE0 = 0
EXPERT_LANE0 = 8


def _params(sem):
    return pltpu.CompilerParams(dimension_semantics=sem, vmem_limit_bytes=VMEM_LIMIT)


def _normcast_kernel(x_ref, g_ref, o_ref):
    x = x_ref[...]
    ms = jnp.mean(x * x, axis=-1, keepdims=True)
    o_ref[...] = (x * lax.rsqrt(ms + NORM_EPS) * g_ref[...]).astype(o_ref.dtype)


def normcast(x, g, out_dtype, tm):
    m, d = x.shape
    tm = min(tm, m)
    return pl.pallas_call(
        _normcast_kernel,
        grid=(m // tm,),
        in_specs=[pl.BlockSpec((tm, d), lambda i: (i, 0)),
                  pl.BlockSpec((1, d), lambda i: (0, 0))],
        out_specs=pl.BlockSpec((tm, d), lambda i: (i, 0)),
        out_shape=jax.ShapeDtypeStruct((m, d), out_dtype),
        compiler_params=_params(("arbitrary",)),
        name="normcast",
    )(x, g.reshape(1, d))


def _cast_rows(src_ref, dst_ref, rows_per_iter=256):
    k = src_ref.shape[0]
    step = min(rows_per_iter, k)

    def body(i, carry):
        r0 = pl.multiple_of(i * step, step)
        dst_ref[pl.ds(r0, step), :] = src_ref[pl.ds(r0, step), :].astype(dst_ref.dtype)
        return carry

    lax.fori_loop(0, k // step, body, 0)


def _mm_kernel(*refs, n_a, has_res):
    a_refs = refs[:n_a]
    w_ref = refs[n_a]
    res_ref = refs[n_a + 1] if has_res else None
    o_ref = refs[n_a + 1 + int(has_res)]
    wbf_ref = refs[n_a + 2 + int(has_res)]

    @pl.when(pl.program_id(1) == 0)
    def _():
        _cast_rows(w_ref, wbf_ref)

    kp = a_refs[0].shape[1]
    acc = None
    for p, a_ref in enumerate(a_refs):
        d = jnp.dot(a_ref[...], wbf_ref[p * kp:(p + 1) * kp, :], preferred_element_type=F32)
        acc = d if acc is None else acc + d
    if has_res:
        acc = acc + res_ref[...]
    o_ref[...] = acc.astype(o_ref.dtype)


def matmul(a_parts, w, out_dtype, res=None, tm=MM_TM, tn=MM_TN):
    m, kp = a_parts[0].shape
    k, n = w.shape
    assert kp * len(a_parts) == k
    tm = min(tm, m)
    tn = min(tn, n)
    in_specs = [pl.BlockSpec((tm, kp), lambda j, i: (i, 0)) for _ in a_parts]
    in_specs.append(pl.BlockSpec((k, tn), lambda j, i: (0, j)))
    args = list(a_parts) + [w]
    if res is not None:
        in_specs.append(pl.BlockSpec((tm, tn), lambda j, i: (i, j)))
        args.append(res)
    return pl.pallas_call(
        functools.partial(_mm_kernel, n_a=len(a_parts), has_res=res is not None),
        grid=(n // tn, m // tm),
        in_specs=in_specs,
        out_specs=pl.BlockSpec((tm, tn), lambda j, i: (i, j)),
        out_shape=jax.ShapeDtypeStruct((m, n), out_dtype),
        scratch_shapes=[pltpu.VMEM((k, tn), BF16)],
        compiler_params=_params(("arbitrary", "arbitrary")),
        name="matmul",
    )(*args)


def _mm_stream_kernel(*refs, n_a, has_res):
    a_refs = refs[:n_a]
    w_hbm = refs[n_a]
    res_ref = refs[n_a + 1] if has_res else None
    o_ref = refs[n_a + 1 + int(has_res)]
    wbf_ref, stage_ref, sem = refs[n_a + 2 + int(has_res):]
    j = pl.program_id(0)
    i = pl.program_id(1)
    n_j = pl.num_programs(0)
    n_i = pl.num_programs(1)
    kc, tn = stage_ref.shape[1:]
    n_chunks = wbf_ref.shape[1] // kc

    def chunk_copy(col_tile, c, slot):
        rows = pl.ds(pl.multiple_of(c * kc, kc), kc)
        cols = pl.ds(pl.multiple_of(col_tile * tn, tn), tn)
        return pltpu.make_async_copy(w_hbm.at[rows, cols], stage_ref.at[slot], sem.at[slot])

    def cast_chunk(buf, c, slot):
        rows = pl.ds(pl.multiple_of(c * kc, kc), kc)
        wbf_ref[buf, rows, :] = stage_ref[slot].astype(BF16)

    @pl.when((j == 0) & (i == 0))
    def _():
        chunk_copy(0, 0, 0).start()
        for c in range(n_chunks):
            if c + 1 < n_chunks:
                chunk_copy(0, c + 1, (c + 1) % 2).start()
            chunk_copy(0, c, c % 2).wait()
            cast_chunk(0, c, c % 2)

    cur = j % 2
    kp = a_refs[0].shape[1]
    acc = None
    for p, a_ref in enumerate(a_refs):
        d = jnp.dot(a_ref[...], wbf_ref[cur, p * kp:(p + 1) * kp, :], preferred_element_type=F32)
        acc = d if acc is None else acc + d
    if has_res:
        acc = acc + res_ref[...]
    o_ref[...] = acc.astype(o_ref.dtype)

    @pl.when(j + 1 < n_j)
    def _():
        slot = i % 2

        @pl.when(i == 0)
        def _():
            chunk_copy(j + 1, 0, 0).start()

        chunk_copy(j + 1, i, slot).wait()

        @pl.when(i + 1 < n_i)
        def _():
            chunk_copy(j + 1, i + 1, 1 - slot).start()

        cast_chunk(1 - cur, i, slot)


def matmul_streamed(a_parts, w, out_dtype, res=None, tm=MM_STREAM_TM, tn=MM_STREAM_TN):
    m, kp = a_parts[0].shape
    k, n = w.shape
    assert kp * len(a_parts) == k
    tm = min(tm, m)
    tn = min(tn, n)
    n_i = m // tm
    kc = k // n_i
    assert kc * n_i == k and kc % SUBLANES == 0
    in_specs = [pl.BlockSpec((tm, kp), lambda j, i: (i, 0)) for _ in a_parts]
    in_specs.append(pl.BlockSpec(memory_space=pl.ANY))
    args = list(a_parts) + [w]
    if res is not None:
        in_specs.append(pl.BlockSpec((tm, tn), lambda j, i: (i, j)))
        args.append(res)
    return pl.pallas_call(
        functools.partial(_mm_stream_kernel, n_a=len(a_parts), has_res=res is not None),
        grid=(n // tn, n_i),
        in_specs=in_specs,
        out_specs=pl.BlockSpec((tm, tn), lambda j, i: (i, j)),
        out_shape=jax.ShapeDtypeStruct((m, n), out_dtype),
        scratch_shapes=[pltpu.VMEM((2, k, tn), BF16),
                        pltpu.VMEM((2, kc, tn), F32),
                        pltpu.SemaphoreType.DMA((2,))],
        compiler_params=_params(("arbitrary", "arbitrary")),
        name="matmul_streamed",
    )(*args)


def _rope_kernel(pos_ref, invf_ref, cos_ref, sin_ref):
    ang = pos_ref[...] * invf_ref[...]
    cos_ref[...] = jnp.cos(ang)
    sin_ref[...] = jnp.sin(ang)


def rope_tables(pos_f, inv_freq, tm=512):
    s = pos_f.shape[0]
    hd = inv_freq.shape[0]
    tm = min(tm, s)
    return pl.pallas_call(
        _rope_kernel,
        grid=(s // tm,),
        in_specs=[pl.BlockSpec((tm, 1), lambda i: (i, 0)),
                  pl.BlockSpec((1, hd), lambda i: (0, 0))],
        out_specs=[pl.BlockSpec((tm, hd), lambda i: (i, 0))] * 2,
        out_shape=[jax.ShapeDtypeStruct((s, hd), F32)] * 2,
        compiler_params=_params(("arbitrary",)),
        name="rope_tables",
    )(pos_f.reshape(s, 1), inv_freq.reshape(1, hd))


def _ret_kernel(q_ref, k_ref, v_ref, g_ref, cos_ref, sin_ref, lg_ref, gn_ref, o_ref, r_ref,
                decay_ref, xi_ref, zeta_ref, *, n_chunks, hpb):
    c = RET_CHUNK
    dk = RET_HEAD_DIM
    half = dk // 2
    scale = dk ** -0.5

    @pl.when(pl.program_id(1) == 0)
    def _():
        r_ref[...] = jnp.zeros_like(r_ref)
        row = lax.broadcasted_iota(jnp.int32, (c, c), 0).astype(F32)
        col = lax.broadcasted_iota(jnp.int32, (c, c), 1).astype(F32)
        diff = row - col
        rowk = lax.broadcasted_iota(jnp.int32, (c, dk), 0).astype(F32)
        for hh in range(hpb):
            lg = lg_ref[hh]
            decay_ref[hh] = jnp.where(diff >= 0, jnp.exp(lg[:, :c] * jnp.maximum(diff, 0.0)), 0.0)
            xi_ref[hh] = jnp.exp(lg * (rowk + 1.0))
            zeta_ref[hh] = jnp.exp(lg * (c - 1.0 - rowk))

    def rope(t, cos, sin):
        t1 = t[:, :half]
        t2 = t[:, half:]
        return jnp.concatenate([t1 * cos - t2 * sin, t1 * sin + t2 * cos], axis=-1)

    def body(j, carry):
        r0 = pl.multiple_of(j * c, c)
        rows = pl.ds(r0, c)
        cos = cos_ref[rows, :]
        sin = sin_ref[rows, :]
        for hh in range(hpb):
            cs = slice(hh * dk, (hh + 1) * dk)
            qr = rope(q_ref[rows, cs], cos, sin)
            kr = rope(k_ref[rows, cs], cos, sin) * scale
            qb = qr.astype(BF16)
            kb = kr.astype(BF16)
            vb = v_ref[rows, cs].astype(BF16)
            state = r_ref[hh]
            inner = lax.dot_general(qb, kb, (((1,), (1,)), ((), ())),
                                    preferred_element_type=F32) * decay_ref[hh]
            o = (jnp.dot(inner.astype(BF16), vb, preferred_element_type=F32)
                 + jnp.dot(qb, state.astype(BF16), preferred_element_type=F32) * xi_ref[hh])
            kz = (kr * zeta_ref[hh]).astype(BF16)
            chunk_decay = jnp.exp(lg_ref[hh] * c)
            r_ref[hh] = state * chunk_decay + lax.dot_general(
                kz, vb, (((0,), (0,)), ((), ())), preferred_element_type=F32)
            mu = jnp.mean(o, axis=-1, keepdims=True)
            oc = o - mu
            var = jnp.mean(oc * oc, axis=-1, keepdims=True)
            on = oc * lax.rsqrt(var + GN_EPS) * gn_ref[hh]
            g = g_ref[rows, cs]
            o_ref[rows, cs] = (on * (g * (1.0 / (1.0 + jnp.exp(-g))))).astype(o_ref.dtype)
        return carry

    lax.fori_loop(0, n_chunks, body, 0, unroll=2)


def retention(proj, cos, sin, lg_rows, gn_g, tr=RET_ROWS, hpb=RET_HEADS_PER_STEP):
    s = proj.shape[0]
    dk = RET_HEAD_DIM
    h = RET_HEADS
    tr = min(tr, s)
    c = RET_CHUNK
    w = hpb * dk

    def col(base):
        return pl.BlockSpec((tr, w), lambda hg, ci, base=base: (ci, base // hpb + hg))

    per_head = pl.BlockSpec((hpb, 1, dk), lambda hg, ci: (hg, 0, 0))
    return pl.pallas_call(
        functools.partial(_ret_kernel, n_chunks=tr // c, hpb=hpb),
        grid=(h // hpb, s // tr),
        in_specs=[col(0), col(h), col(2 * h), col(3 * h),
                  pl.BlockSpec((tr, dk // 2), lambda hg, ci: (ci, 0)),
                  pl.BlockSpec((tr, dk // 2), lambda hg, ci: (ci, 0)),
                  per_head, per_head],
        out_specs=pl.BlockSpec((tr, w), lambda hg, ci: (ci, hg)),
        out_shape=jax.ShapeDtypeStruct((s, h * dk), BF16),
        scratch_shapes=[pltpu.VMEM((hpb, dk, dk), F32),
                        pltpu.VMEM((hpb, c, c), F32),
                        pltpu.VMEM((hpb, c, dk), F32),
                        pltpu.VMEM((hpb, c, dk), F32)],
        compiler_params=_params(("arbitrary", "arbitrary")),
        name="retention",
    )(proj, proj, proj, proj, cos, sin, lg_rows, gn_g.reshape(h, 1, dk))


def _sigmoid(x):
    return 1.0 / (1.0 + jnp.exp(-x))


def _lru_kernel(xb_ref, gb_ref, cw_ref, cb_ref, wa_ref, ba_ref, wi_ref, bi_ref, lam_ref, og_ref,
                o_ref, tail_ref, xs_ref, hs_ref, h_ref, wabf_ref, wibf_ref):
    tr, cdim = xb_ref.shape
    nb = wa_ref.shape[0]
    bd = cdim // nb
    ph = SUBLANES
    ng = tr // ph

    @pl.when(pl.program_id(0) == 0)
    def _():
        tail_ref[...] = jnp.zeros_like(tail_ref)
        h_ref[...] = jnp.zeros_like(h_ref)
        wabf_ref[...] = wa_ref[...].astype(BF16)
        wibf_ref[...] = wi_ref[...].astype(BF16)

    lam = lam_ref[...]
    sp = jnp.maximum(-lam, 0.0) + jnp.log1p(jnp.exp(-jnp.abs(lam)))
    rowg = lax.broadcasted_iota(jnp.int32, (ng, bd), 0)
    lpb = bd // LANES
    for c in range(cdim // LANES):
        xs_ref[c] = xb_ref[:, c * LANES:(c + 1) * LANES]

    def phase_rows(ref, n, p):
        return jnp.concatenate([ref[n * lpb + c, pl.ds(p, ng, stride=ph), :] for c in range(lpb)],
                               axis=1)

    for n in range(nb):
        cs = slice(n * bd, (n + 1) * bd)
        x = [phase_rows(xs_ref, n, p) for p in range(ph)]

        def prev_group(p):
            return jnp.where(rowg == 0, tail_ref[p:p + 1, cs], pltpu.roll(x[p], 1, 0))

        back = {-k: prev_group(ph - k) for k in range(1, CONV_WIDTH)}

        def xat(p):
            return x[p] if p >= 0 else back[p]

        xc = []
        for p in range(ph):
            acc = cb_ref[:, cs] + cw_ref[CONV_WIDTH - 1:CONV_WIDTH, cs] * xat(p)
            for k in range(1, CONV_WIDTH):
                acc = acc + cw_ref[CONV_WIDTH - 1 - k:CONV_WIDTH - k, cs] * xat(p - k)
            xc.append(acc)
        xg = jnp.concatenate(xc, axis=0)
        xgb = xg.astype(BF16)
        r = _sigmoid(jnp.dot(xgb, wabf_ref[n], preferred_element_type=F32) + ba_ref[:, cs])
        ig = _sigmoid(jnp.dot(xgb, wibf_ref[n], preferred_element_type=F32) + bi_ref[:, cs])
        log_a = (-RG_C * r) * sp[:, cs]
        a = jnp.exp(log_a)
        b = jnp.sqrt(-jnp.tanh(log_a) * (a * a + 1.0)) * (ig * xg)

        cum_a = [a[0:ng]]
        cum_b = [b[0:ng]]
        for p in range(1, ph):
            ap = a[p * ng:(p + 1) * ng]
            cum_b.append(ap * cum_b[-1] + b[p * ng:(p + 1) * ng])
            cum_a.append(ap * cum_a[-1])
        sa, sb = cum_a[-1], cum_b[-1]
        d = 1
        while d < ng:
            keep = rowg >= d
            sa_sh = pltpu.roll(sa, d, 0)
            sb_sh = pltpu.roll(sb, d, 0)
            sb = jnp.where(keep, sa * sb_sh + sb, sb)
            sa = jnp.where(keep, sa * sa_sh, sa)
            d *= 2
        h_in = h_ref[:, cs]
        h_end = sa * h_in + sb
        h_prev = jnp.where(rowg == 0, h_in, pltpu.roll(h_end, 1, 0))
        for p in range(ph):
            hp = cum_a[p] * h_prev + cum_b[p]
            for c in range(lpb):
                hs_ref[n * lpb + c, pl.ds(p, ng, stride=ph), :] = hp[:, c * LANES:(c + 1) * LANES]
        h_ref[:, cs] = h_end[ng - 1:ng, :]

    gb = gb_ref[...]
    gelu = 0.5 * gb * (1.0 + jnp.tanh(0.7978845608028654 * (gb + 0.044715 * (gb * gb * gb))))
    y = jnp.concatenate([hs_ref[c] for c in range(cdim // LANES)], axis=1) * gelu
    ms = jnp.mean(y * y, axis=-1, keepdims=True)
    o_ref[...] = (y * lax.rsqrt(ms + NORM_EPS) * og_ref[...]).astype(o_ref.dtype)
    tail_ref[...] = xb_ref[tr - ph:tr, :]


def rg_lru(proj, xb_block, gb_block, conv_w, conv_b, w_a, b_a, w_i, b_i, lam, out_g, tr=LRU_ROWS):
    s = proj.shape[0]
    cdim = conv_w.shape[1]
    nb, bd, _ = w_a.shape
    tr = min(tr, s)
    vec = pl.BlockSpec((1, cdim), lambda i: (0, 0))
    wspec = pl.BlockSpec((nb, bd, bd), lambda i: (0, 0, 0))
    return pl.pallas_call(
        _lru_kernel,
        grid=(s // tr,),
        in_specs=[pl.BlockSpec((tr, cdim), lambda i: (i, xb_block)),
                  pl.BlockSpec((tr, cdim), lambda i: (i, gb_block)),
                  pl.BlockSpec((CONV_WIDTH, cdim), lambda i: (0, 0)),
                  vec, wspec, vec, wspec, vec, vec, vec],
        out_specs=pl.BlockSpec((tr, cdim), lambda i: (i, 0)),
        out_shape=jax.ShapeDtypeStruct((s, cdim), BF16),
        scratch_shapes=[pltpu.VMEM((SUBLANES, cdim), F32),
                        pltpu.VMEM((cdim // LANES, tr, LANES), F32),
                        pltpu.VMEM((cdim // LANES, tr, LANES), F32),
                        pltpu.VMEM((1, cdim), F32),
                        pltpu.VMEM((nb, bd, bd), BF16),
                        pltpu.VMEM((nb, bd, bd), BF16)],
        compiler_params=_params(("arbitrary",)),
        name="rg_lru",
    )(proj, proj, conv_w, conv_b.reshape(1, cdim), w_a, b_a.reshape(1, cdim), w_i,
      b_i.reshape(1, cdim), lam.reshape(1, cdim), out_g.reshape(1, cdim))


def _wqk_kernel(wq_ref, k_ref, o_ref):
    o_ref[...] = lax.dot_general(wq_ref[...].astype(BF16), k_ref[...], (((1,), (1,)), ((), ())),
                                 preferred_element_type=F32).astype(o_ref.dtype)


def _vo_kernel(v_ref, wo_ref, o_ref):
    o_ref[...] = jnp.dot(v_ref[...], wo_ref[...].astype(BF16),
                         preferred_element_type=F32).astype(o_ref.dtype)


def xattn_fold(k, v, wq, wo, tile=XF_TILE):
    mlen, d = k.shape
    hd = d // X_HEADS
    tile = min(tile, d)
    wqk = pl.pallas_call(
        _wqk_kernel,
        grid=(X_HEADS, d // tile),
        in_specs=[pl.BlockSpec((tile, hd), lambda h, r: (r, h)),
                  pl.BlockSpec((mlen, hd), lambda h, r: (0, h))],
        out_specs=pl.BlockSpec((tile, mlen), lambda h, r: (r, h)),
        out_shape=jax.ShapeDtypeStruct((d, X_HEADS * mlen), BF16),
        compiler_params=_params(("arbitrary", "arbitrary")),
        name="xattn_wqk",
    )(wq, k)
    vo = pl.pallas_call(
        _vo_kernel,
        grid=(X_HEADS, d // tile),
        in_specs=[pl.BlockSpec((mlen, hd), lambda h, j: (0, h)),
                  pl.BlockSpec((hd, tile), lambda h, j: (h, j))],
        out_specs=pl.BlockSpec((mlen, tile), lambda h, j: (h, j)),
        out_shape=jax.ShapeDtypeStruct((X_HEADS * mlen, d), BF16),
        compiler_params=_params(("arbitrary", "arbitrary")),
        name="xattn_vo",
    )(v, wo)
    return wqk, vo


def _xattn_kernel(x_ref, g_ref, wqk_ref, vo_ref, o_ref):
    d = x_ref.shape[1]
    mlen = wqk_ref.shape[1] // X_HEADS
    scale = (d // X_HEADS) ** -0.5
    x = x_ref[...]
    ms = jnp.mean(x * x, axis=-1, keepdims=True)
    h = (x * lax.rsqrt(ms + NORM_EPS) * g_ref[...]).astype(BF16)
    s = jnp.dot(h, wqk_ref[...], preferred_element_type=F32) * scale
    ps = []
    for hh in range(X_HEADS):
        sh = s[:, hh * mlen:(hh + 1) * mlen]
        m = jnp.max(sh, axis=-1, keepdims=True)
        e = jnp.exp(sh - m)
        ps.append((e / jnp.sum(e, axis=-1, keepdims=True)).astype(BF16))
    p = jnp.concatenate(ps, axis=1)
    o_ref[...] = x + jnp.dot(p, vo_ref[...], preferred_element_type=F32)


def xattn(x, g, wqk, vo, tm=ATT_ROWS):
    s, d = x.shape
    tm = min(tm, s)
    return pl.pallas_call(
        _xattn_kernel,
        grid=(s // tm,),
        in_specs=[pl.BlockSpec((tm, d), lambda i: (i, 0)),
                  pl.BlockSpec((1, d), lambda i: (0, 0)),
                  pl.BlockSpec(wqk.shape, lambda i: (0, 0), pipeline_mode=pl.Buffered(1)),
                  pl.BlockSpec(vo.shape, lambda i: (0, 0), pipeline_mode=pl.Buffered(1))],
        out_specs=pl.BlockSpec((tm, d), lambda i: (i, 0)),
        out_shape=jax.ShapeDtypeStruct((s, d), F32),
        compiler_params=_params(("arbitrary",)),
        name="xattn",
    )(x, g.reshape(1, d), wqk, vo)


def _split_bf16(a):
    hi = a.astype(BF16)
    return hi, (a - hi.astype(F32)).astype(BF16)


def _router_kernel(x_ref, g_ref, wr_ref, br_ref, h_ref, route_ref, counts_ref, carry_ref,
                   wsplit_ref):
    tm = x_ref.shape[0]
    nl = ROUTE_LANES

    @pl.when(pl.program_id(0) == 0)
    def _():
        carry_ref[...] = jnp.zeros_like(carry_ref)
        w_hi, w_lo = _split_bf16(wr_ref[...])
        wsplit_ref[:, :nl] = w_hi
        wsplit_ref[:, nl:] = w_lo

    x = x_ref[...]
    ms = jnp.mean(x * x, axis=-1, keepdims=True)
    h = x * lax.rsqrt(ms + NORM_EPS) * g_ref[...]
    h_ref[...] = h
    h_hi, h_lo = _split_bf16(h)
    both = jnp.dot(h_hi, wsplit_ref[...], preferred_element_type=F32)
    cross = jnp.dot(h_lo, wsplit_ref[:, :nl], preferred_element_type=F32)
    logits = both[:, :nl] + (both[:, nl:] + cross) + br_ref[...]
    lane = lax.broadcasted_iota(jnp.int32, (tm, ROUTE_LANES), 1).astype(F32)
    neg = -jnp.inf
    big = float(ROUTE_LANES)

    gmask = (lane >= GROUP_LANE0) & (lane < GROUP_LANE0 + N_GROUPS)
    gl = jnp.where(gmask, logits, neg)
    gmax = jnp.max(gl, axis=-1, keepdims=True)
    gsum = jnp.sum(jnp.where(gmask, jnp.exp(gl - gmax), 0.0), axis=-1, keepdims=True)
    g_val = 1.0 / gsum
    g_idx = jnp.min(jnp.where(gl == gmax, lane, big), axis=-1, keepdims=True) - GROUP_LANE0

    lo = EXPERT_LANE0 + g_idx * EXPERTS_PER_GROUP
    emask = (lane >= lo) & (lane < lo + EXPERTS_PER_GROUP)
    el = jnp.where(emask, logits, neg)
    t1 = jnp.max(el, axis=-1, keepdims=True)
    i1 = jnp.min(jnp.where(emask & (el == t1), lane, big), axis=-1, keepdims=True)
    emask2 = emask & (lane != i1)
    el2 = jnp.where(emask2, logits, neg)
    t2 = jnp.max(el2, axis=-1, keepdims=True)
    i2 = jnp.min(jnp.where(emask2 & (el2 == t2), lane, big), axis=-1, keepdims=True)
    dexp = jnp.exp(t2 - t1)
    w0 = g_val / (1.0 + dexp)
    w1 = g_val * dexp / (1.0 + dexp)

    sel1 = lane == i1
    sel2 = lane == i2
    onehot = jnp.where(sel1 | sel2, 1.0, 0.0)
    rr = lax.broadcasted_iota(jnp.int32, (tm, tm), 0)
    cc = lax.broadcasted_iota(jnp.int32, (tm, tm), 1)
    tri = jnp.where(cc < rr, 1.0, 0.0).astype(BF16)
    prefix = jnp.dot(tri, onehot.astype(BF16), preferred_element_type=F32) + carry_ref[...]
    rank0 = jnp.sum(jnp.where(sel1, prefix, 0.0), axis=-1, keepdims=True)
    rank1 = jnp.sum(jnp.where(sel2, prefix, 0.0), axis=-1, keepdims=True)
    total = carry_ref[...] + jnp.sum(onehot, axis=0, keepdims=True)
    carry_ref[...] = total
    counts_ref[...] = total

    e0 = i1 - EXPERT_LANE0
    e1 = i2 - EXPERT_LANE0
    route = jnp.where(lane == 0, e0, 0.0)
    route = jnp.where(lane == 1, e1, route)
    route = jnp.where(lane == 2, w0, route)
    route = jnp.where(lane == 3, w1, route)
    route = jnp.where(lane == 4, rank0, route)
    route = jnp.where(lane == 5, rank1, route)
    route_ref[...] = route


def router(x, g, wr, br, tm=ROUTE_ROWS):
    t, d = x.shape
    tm = min(tm, t)
    return pl.pallas_call(
        _router_kernel,
        grid=(t // tm,),
        in_specs=[pl.BlockSpec((tm, d), lambda i: (i, 0)),
                  pl.BlockSpec((1, d), lambda i: (0, 0)),
                  pl.BlockSpec((d, ROUTE_LANES), lambda i: (0, 0)),
                  pl.BlockSpec((1, ROUTE_LANES), lambda i: (0, 0))],
        out_specs=[pl.BlockSpec((tm, d), lambda i: (i, 0)),
                   pl.BlockSpec((tm, ROUTE_LANES), lambda i: (i, 0)),
                   pl.BlockSpec((1, ROUTE_LANES), lambda i: (0, 0))],
        out_shape=[jax.ShapeDtypeStruct((t, d), F32),
                   jax.ShapeDtypeStruct((t, ROUTE_LANES), F32),
                   jax.ShapeDtypeStruct((1, ROUTE_LANES), F32)],
        scratch_shapes=[pltpu.VMEM((1, ROUTE_LANES), F32),
                        pltpu.VMEM((d, 2 * ROUTE_LANES), BF16)],
        compiler_params=_params(("arbitrary",)),
        name="router",
    )(x, g.reshape(1, d), wr, br)


def _gather_kernel(e0_ref, e1_ref, r0_ref, r1_ref, ps_ref, nu_ref, x_hbm, xs_ref,
                   rowtok_ref, buf_ref, sem, *, blk, n_tok):
    i = pl.program_id(0)
    nu = nu_ref[0]

    def row_copy(b, slot, r):
        tok = rowtok_ref[b * blk + r]
        return pltpu.make_async_copy(x_hbm.at[pl.ds(tok, 1), :],
                                     buf_ref.at[slot, pl.ds(r, 1), :], sem.at[slot])

    def for_rows(fn):
        def body(grp, carry):
            r0 = pl.multiple_of(grp * SUBLANES, SUBLANES)
            for k in range(SUBLANES):
                fn(r0 + k, k % 2)
            return carry
        lax.fori_loop(0, blk // SUBLANES, body, 0)

    def start_block(b, slot):
        for_rows(lambda r, prio: row_copy(b, slot, r).start(priority=prio))

    def wait_block(b, slot):
        for_rows(lambda r, prio: row_copy(b, slot, r).wait())

    @pl.when(i == 0)
    def _():
        n_rows = rowtok_ref.shape[0]
        for base in range(0, n_rows, n_tok):
            def init(r, carry, base=base):
                rowtok_ref[base + r] = r
                return carry
            lax.fori_loop(0, min(n_tok, n_rows - base), init, 0, unroll=SCALAR_UNROLL)

        def fill(t, carry):
            rowtok_ref[ps_ref[e0_ref[t]] + r0_ref[t]] = t
            rowtok_ref[ps_ref[e1_ref[t]] + r1_ref[t]] = t
            return carry
        lax.fori_loop(0, n_tok, fill, 0, unroll=SCALAR_UNROLL)
        start_block(0, 0)

    @pl.when(i + 1 < nu)
    def _():
        start_block(i + 1, (i + 1) % 2)

    @pl.when(i < nu)
    def _():
        slot = i % 2
        wait_block(i, slot)
        xs_ref[...] = buf_ref[slot].astype(xs_ref.dtype)

    @pl.when(i >= nu)
    def _():
        xs_ref[...] = jnp.zeros_like(xs_ref)


def moe_gather(hn, slots, n_used, n_blocks, blk):
    t, d = hn.shape
    grid_spec = pltpu.PrefetchScalarGridSpec(
        num_scalar_prefetch=6,
        grid=(n_blocks,),
        in_specs=[pl.BlockSpec(memory_space=pl.ANY)],
        out_specs=pl.BlockSpec((blk, d), lambda i, *_: (i, 0)),
        scratch_shapes=[pltpu.SMEM((n_blocks * blk,), jnp.int32),
                        pltpu.VMEM((2, blk, d), F32),
                        pltpu.SemaphoreType.DMA((2,))],
    )
    return pl.pallas_call(
        functools.partial(_gather_kernel, blk=blk, n_tok=t),
        grid_spec=grid_spec,
        out_shape=jax.ShapeDtypeStruct((n_blocks * blk, d), BF16),
        compiler_params=_params(("arbitrary",)),
        name="moe_gather",
    )(*slots, n_used, hn)


def _expert_changed(be_ref, i):
    prev = be_ref[jnp.maximum(i - 1, 0)]
    return (i == 0) | (be_ref[i] != prev)


def _stream_expert_weights(w_hbms, col0, be_ref, nx_ref, nu, i, wst_ref, wbf_ref, sem, slot_ref):
    tn = wst_ref.shape[-1]

    def copies(e, slot):
        return [pltpu.make_async_copy(w.at[e, :, pl.ds(col0, tn)], wst_ref.at[slot, l], sem.at[slot])
                for l, w in enumerate(w_hbms)]

    @pl.when(i == 0)
    def _():
        slot_ref[0] = 0
        for c in copies(be_ref[0], 0):
            c.start()

    @pl.when(_expert_changed(be_ref, i))
    def _():
        slot = slot_ref[0]
        for c in copies(be_ref[i], slot):
            c.wait()
        nxt = nx_ref[i]

        @pl.when(nxt < nu)
        def _():
            for c in copies(be_ref[jnp.minimum(nxt, be_ref.shape[0] - 1)], 1 - slot):
                c.start()

        for l in range(len(w_hbms)):
            _cast_rows(wst_ref.at[slot, l], wbf_ref.at[l])
        slot_ref[0] = 1 - slot


def _moe_up_kernel(be_ref, nx_ref, nu_ref, xs_ref, wg_hbm, wu_hbm, act_ref,
                   wst_ref, wbf_ref, sem, slot_ref):
    j = pl.program_id(0)
    i = pl.program_id(1)
    nu = nu_ref[0]
    tf = act_ref.shape[1]

    @pl.when(i < nu)
    def _():
        _stream_expert_weights([wg_hbm, wu_hbm], pl.multiple_of(j * tf, tf), be_ref, nx_ref, nu, i,
                               wst_ref, wbf_ref, sem, slot_ref)
        x = xs_ref[...]
        gate = jnp.dot(x, wbf_ref[0], preferred_element_type=F32)
        up = jnp.dot(x, wbf_ref[1], preferred_element_type=F32)
        act_ref[...] = (gate * _sigmoid(gate) * up).astype(act_ref.dtype)

    @pl.when(i >= nu)
    def _():
        act_ref[...] = jnp.zeros_like(act_ref)


def _moe_down_kernel(be_ref, nx_ref, nu_ref, act_ref, wd_hbm, y_ref, wst_ref, wbf_ref, sem, slot_ref):
    j = pl.program_id(0)
    i = pl.program_id(1)
    nu = nu_ref[0]
    tn = y_ref.shape[1]

    @pl.when(i < nu)
    def _():
        _stream_expert_weights([wd_hbm], pl.multiple_of(j * tn, tn), be_ref, nx_ref, nu, i,
                               wst_ref, wbf_ref, sem, slot_ref)
        y_ref[...] = jnp.dot(act_ref[...], wbf_ref[0], preferred_element_type=F32)

    @pl.when(i >= nu)
    def _():
        y_ref[...] = jnp.zeros_like(y_ref)


def moe_experts(xs, block_expert, next_expert_block, n_used, w_gate, w_up, w_down,
                blk=MOE_BLK, tf=MOE_TF, tn=MOE_TN):
    r, dw = xs.shape
    _, d, f = w_gate.shape
    n_blocks = r // blk
    tf = min(tf, f)
    tn = min(tn, d)

    def used(i, nu):
        return jnp.minimum(i, jnp.maximum(nu[0] - 1, 0))

    def stream_scratch(n_mats, k, n):
        return [pltpu.VMEM((2, n_mats, k, n), F32),
                pltpu.VMEM((n_mats, k, n), BF16),
                pltpu.SemaphoreType.DMA((2,)),
                pltpu.SMEM((1,), jnp.int32)]

    up_spec = pltpu.PrefetchScalarGridSpec(
        num_scalar_prefetch=3,
        grid=(f // tf, n_blocks),
        in_specs=[pl.BlockSpec((blk, dw), lambda j, i, be, nx, nu: (used(i, nu), 0)),
                  pl.BlockSpec(memory_space=pl.ANY),
                  pl.BlockSpec(memory_space=pl.ANY)],
        out_specs=pl.BlockSpec((blk, tf), lambda j, i, be, nx, nu: (i, j)),
        scratch_shapes=stream_scratch(2, d, tf),
    )
    act = pl.pallas_call(
        _moe_up_kernel,
        grid_spec=up_spec,
        out_shape=jax.ShapeDtypeStruct((r, f), BF16),
        compiler_params=_params(("arbitrary", "arbitrary")),
        name="moe_up",
    )(block_expert, next_expert_block, n_used, xs, w_gate, w_up)
    down_spec = pltpu.PrefetchScalarGridSpec(
        num_scalar_prefetch=3,
        grid=(d // tn, n_blocks),
        in_specs=[pl.BlockSpec((blk, f), lambda j, i, be, nx, nu: (used(i, nu), 0)),
                  pl.BlockSpec(memory_space=pl.ANY)],
        out_specs=pl.BlockSpec((blk, tn), lambda j, i, be, nx, nu: (i, j)),
        scratch_shapes=stream_scratch(1, f, tn),
    )
    return pl.pallas_call(
        _moe_down_kernel,
        grid_spec=down_spec,
        out_shape=jax.ShapeDtypeStruct((r, d), F32),
        compiler_params=_params(("arbitrary", "arbitrary")),
        name="moe_down",
    )(block_expert, next_expert_block, n_used, act, w_down)


def _combine_kernel(e0_ref, e1_ref, r0_ref, r1_ref, ps_ref, x_ref, route_ref, g_ref, y_hbm, o_ref,
                    ya_ref, yb_ref, sem, *, tb, final_norm):
    step = pl.program_id(0)

    def copies(b, slot, i):
        t = b * tb + i
        row0 = ps_ref[e0_ref[t]] + r0_ref[t]
        row1 = ps_ref[e1_ref[t]] + r1_ref[t]
        return (pltpu.make_async_copy(y_hbm.at[pl.ds(row0, 1), :],
                                      ya_ref.at[slot, pl.ds(i, 1), :], sem.at[slot]),
                pltpu.make_async_copy(y_hbm.at[pl.ds(row1, 1), :],
                                      yb_ref.at[slot, pl.ds(i, 1), :], sem.at[slot]))

    def for_rows(fn):
        def body(grp, carry):
            r0 = pl.multiple_of(grp * SUBLANES, SUBLANES)
            for k in range(SUBLANES):
                fn(r0 + k, k % 2)
            return carry
        lax.fori_loop(0, tb // SUBLANES, body, 0)

    def start_block(b, slot):
        def start(i, prio):
            c0, c1 = copies(b, slot, i)
            c0.start(priority=prio)
            c1.start(priority=1 - prio)
        for_rows(start)

    def wait_block(b, slot):
        def wait(i, prio):
            c0, c1 = copies(b, slot, i)
            c0.wait()
            c1.wait()
        for_rows(wait)

    @pl.when(step == 0)
    def _():
        start_block(0, 0)

    @pl.when(step + 1 < pl.num_programs(0))
    def _():
        start_block(step + 1, (step + 1) % 2)

    slot = step % 2
    wait_block(step, slot)
    w0 = route_ref[:, 2:3]
    w1 = route_ref[:, 3:4]
    x = x_ref[...] + (ya_ref[slot] * w0 + yb_ref[slot] * w1)
    if final_norm:
        ms = jnp.mean(x * x, axis=-1, keepdims=True)
        x = x * lax.rsqrt(ms + NORM_EPS) * g_ref[...]
    o_ref[...] = x


def combine(x, route, y, slots, g, final_norm, tb=COMB_ROWS):
    t, d = x.shape
    tb = min(tb, t)
    grid_spec = pltpu.PrefetchScalarGridSpec(
        num_scalar_prefetch=5,
        grid=(t // tb,),
        in_specs=[pl.BlockSpec((tb, d), lambda i, *_: (i, 0)),
                  pl.BlockSpec((tb, ROUTE_LANES), lambda i, *_: (i, 0)),
                  pl.BlockSpec((1, d), lambda i, *_: (0, 0)),
                  pl.BlockSpec(memory_space=pl.ANY)],
        out_specs=pl.BlockSpec((tb, d), lambda i, *_: (i, 0)),
        scratch_shapes=[pltpu.VMEM((2, tb, d), F32), pltpu.VMEM((2, tb, d), F32),
                        pltpu.SemaphoreType.DMA((2,))],
    )
    return pl.pallas_call(
        functools.partial(_combine_kernel, tb=tb, final_norm=final_norm),
        grid_spec=grid_spec,
        out_shape=jax.ShapeDtypeStruct((t, d), F32),
        compiler_params=_params(("arbitrary",)),
        name="moe_combine",
    )(*slots, x, route, g.reshape(1, d), y)


def _route_lanes(group_part, expert_part):
    rows = group_part.shape[0]
    gap = jnp.zeros((rows, EXPERT_LANE0 - GROUP_LANE0 - N_GROUPS), F32)
    tail = jnp.zeros((rows, ROUTE_LANES - EXPERT_LANE0 - N_EXPERTS), F32)
    return jnp.concatenate([group_part, gap, expert_part, tail], axis=1)


def _moe_layout(route, counts, blk):
    t = route.shape[0]
    ri = route[:, :8].astype(jnp.int32)
    e0, e1, rank0, rank1 = ri[:, 0], ri[:, 1], ri[:, 4], ri[:, 5]
    cnt = counts[0, EXPERT_LANE0:EXPERT_LANE0 + N_EXPERTS].astype(jnp.int32)
    padded = (cnt + blk - 1) // blk * blk
    pends = jnp.cumsum(padded)
    pstarts = pends - padded
    n_blocks = (2 * t) // blk + N_EXPERTS
    block_start = jnp.arange(n_blocks, dtype=jnp.int32) * blk
    block_expert = jnp.minimum(
        jnp.sum((block_start[:, None] >= pends[None, :]).astype(jnp.int32), axis=1), N_EXPERTS - 1)
    n_used = (pends[-1] // blk).astype(jnp.int32).reshape(1)
    block_expert = block_expert[jnp.minimum(jnp.arange(n_blocks), jnp.maximum(n_used[0] - 1, 0))]
    next_expert_block = pends[block_expert] // blk
    return (e0, e1, rank0, rank1, pstarts), block_expert, next_expert_block, n_used, n_blocks


def kernel(x, mem, positions, mix_norm_g, w_in, ret_norm_g, lru_conv_w, lru_conv_b, lru_w_a, lru_b_a, lru_w_i, lru_b_i, lru_lambda, lru_norm_g, w_out, xattn_norm_g, mem_norm_g, xattn_wq, xattn_wk, xattn_wv, xattn_wo, moe_norm_g, router_group_w, router_group_b, router_expert_w, router_expert_b, expert_w_gate, expert_w_up, expert_w_down, final_norm_g):
    b, s, d = x.shape
    depth = w_in.shape[0]
    ret_width = RET_HEADS * RET_HEAD_DIM
    lru_width = lru_conv_w.shape[-1]
    assert ret_width == lru_width and ret_width + lru_width == d
    inv_freq = ROPE_BASE ** (-jnp.arange(0, RET_HEAD_DIM, 2, dtype=F32) / RET_HEAD_DIM)
    lg = jnp.log1p(-jnp.exp2(-5.0 - jnp.arange(RET_HEADS, dtype=F32)))
    lg_rows = jnp.broadcast_to(lg[:, None, None], (RET_HEADS, 1, RET_HEAD_DIM))
    blk = min(MOE_BLK, s)
    outs = []
    for bi in range(b):
        xcur = x[bi]
        cos, sin = rope_tables(positions[bi].astype(F32), inv_freq)
        for l in range(depth):
            h = normcast(xcur, mix_norm_g[l], BF16, NORM_ROWS)
            proj = matmul_streamed([h], w_in[l], F32)
            ret = retention(proj, cos, sin, lg_rows, ret_norm_g[l])
            lru = rg_lru(proj, 4 * ret_width // lru_width, 4 * ret_width // lru_width + 1,
                         lru_conv_w[l], lru_conv_b[l], lru_w_a[l], lru_b_a[l], lru_w_i[l],
                         lru_b_i[l], lru_lambda[l], lru_norm_g[l])
            xcur = matmul_streamed([ret, lru], w_out[l], F32, res=xcur, tm=MM_STREAM_TM // 2)
            memn = normcast(mem[bi], mem_norm_g[l], BF16, NORM_ROWS)
            kk = matmul([memn], xattn_wk[l], BF16)
            vv = matmul([memn], xattn_wv[l], BF16)
            wqk, vo = xattn_fold(kk, vv, xattn_wq[l], xattn_wo[l])
            xcur = xattn(xcur, xattn_norm_g[l], wqk, vo)
            wr = _route_lanes(router_group_w[l], router_expert_w[l])
            br = _route_lanes(router_group_b[l][None], router_expert_b[l][None])
            hn, route, counts = router(xcur, moe_norm_g[l], wr, br)
            slots, block_expert, next_block, n_used, n_blocks = _moe_layout(route, counts, blk)
            xs = moe_gather(hn, slots, n_used, n_blocks, blk)
            y = moe_experts(xs, block_expert, next_block, n_used, expert_w_gate[l], expert_w_up[l],
                            expert_w_down[l], blk=blk)
            xcur = combine(xcur, route, y, slots, final_norm_g, final_norm=l == depth - 1)
        outs.append(xcur)
    return outs[0][None] if b == 1 else jnp.stack(outs, axis=0)
```

```python
import functools

import jax
import jax.numpy as jnp
from jax import lax
from jax.experimental import pallas as pl
from jax.experimental.pallas import tpu as pltpu

F32 = jnp.float32
BF16 = jnp.bfloat16

RET_HEADS = 8
RET_HEAD_DIM = 256
RET_CHUNK = 128
LRU_BLOCKS = 8
CONV_WIDTH = 4
RG_C = 8.0
ROPE_BASE = 10000.0
X_HEADS = 4
N_GROUPS = 4
EXPERTS_PER_GROUP = 8
N_EXPERTS = N_GROUPS * EXPERTS_PER_GROUP
NORM_EPS = 1e-6
GN_EPS = 1e-5

LANES = 128
SUBLANES = 8
VMEM_LIMIT = 56 * 1024 * 1024

NORM_ROWS = 512
MM_TM = 1024
MM_TN = 512
MM_STREAM_TM = 1024
MM_STREAM_TN = 1024
RET_ROWS = 512
RET_HEADS_PER_STEP = 8
LRU_ROWS = 256
ATT_ROWS = 512
ROUTE_ROWS = 256
MOE_BLK = 256
MOE_TF = 512
MOE_TN = 4096
XF_TILE = 1024
SCALAR_UNROLL = 8
COMB_ROWS = 256
ROUTE_LANES = LANES
GROUP_LANE0 = 0
EXPERT_LANE0 = 8


def _params(sem):
    return pltpu.CompilerParams(dimension_semantics=sem, vmem_limit_bytes=VMEM_LIMIT)


def _normcast_kernel(x_ref, g_ref, o_ref):
    x = x_ref[...]
    ms = jnp.mean(x * x, axis=-1, keepdims=True)
    o_ref[...] = (x * lax.rsqrt(ms + NORM_EPS) * g_ref[...]).astype(o_ref.dtype)


def normcast(x, g, out_dtype, tm):
    m, d = x.shape
    tm = min(tm, m)
    return pl.pallas_call(
        _normcast_kernel,
        grid=(m // tm,),
        in_specs=[pl.BlockSpec((tm, d), lambda i: (i, 0)),
                  pl.BlockSpec((1, d), lambda i: (0, 0))],
        out_specs=pl.BlockSpec((tm, d), lambda i: (i, 0)),
        out_shape=jax.ShapeDtypeStruct((m, d), out_dtype),
        compiler_params=_params(("arbitrary",)),
        name="normcast",
    )(x, g.reshape(1, d))


def _cast_rows(src_ref, dst_ref, rows_per_iter=256):
    k = src_ref.shape[0]
    step = min(rows_per_iter, k)

    def body(i, carry):
        r0 = pl.multiple_of(i * step, step)
        dst_ref[pl.ds(r0, step), :] = src_ref[pl.ds(r0, step), :].astype(dst_ref.dtype)
        return carry

    lax.fori_loop(0, k // step, body, 0)


def _mm_kernel(*refs, n_a, has_res):
    a_refs = refs[:n_a]
    w_ref = refs[n_a]
    res_ref = refs[n_a + 1] if has_res else None
    o_ref = refs[n_a + 1 + int(has_res)]
    wbf_ref = refs[n_a + 2 + int(has_res)]

    @pl.when(pl.program_id(1) == 0)
    def _():
        _cast_rows(w_ref, wbf_ref)

    kp = a_refs[0].shape[1]
    acc = None
    for p, a_ref in enumerate(a_refs):
        d = jnp.dot(a_ref[...], wbf_ref[p * kp:(p + 1) * kp, :], preferred_element_type=F32)
        acc = d if acc is None else acc + d
    if has_res:
        acc = acc + res_ref[...]
    o_ref[...] = acc.astype(o_ref.dtype)


def matmul(a_parts, w, out_dtype, res=None, tm=MM_TM, tn=MM_TN):
    m, kp = a_parts[0].shape
    k, n = w.shape
    assert kp * len(a_parts) == k
    tm = min(tm, m)
    tn = min(tn, n)
    in_specs = [pl.BlockSpec((tm, kp), lambda j, i: (i, 0)) for _ in a_parts]
    in_specs.append(pl.BlockSpec((k, tn), lambda j, i: (0, j)))
    args = list(a_parts) + [w]
    if res is not None:
        in_specs.append(pl.BlockSpec((tm, tn), lambda j, i: (i, j)))
        args.append(res)
    return pl.pallas_call(
        functools.partial(_mm_kernel, n_a=len(a_parts), has_res=res is not None),
        grid=(n // tn, m // tm),
        in_specs=in_specs,
        out_specs=pl.BlockSpec((tm, tn), lambda j, i: (i, j)),
        out_shape=jax.ShapeDtypeStruct((m, n), out_dtype),
        scratch_shapes=[pltpu.VMEM((k, tn), BF16)],
        compiler_params=_params(("arbitrary", "arbitrary")),
        name="matmul",
    )(*args)


def _mm_stream_kernel(*refs, n_a, has_res):
    a_refs = refs[:n_a]
    w_hbm = refs[n_a]
    res_ref = refs[n_a + 1] if has_res else None
    o_ref = refs[n_a + 1 + int(has_res)]
    wbf_ref, stage_ref, sem = refs[n_a + 2 + int(has_res):]
    j = pl.program_id(0)
    i = pl.program_id(1)
    n_j = pl.num_programs(0)
    n_i = pl.num_programs(1)
    kc, tn = stage_ref.shape[1:]
    n_chunks = wbf_ref.shape[1] // kc

    def chunk_copy(col_tile, c, slot):
        rows = pl.ds(pl.multiple_of(c * kc, kc), kc)
        cols = pl.ds(pl.multiple_of(col_tile * tn, tn), tn)
        return pltpu.make_async_copy(w_hbm.at[rows, cols], stage_ref.at[slot], sem.at[slot])

    def cast_chunk(buf, c, slot):
        rows = pl.ds(pl.multiple_of(c * kc, kc), kc)
        wbf_ref[buf, rows, :] = stage_ref[slot].astype(BF16)

    @pl.when((j == 0) & (i == 0))
    def _():
        chunk_copy(0, 0, 0).start()
        for c in range(n_chunks):
            if c + 1 < n_chunks:
                chunk_copy(0, c + 1, (c + 1) % 2).start()
            chunk_copy(0, c, c % 2).wait()
            cast_chunk(0, c, c % 2)

    kp = a_refs[0].shape[1]
    nxt_tile = jnp.minimum(j + 1, n_j - 1)
    slot = i % 2

    def run(cur):
        @pl.when(i == 0)
        def _():
            chunk_copy(nxt_tile, 0, 0).start()

        @pl.when(i + 1 < n_i)
        def _():
            chunk_copy(nxt_tile, i + 1, 1 - slot).start()

        chunk_copy(nxt_tile, i, slot).wait()
        cast_chunk(1 - cur, i, slot)
        acc = None
        for p, a_ref in enumerate(a_refs):
            d = jnp.dot(a_ref[...], wbf_ref[cur, p * kp:(p + 1) * kp, :],
                        preferred_element_type=F32)
            acc = d if acc is None else acc + d
        if has_res:
            acc = acc + res_ref[...]
        o_ref[...] = acc.astype(o_ref.dtype)

    for parity in range(2):
        pl.when(j % 2 == parity)(functools.partial(run, parity))


def matmul_streamed(a_parts, w, out_dtype, res=None, tm=MM_STREAM_TM, tn=MM_STREAM_TN):
    m, kp = a_parts[0].shape
    k, n = w.shape
    assert kp * len(a_parts) == k
    tm = min(tm, m)
    tn = min(tn, n)
    n_i = m // tm
    kc = k // n_i
    assert kc * n_i == k and kc % SUBLANES == 0
    in_specs = [pl.BlockSpec((tm, kp), lambda j, i: (i, 0)) for _ in a_parts]
    in_specs.append(pl.BlockSpec(memory_space=pl.ANY))
    args = list(a_parts) + [w]
    if res is not None:
        in_specs.append(pl.BlockSpec((tm, tn), lambda j, i: (i, j)))
        args.append(res)
    return pl.pallas_call(
        functools.partial(_mm_stream_kernel, n_a=len(a_parts), has_res=res is not None),
        grid=(n // tn, n_i),
        in_specs=in_specs,
        out_specs=pl.BlockSpec((tm, tn), lambda j, i: (i, j)),
        out_shape=jax.ShapeDtypeStruct((m, n), out_dtype),
        scratch_shapes=[pltpu.VMEM((2, k, tn), BF16),
                        pltpu.VMEM((2, kc, tn), F32),
                        pltpu.SemaphoreType.DMA((2,))],
        compiler_params=_params(("arbitrary", "arbitrary")),
        name="matmul_streamed",
    )(*args)


def _rope_kernel(pos_ref, invf_ref, cos_ref, sin_ref):
    ang = pos_ref[...] * invf_ref[...]
    cos_ref[...] = jnp.cos(ang)
    sin_ref[...] = jnp.sin(ang)


def rope_tables(pos_f, inv_freq, tm=512):
    s = pos_f.shape[0]
    hd = inv_freq.shape[0]
    tm = min(tm, s)
    return pl.pallas_call(
        _rope_kernel,
        grid=(s // tm,),
        in_specs=[pl.BlockSpec((tm, 1), lambda i: (i, 0)),
                  pl.BlockSpec((1, hd), lambda i: (0, 0))],
        out_specs=[pl.BlockSpec((tm, hd), lambda i: (i, 0))] * 2,
        out_shape=[jax.ShapeDtypeStruct((s, hd), F32)] * 2,
        compiler_params=_params(("arbitrary",)),
        name="rope_tables",
    )(pos_f.reshape(s, 1), inv_freq.reshape(1, hd))


def _ret_kernel(q_ref, k_ref, v_ref, g_ref, cos_ref, sin_ref, lg_ref, gn_ref, o_ref, r_ref,
                decay_ref, xi_ref, zeta_ref, *, n_chunks, hpb):
    c = RET_CHUNK
    dk = RET_HEAD_DIM
    half = dk // 2
    scale = dk ** -0.5

    @pl.when(pl.program_id(1) == 0)
    def _():
        r_ref[...] = jnp.zeros_like(r_ref)
        row = lax.broadcasted_iota(jnp.int32, (c, c), 0).astype(F32)
        col = lax.broadcasted_iota(jnp.int32, (c, c), 1).astype(F32)
        diff = row - col
        rowk = lax.broadcasted_iota(jnp.int32, (c, dk), 0).astype(F32)
        for hh in range(hpb):
            lg = lg_ref[hh]
            decay_ref[hh] = jnp.where(diff >= 0, jnp.exp(lg[:, :c] * jnp.maximum(diff, 0.0)), 0.0)
            xi_ref[hh] = jnp.exp(lg * (rowk + 1.0))
            zeta_ref[hh] = jnp.exp(lg * (c - 1.0 - rowk))

    def rope(t, cos, sin):
        t1 = t[:, :half]
        t2 = t[:, half:]
        return jnp.concatenate([t1 * cos - t2 * sin, t1 * sin + t2 * cos], axis=-1)

    def body(j, carry):
        r0 = pl.multiple_of(j * c, c)
        rows = pl.ds(r0, c)
        cos = cos_ref[rows, :]
        sin = sin_ref[rows, :]
        for hh in range(hpb):
            cs = slice(hh * dk, (hh + 1) * dk)
            qr = rope(q_ref[rows, cs], cos, sin)
            kr = rope(k_ref[rows, cs], cos, sin) * scale
            qb = qr.astype(BF16)
            kb = kr.astype(BF16)
            vb = v_ref[rows, cs].astype(BF16)
            state = r_ref[hh]
            inner = lax.dot_general(qb, kb, (((1,), (1,)), ((), ())),
                                    preferred_element_type=F32) * decay_ref[hh]
            o = (jnp.dot(inner.astype(BF16), vb, preferred_element_type=F32)
                 + jnp.dot(qb, state.astype(BF16), preferred_element_type=F32) * xi_ref[hh])
            kz = (kr * zeta_ref[hh]).astype(BF16)
            chunk_decay = jnp.exp(lg_ref[hh] * c)
            r_ref[hh] = state * chunk_decay + lax.dot_general(
                kz, vb, (((0,), (0,)), ((), ())), preferred_element_type=F32)
            mu = jnp.mean(o, axis=-1, keepdims=True)
            oc = o - mu
            var = jnp.mean(oc * oc, axis=-1, keepdims=True)
            on = oc * lax.rsqrt(var + GN_EPS) * gn_ref[hh]
            g = g_ref[rows, cs]
            o_ref[rows, cs] = (on * (g * (1.0 / (1.0 + jnp.exp(-g))))).astype(o_ref.dtype)
        return carry

    lax.fori_loop(0, n_chunks, body, 0, unroll=2)


def retention(proj, cos, sin, lg_rows, gn_g, tr=RET_ROWS, hpb=RET_HEADS_PER_STEP):
    s = proj.shape[0]
    dk = RET_HEAD_DIM
    h = RET_HEADS
    tr = min(tr, s)
    c = RET_CHUNK
    w = hpb * dk

    def col(base):
        return pl.BlockSpec((tr, w), lambda hg, ci, base=base: (ci, base // hpb + hg))

    per_head = pl.BlockSpec((hpb, 1, dk), lambda hg, ci: (hg, 0, 0))
    return pl.pallas_call(
        functools.partial(_ret_kernel, n_chunks=tr // c, hpb=hpb),
        grid=(h // hpb, s // tr),
        in_specs=[col(0), col(h), col(2 * h), col(3 * h),
                  pl.BlockSpec((tr, dk // 2), lambda hg, ci: (ci, 0)),
                  pl.BlockSpec((tr, dk // 2), lambda hg, ci: (ci, 0)),
                  per_head, per_head],
        out_specs=pl.BlockSpec((tr, w), lambda hg, ci: (ci, hg)),
        out_shape=jax.ShapeDtypeStruct((s, h * dk), BF16),
        scratch_shapes=[pltpu.VMEM((hpb, dk, dk), F32),
                        pltpu.VMEM((hpb, c, c), F32),
                        pltpu.VMEM((hpb, c, dk), F32),
                        pltpu.VMEM((hpb, c, dk), F32)],
        compiler_params=_params(("arbitrary", "arbitrary")),
        name="retention",
    )(proj, proj, proj, proj, cos, sin, lg_rows, gn_g.reshape(h, 1, dk))


def _sigmoid(x):
    return 1.0 / (1.0 + jnp.exp(-x))


def _lru_kernel(xb_ref, gb_ref, cw_ref, cb_ref, wa_ref, ba_ref, wi_ref, bi_ref, lam_ref, og_ref,
                o_ref, tail_ref, xs_ref, hs_ref, h_ref, wabf_ref, wibf_ref):
    tr, cdim = xb_ref.shape
    nb = wa_ref.shape[0]
    bd = cdim // nb
    ph = SUBLANES
    ng = tr // ph

    @pl.when(pl.program_id(0) == 0)
    def _():
        tail_ref[...] = jnp.zeros_like(tail_ref)
        h_ref[...] = jnp.zeros_like(h_ref)
        wabf_ref[...] = wa_ref[...].astype(BF16)
        wibf_ref[...] = wi_ref[...].astype(BF16)

    lam = lam_ref[...]
    sp = jnp.maximum(-lam, 0.0) + jnp.log1p(jnp.exp(-jnp.abs(lam)))
    rowg = lax.broadcasted_iota(jnp.int32, (ng, bd), 0)
    lpb = bd // LANES
    for c in range(cdim // LANES):
        xs_ref[c] = xb_ref[:, c * LANES:(c + 1) * LANES]

    def phase_rows(ref, n, p):
        return jnp.concatenate([ref[n * lpb + c, pl.ds(p, ng, stride=ph), :] for c in range(lpb)],
                               axis=1)

    for n in range(nb):
        cs = slice(n * bd, (n + 1) * bd)
        x = [phase_rows(xs_ref, n, p) for p in range(ph)]

        def prev_group(p):
            return jnp.where(rowg == 0, tail_ref[p:p + 1, cs], pltpu.roll(x[p], 1, 0))

        back = {-k: prev_group(ph - k) for k in range(1, CONV_WIDTH)}

        def xat(p):
            return x[p] if p >= 0 else back[p]

        xc = []
        for p in range(ph):
            acc = cb_ref[:, cs] + cw_ref[CONV_WIDTH - 1:CONV_WIDTH, cs] * xat(p)
            for k in range(1, CONV_WIDTH):
                acc = acc + cw_ref[CONV_WIDTH - 1 - k:CONV_WIDTH - k, cs] * xat(p - k)
            xc.append(acc)
        xg = jnp.concatenate(xc, axis=0)
        xgb = xg.astype(BF16)
        r = _sigmoid(jnp.dot(xgb, wabf_ref[n], preferred_element_type=F32) + ba_ref[:, cs])
        ig = _sigmoid(jnp.dot(xgb, wibf_ref[n], preferred_element_type=F32) + bi_ref[:, cs])
        log_a = (-RG_C * r) * sp[:, cs]
        a = jnp.exp(log_a)
        b = jnp.sqrt(-jnp.tanh(log_a) * (a * a + 1.0)) * (ig * xg)

        cum_a = [a[0:ng]]
        cum_b = [b[0:ng]]
        for p in range(1, ph):
            ap = a[p * ng:(p + 1) * ng]
            cum_b.append(ap * cum_b[-1] + b[p * ng:(p + 1) * ng])
            cum_a.append(ap * cum_a[-1])
        sa, sb = cum_a[-1], cum_b[-1]
        d = 1
        while d < ng:
            keep = rowg >= d
            sa_sh = pltpu.roll(sa, d, 0)
            sb_sh = pltpu.roll(sb, d, 0)
            sb = jnp.where(keep, sa * sb_sh + sb, sb)
            sa = jnp.where(keep, sa * sa_sh, sa)
            d *= 2
        h_in = h_ref[:, cs]
        h_end = sa * h_in + sb
        h_prev = jnp.where(rowg == 0, h_in, pltpu.roll(h_end, 1, 0))
        for p in range(ph):
            hp = cum_a[p] * h_prev + cum_b[p]
            for c in range(lpb):
                hs_ref[n * lpb + c, pl.ds(p, ng, stride=ph), :] = hp[:, c * LANES:(c + 1) * LANES]
        h_ref[:, cs] = h_end[ng - 1:ng, :]

    gb = gb_ref[...]
    gelu = 0.5 * gb * (1.0 + jnp.tanh(0.7978845608028654 * (gb + 0.044715 * (gb * gb * gb))))
    y = jnp.concatenate([hs_ref[c] for c in range(cdim // LANES)], axis=1) * gelu
    ms = jnp.mean(y * y, axis=-1, keepdims=True)
    o_ref[...] = (y * lax.rsqrt(ms + NORM_EPS) * og_ref[...]).astype(o_ref.dtype)
    tail_ref[...] = xb_ref[tr - ph:tr, :]


def rg_lru(proj, xb_block, gb_block, conv_w, conv_b, w_a, b_a, w_i, b_i, lam, out_g, tr=LRU_ROWS):
    s = proj.shape[0]
    cdim = conv_w.shape[1]
    nb, bd, _ = w_a.shape
    tr = min(tr, s)
    vec = pl.BlockSpec((1, cdim), lambda i: (0, 0))
    wspec = pl.BlockSpec((nb, bd, bd), lambda i: (0, 0, 0))
    return pl.pallas_call(
        _lru_kernel,
        grid=(s // tr,),
        in_specs=[pl.BlockSpec((tr, cdim), lambda i: (i, xb_block)),
                  pl.BlockSpec((tr, cdim), lambda i: (i, gb_block)),
                  pl.BlockSpec((CONV_WIDTH, cdim), lambda i: (0, 0)),
                  vec, wspec, vec, wspec, vec, vec, vec],
        out_specs=pl.BlockSpec((tr, cdim), lambda i: (i, 0)),
        out_shape=jax.ShapeDtypeStruct((s, cdim), BF16),
        scratch_shapes=[pltpu.VMEM((SUBLANES, cdim), F32),
                        pltpu.VMEM((cdim // LANES, tr, LANES), F32),
                        pltpu.VMEM((cdim // LANES, tr, LANES), F32),
                        pltpu.VMEM((1, cdim), F32),
                        pltpu.VMEM((nb, bd, bd), BF16),
                        pltpu.VMEM((nb, bd, bd), BF16)],
        compiler_params=_params(("arbitrary",)),
        name="rg_lru",
    )(proj, proj, conv_w, conv_b.reshape(1, cdim), w_a, b_a.reshape(1, cdim), w_i,
      b_i.reshape(1, cdim), lam.reshape(1, cdim), out_g.reshape(1, cdim))


def _wqk_kernel(wq_ref, k_ref, o_ref):
    o_ref[...] = lax.dot_general(wq_ref[...].astype(BF16), k_ref[...], (((1,), (1,)), ((), ())),
                                 preferred_element_type=F32).astype(o_ref.dtype)


def _vo_kernel(v_ref, wo_ref, o_ref):
    o_ref[...] = jnp.dot(v_ref[...], wo_ref[...].astype(BF16),
                         preferred_element_type=F32).astype(o_ref.dtype)


def xattn_fold(k, v, wq, wo, tile=XF_TILE):
    mlen, d = k.shape
    hd = d // X_HEADS
    tile = min(tile, d)
    wqk = pl.pallas_call(
        _wqk_kernel,
        grid=(X_HEADS, d // tile),
        in_specs=[pl.BlockSpec((tile, hd), lambda h, r: (r, h)),
                  pl.BlockSpec((mlen, hd), lambda h, r: (0, h))],
        out_specs=pl.BlockSpec((tile, mlen), lambda h, r: (r, h)),
        out_shape=jax.ShapeDtypeStruct((d, X_HEADS * mlen), BF16),
        compiler_params=_params(("arbitrary", "arbitrary")),
        name="xattn_wqk",
    )(wq, k)
    vo = pl.pallas_call(
        _vo_kernel,
        grid=(X_HEADS, d // tile),
        in_specs=[pl.BlockSpec((mlen, hd), lambda h, j: (0, h)),
                  pl.BlockSpec((hd, tile), lambda h, j: (h, j))],
        out_specs=pl.BlockSpec((mlen, tile), lambda h, j: (h, j)),
        out_shape=jax.ShapeDtypeStruct((X_HEADS * mlen, d), BF16),
        compiler_params=_params(("arbitrary", "arbitrary")),
        name="xattn_vo",
    )(v, wo)
    return wqk, vo


def _xattn_kernel(x_ref, g_ref, wqk_ref, vo_ref, o_ref):
    d = x_ref.shape[1]
    mlen = wqk_ref.shape[1] // X_HEADS
    scale = (d // X_HEADS) ** -0.5
    x = x_ref[...]
    ms = jnp.mean(x * x, axis=-1, keepdims=True)
    h = (x * lax.rsqrt(ms + NORM_EPS) * g_ref[...]).astype(BF16)
    s = jnp.dot(h, wqk_ref[...], preferred_element_type=F32) * scale
    ps = []
    for hh in range(X_HEADS):
        sh = s[:, hh * mlen:(hh + 1) * mlen]
        m = jnp.max(sh, axis=-1, keepdims=True)
        e = jnp.exp(sh - m)
        ps.append((e / jnp.sum(e, axis=-1, keepdims=True)).astype(BF16))
    p = jnp.concatenate(ps, axis=1)
    o_ref[...] = x + jnp.dot(p, vo_ref[...], preferred_element_type=F32)


def xattn(x, g, wqk, vo, tm=ATT_ROWS):
    s, d = x.shape
    tm = min(tm, s)
    return pl.pallas_call(
        _xattn_kernel,
        grid=(s // tm,),
        in_specs=[pl.BlockSpec((tm, d), lambda i: (i, 0)),
                  pl.BlockSpec((1, d), lambda i: (0, 0)),
                  pl.BlockSpec(wqk.shape, lambda i: (0, 0), pipeline_mode=pl.Buffered(1)),
                  pl.BlockSpec(vo.shape, lambda i: (0, 0), pipeline_mode=pl.Buffered(1))],
        out_specs=pl.BlockSpec((tm, d), lambda i: (i, 0)),
        out_shape=jax.ShapeDtypeStruct((s, d), F32),
        compiler_params=_params(("arbitrary",)),
        name="xattn",
    )(x, g.reshape(1, d), wqk, vo)


def _split_bf16(a):
    hi = a.astype(BF16)
    return hi, (a - hi.astype(F32)).astype(BF16)


def _router_kernel(x_ref, g_ref, wr_ref, br_ref, h_ref, route_ref, counts_ref, carry_ref,
                   wsplit_ref):
    tm = x_ref.shape[0]
    nl = ROUTE_LANES

    @pl.when(pl.program_id(0) == 0)
    def _():
        carry_ref[...] = jnp.zeros_like(carry_ref)
        w_hi, w_lo = _split_bf16(wr_ref[...])
        wsplit_ref[:, :nl] = w_hi
        wsplit_ref[:, nl:] = w_lo

    x = x_ref[...]
    ms = jnp.mean(x * x, axis=-1, keepdims=True)
    h = x * lax.rsqrt(ms + NORM_EPS) * g_ref[...]
    h_ref[...] = h
    h_hi, h_lo = _split_bf16(h)
    both = jnp.dot(h_hi, wsplit_ref[...], preferred_element_type=F32)
    cross = jnp.dot(h_lo, wsplit_ref[:, :nl], preferred_element_type=F32)
    logits = both[:, :nl] + (both[:, nl:] + cross) + br_ref[...]
    lane = lax.broadcasted_iota(jnp.int32, (tm, ROUTE_LANES), 1).astype(F32)
    neg = -jnp.inf
    big = float(ROUTE_LANES)

    gmask = (lane >= GROUP_LANE0) & (lane < GROUP_LANE0 + N_GROUPS)
    gl = jnp.where(gmask, logits, neg)
    gmax = jnp.max(gl, axis=-1, keepdims=True)
    gsum = jnp.sum(jnp.where(gmask, jnp.exp(gl - gmax), 0.0), axis=-1, keepdims=True)
    g_val = 1.0 / gsum
    g_idx = jnp.min(jnp.where(gl == gmax, lane, big), axis=-1, keepdims=True) - GROUP_LANE0

    lo = EXPERT_LANE0 + g_idx * EXPERTS_PER_GROUP
    emask = (lane >= lo) & (lane < lo + EXPERTS_PER_GROUP)
    el = jnp.where(emask, logits, neg)
    t1 = jnp.max(el, axis=-1, keepdims=True)
    i1 = jnp.min(jnp.where(emask & (el == t1), lane, big), axis=-1, keepdims=True)
    emask2 = emask & (lane != i1)
    el2 = jnp.where(emask2, logits, neg)
    t2 = jnp.max(el2, axis=-1, keepdims=True)
    i2 = jnp.min(jnp.where(emask2 & (el2 == t2), lane, big), axis=-1, keepdims=True)
    dexp = jnp.exp(t2 - t1)
    w0 = g_val / (1.0 + dexp)
    w1 = g_val * dexp / (1.0 + dexp)

    sel1 = lane == i1
    sel2 = lane == i2
    onehot = jnp.where(sel1 | sel2, 1.0, 0.0)
    rr = lax.broadcasted_iota(jnp.int32, (tm, tm), 0)
    cc = lax.broadcasted_iota(jnp.int32, (tm, tm), 1)
    tri = jnp.where(cc < rr, 1.0, 0.0).astype(BF16)
    prefix = jnp.dot(tri, onehot.astype(BF16), preferred_element_type=F32) + carry_ref[...]
    rank0 = jnp.sum(jnp.where(sel1, prefix, 0.0), axis=-1, keepdims=True)
    rank1 = jnp.sum(jnp.where(sel2, prefix, 0.0), axis=-1, keepdims=True)
    total = carry_ref[...] + jnp.sum(onehot, axis=0, keepdims=True)
    carry_ref[...] = total
    counts_ref[...] = total

    e0 = i1 - EXPERT_LANE0
    e1 = i2 - EXPERT_LANE0
    route = jnp.where(lane == 0, e0, 0.0)
    route = jnp.where(lane == 1, e1, route)
    route = jnp.where(lane == 2, w0, route)
    route = jnp.where(lane == 3, w1, route)
    route = jnp.where(lane == 4, rank0, route)
    route = jnp.where(lane == 5, rank1, route)
    route_ref[...] = route


def router(x, g, wr, br, tm=ROUTE_ROWS):
    t, d = x.shape
    tm = min(tm, t)
    return pl.pallas_call(
        _router_kernel,
        grid=(t // tm,),
        in_specs=[pl.BlockSpec((tm, d), lambda i: (i, 0)),
                  pl.BlockSpec((1, d), lambda i: (0, 0)),
                  pl.BlockSpec((d, ROUTE_LANES), lambda i: (0, 0)),
                  pl.BlockSpec((1, ROUTE_LANES), lambda i: (0, 0))],
        out_specs=[pl.BlockSpec((tm, d), lambda i: (i, 0)),
                   pl.BlockSpec((tm, ROUTE_LANES), lambda i: (i, 0)),
                   pl.BlockSpec((1, ROUTE_LANES), lambda i: (0, 0))],
        out_shape=[jax.ShapeDtypeStruct((t, d), F32),
                   jax.ShapeDtypeStruct((t, ROUTE_LANES), F32),
                   jax.ShapeDtypeStruct((1, ROUTE_LANES), F32)],
        scratch_shapes=[pltpu.VMEM((1, ROUTE_LANES), F32),
                        pltpu.VMEM((d, 2 * ROUTE_LANES), BF16)],
        compiler_params=_params(("arbitrary",)),
        name="router",
    )(x, g.reshape(1, d), wr, br)


def _gather_kernel(e0_ref, e1_ref, r0_ref, r1_ref, ps_ref, nu_ref, x_hbm, xs_ref,
                   rowtok_ref, buf_ref, sem, *, blk, n_tok):
    i = pl.program_id(0)
    nu = nu_ref[0]

    def row_copy(b, slot, r):
        tok = rowtok_ref[b * blk + r]
        return pltpu.make_async_copy(x_hbm.at[pl.ds(tok, 1), :],
                                     buf_ref.at[slot, pl.ds(r, 1), :], sem.at[slot])

    def for_rows(fn):
        def body(grp, carry):
            r0 = pl.multiple_of(grp * SUBLANES, SUBLANES)
            for k in range(SUBLANES):
                fn(r0 + k)
            return carry
        lax.fori_loop(0, blk // SUBLANES, body, 0)

    def start_block(b, slot):
        for_rows(lambda r: row_copy(b, slot, r).start())

    def wait_block(b, slot):
        for_rows(lambda r: row_copy(b, slot, r).wait())

    @pl.when(i == 0)
    def _():
        n_rows = rowtok_ref.shape[0]
        for base in range(0, n_rows, n_tok):
            def init(r, carry, base=base):
                rowtok_ref[base + r] = r
                return carry
            lax.fori_loop(0, min(n_tok, n_rows - base), init, 0, unroll=SCALAR_UNROLL)

        def fill(t, carry):
            rowtok_ref[ps_ref[e0_ref[t]] + r0_ref[t]] = t
            rowtok_ref[ps_ref[e1_ref[t]] + r1_ref[t]] = t
            return carry
        lax.fori_loop(0, n_tok, fill, 0, unroll=SCALAR_UNROLL)
        start_block(0, 0)

    @pl.when(i + 1 < nu)
    def _():
        start_block(i + 1, (i + 1) % 2)

    @pl.when(i < nu)
    def _():
        slot = i % 2
        wait_block(i, slot)
        xs_ref[...] = buf_ref[slot].astype(xs_ref.dtype)

    @pl.when(i >= nu)
    def _():
        xs_ref[...] = jnp.zeros_like(xs_ref)


def moe_gather(hn, slots, n_used, n_blocks, blk):
    t, d = hn.shape
    grid_spec = pltpu.PrefetchScalarGridSpec(
        num_scalar_prefetch=6,
        grid=(n_blocks,),
        in_specs=[pl.BlockSpec(memory_space=pl.ANY)],
        out_specs=pl.BlockSpec((blk, d), lambda i, *_: (i, 0)),
        scratch_shapes=[pltpu.SMEM((n_blocks * blk,), jnp.int32),
                        pltpu.VMEM((2, blk, d), F32),
                        pltpu.SemaphoreType.DMA((2,))],
    )
    return pl.pallas_call(
        functools.partial(_gather_kernel, blk=blk, n_tok=t),
        grid_spec=grid_spec,
        out_shape=jax.ShapeDtypeStruct((n_blocks * blk, d), BF16),
        compiler_params=_params(("arbitrary",)),
        name="moe_gather",
    )(*slots, n_used, hn)


def _expert_changed(be_ref, i):
    prev = be_ref[jnp.maximum(i - 1, 0)]
    return (i == 0) | (be_ref[i] != prev)


def _stream_expert_weights(w_hbms, col0, be_ref, nx_ref, nu, i, wst_ref, wbf_ref, sem, slot_ref):
    tn = wst_ref.shape[-1]

    def copies(e, slot):
        return [pltpu.make_async_copy(w.at[e, :, pl.ds(col0, tn)], wst_ref.at[slot, l], sem.at[slot])
                for l, w in enumerate(w_hbms)]

    @pl.when(i == 0)
    def _():
        slot_ref[0] = 0
        for c in copies(be_ref[0], 0):
            c.start()

    @pl.when(_expert_changed(be_ref, i))
    def _():
        slot = slot_ref[0]
        for c in copies(be_ref[i], slot):
            c.wait()
        nxt = nx_ref[i]

        @pl.when(nxt < nu)
        def _():
            for c in copies(be_ref[jnp.minimum(nxt, be_ref.shape[0] - 1)], 1 - slot):
                c.start()

        for l in range(len(w_hbms)):
            _cast_rows(wst_ref.at[slot, l], wbf_ref.at[l])
        slot_ref[0] = 1 - slot


def _moe_up_kernel(be_ref, nx_ref, nu_ref, xs_ref, wg_hbm, wu_hbm, act_ref,
                   wst_ref, wbf_ref, sem, slot_ref):
    j = pl.program_id(0)
    i = pl.program_id(1)
    nu = nu_ref[0]
    tf = act_ref.shape[1]

    @pl.when(i < nu)
    def _():
        _stream_expert_weights([wg_hbm, wu_hbm], pl.multiple_of(j * tf, tf), be_ref, nx_ref, nu, i,
                               wst_ref, wbf_ref, sem, slot_ref)
        x = xs_ref[...]
        gate = jnp.dot(x, wbf_ref[0], preferred_element_type=F32)
        up = jnp.dot(x, wbf_ref[1], preferred_element_type=F32)
        act_ref[...] = (gate * _sigmoid(gate) * up).astype(act_ref.dtype)

    @pl.when(i >= nu)
    def _():
        act_ref[...] = jnp.zeros_like(act_ref)


def _moe_down_kernel(be_ref, nx_ref, nu_ref, act_ref, wd_hbm, y_ref, wst_ref, wbf_ref, sem, slot_ref):
    j = pl.program_id(0)
    i = pl.program_id(1)
    nu = nu_ref[0]
    tn = y_ref.shape[1]

    @pl.when(i < nu)
    def _():
        _stream_expert_weights([wd_hbm], pl.multiple_of(j * tn, tn), be_ref, nx_ref, nu, i,
                               wst_ref, wbf_ref, sem, slot_ref)
        y_ref[...] = jnp.dot(act_ref[...], wbf_ref[0], preferred_element_type=F32)

    @pl.when(i >= nu)
    def _():
        y_ref[...] = jnp.zeros_like(y_ref)


def moe_experts(xs, block_expert, next_expert_block, n_used, w_gate, w_up, w_down,
                blk=MOE_BLK, tf=MOE_TF, tn=MOE_TN):
    r, dw = xs.shape
    _, d, f = w_gate.shape
    n_blocks = r // blk
    tf = min(tf, f)
    tn = min(tn, d)

    def used(i, nu):
        return jnp.minimum(i, jnp.maximum(nu[0] - 1, 0))

    def stream_scratch(n_mats, k, n):
        return [pltpu.VMEM((2, n_mats, k, n), F32),
                pltpu.VMEM((n_mats, k, n), BF16),
                pltpu.SemaphoreType.DMA((2,)),
                pltpu.SMEM((1,), jnp.int32)]

    up_spec = pltpu.PrefetchScalarGridSpec(
        num_scalar_prefetch=3,
        grid=(f // tf, n_blocks),
        in_specs=[pl.BlockSpec((blk, dw), lambda j, i, be, nx, nu: (used(i, nu), 0)),
                  pl.BlockSpec(memory_space=pl.ANY),
                  pl.BlockSpec(memory_space=pl.ANY)],
        out_specs=pl.BlockSpec((blk, tf), lambda j, i, be, nx, nu: (i, j)),
        scratch_shapes=stream_scratch(2, d, tf),
    )
    act = pl.pallas_call(
        _moe_up_kernel,
        grid_spec=up_spec,
        out_shape=jax.ShapeDtypeStruct((r, f), BF16),
        compiler_params=_params(("arbitrary", "arbitrary")),
        name="moe_up",
    )(block_expert, next_expert_block, n_used, xs, w_gate, w_up)
    down_spec = pltpu.PrefetchScalarGridSpec(
        num_scalar_prefetch=3,
        grid=(d // tn, n_blocks),
        in_specs=[pl.BlockSpec((blk, f), lambda j, i, be, nx, nu: (used(i, nu), 0)),
                  pl.BlockSpec(memory_space=pl.ANY)],
        out_specs=pl.BlockSpec((blk, tn), lambda j, i, be, nx, nu: (i, j)),
        scratch_shapes=stream_scratch(1, f, tn),
    )
    return pl.pallas_call(
        _moe_down_kernel,
        grid_spec=down_spec,
        out_shape=jax.ShapeDtypeStruct((r, d), F32),
        compiler_params=_params(("arbitrary", "arbitrary")),
        name="moe_down",
    )(block_expert, next_expert_block, n_used, act, w_down)


def _combine_kernel(e0_ref, e1_ref, r0_ref, r1_ref, ps_ref, x_ref, route_ref, g_ref, y_hbm, o_ref,
                    ya_ref, yb_ref, sem, *, tb, final_norm):
    step = pl.program_id(0)

    def copies(b, slot, i):
        t = b * tb + i
        row0 = ps_ref[e0_ref[t]] + r0_ref[t]
        row1 = ps_ref[e1_ref[t]] + r1_ref[t]
        return (pltpu.make_async_copy(y_hbm.at[pl.ds(row0, 1), :],
                                      ya_ref.at[slot, pl.ds(i, 1), :], sem.at[slot]),
                pltpu.make_async_copy(y_hbm.at[pl.ds(row1, 1), :],
                                      yb_ref.at[slot, pl.ds(i, 1), :], sem.at[slot]))

    def for_rows(fn):
        def body(grp, carry):
            r0 = pl.multiple_of(grp * SUBLANES, SUBLANES)
            for k in range(SUBLANES):
                fn(r0 + k)
            return carry
        lax.fori_loop(0, tb // SUBLANES, body, 0)

    def start_block(b, slot):
        def start(i):
            c0, c1 = copies(b, slot, i)
            c0.start()
            c1.start()
        for_rows(start)

    def wait_block(b, slot):
        def wait(i):
            c0, c1 = copies(b, slot, i)
            c0.wait()
            c1.wait()
        for_rows(wait)

    @pl.when(step == 0)
    def _():
        start_block(0, 0)

    @pl.when(step + 1 < pl.num_programs(0))
    def _():
        start_block(step + 1, (step + 1) % 2)

    slot = step % 2
    wait_block(step, slot)
    w0 = route_ref[:, 2:3]
    w1 = route_ref[:, 3:4]
    x = x_ref[...] + (ya_ref[slot] * w0 + yb_ref[slot] * w1)
    if final_norm:
        ms = jnp.mean(x * x, axis=-1, keepdims=True)
        x = x * lax.rsqrt(ms + NORM_EPS) * g_ref[...]
    o_ref[...] = x


def combine(x, route, y, slots, g, final_norm, tb=COMB_ROWS):
    t, d = x.shape
    tb = min(tb, t)
    grid_spec = pltpu.PrefetchScalarGridSpec(
        num_scalar_prefetch=5,
        grid=(t // tb,),
        in_specs=[pl.BlockSpec((tb, d), lambda i, *_: (i, 0)),
                  pl.BlockSpec((tb, ROUTE_LANES), lambda i, *_: (i, 0)),
                  pl.BlockSpec((1, d), lambda i, *_: (0, 0)),
                  pl.BlockSpec(memory_space=pl.ANY)],
        out_specs=pl.BlockSpec((tb, d), lambda i, *_: (i, 0)),
        scratch_shapes=[pltpu.VMEM((2, tb, d), F32), pltpu.VMEM((2, tb, d), F32),
                        pltpu.SemaphoreType.DMA((2,))],
    )
    return pl.pallas_call(
        functools.partial(_combine_kernel, tb=tb, final_norm=final_norm),
        grid_spec=grid_spec,
        out_shape=jax.ShapeDtypeStruct((t, d), F32),
        compiler_params=_params(("arbitrary",)),
        name="moe_combine",
    )(*slots, x, route, g.reshape(1, d), y)


def _route_lanes(group_part, expert_part):
    rows = group_part.shape[0]
    gap = jnp.zeros((rows, EXPERT_LANE0 - GROUP_LANE0 - N_GROUPS), F32)
    tail = jnp.zeros((rows, ROUTE_LANES - EXPERT_LANE0 - N_EXPERTS), F32)
    return jnp.concatenate([group_part, gap, expert_part, tail], axis=1)


def _moe_layout(route, counts, blk):
    t = route.shape[0]
    ri = route[:, :8].astype(jnp.int32)
    e0, e1, rank0, rank1 = ri[:, 0], ri[:, 1], ri[:, 4], ri[:, 5]
    cnt = counts[0, EXPERT_LANE0:EXPERT_LANE0 + N_EXPERTS].astype(jnp.int32)
    padded = (cnt + blk - 1) // blk * blk
    pends = jnp.cumsum(padded)
    pstarts = pends - padded
    n_blocks = (2 * t) // blk + N_EXPERTS
    block_start = jnp.arange(n_blocks, dtype=jnp.int32) * blk
    block_expert = jnp.minimum(
        jnp.sum((block_start[:, None] >= pends[None, :]).astype(jnp.int32), axis=1), N_EXPERTS - 1)
    n_used = (pends[-1] // blk).astype(jnp.int32).reshape(1)
    block_expert = block_expert[jnp.minimum(jnp.arange(n_blocks), jnp.maximum(n_used[0] - 1, 0))]
    next_expert_block = pends[block_expert] // blk
    return (e0, e1, rank0, rank1, pstarts), block_expert, next_expert_block, n_used, n_blocks


def kernel(x, mem, positions, mix_norm_g, w_in, ret_norm_g, lru_conv_w, lru_conv_b, lru_w_a, lru_b_a, lru_w_i, lru_b_i, lru_lambda, lru_norm_g, w_out, xattn_norm_g, mem_norm_g, xattn_wq, xattn_wk, xattn_wv, xattn_wo, moe_norm_g, router_group_w, router_group_b, router_expert_w, router_expert_b, expert_w_gate, expert_w_up, expert_w_down, final_norm_g):
    b, s, d = x.shape
    depth = w_in.shape[0]
    ret_width = RET_HEADS * RET_HEAD_DIM
    lru_width = lru_conv_w.shape[-1]
    assert ret_width == lru_width and ret_width + lru_width == d
    inv_freq = ROPE_BASE ** (-jnp.arange(0, RET_HEAD_DIM, 2, dtype=F32) / RET_HEAD_DIM)
    lg = jnp.log1p(-jnp.exp2(-5.0 - jnp.arange(RET_HEADS, dtype=F32)))
    lg_rows = jnp.broadcast_to(lg[:, None, None], (RET_HEADS, 1, RET_HEAD_DIM))
    blk = min(MOE_BLK, s)
    outs = []
    for bi in range(b):
        xcur = x[bi]
        cos, sin = rope_tables(positions[bi].astype(F32), inv_freq)
        for l in range(depth):
            h = normcast(xcur, mix_norm_g[l], BF16, NORM_ROWS)
            proj = matmul_streamed([h], w_in[l], F32)
            ret = retention(proj, cos, sin, lg_rows, ret_norm_g[l])
            lru = rg_lru(proj, 4 * ret_width // lru_width, 4 * ret_width // lru_width + 1,
                         lru_conv_w[l], lru_conv_b[l], lru_w_a[l], lru_b_a[l], lru_w_i[l],
                         lru_b_i[l], lru_lambda[l], lru_norm_g[l])
            xcur = matmul_streamed([ret, lru], w_out[l], F32, res=xcur, tm=MM_STREAM_TM // 2)
            memn = normcast(mem[bi], mem_norm_g[l], BF16, NORM_ROWS)
            kk = matmul([memn], xattn_wk[l], BF16)
            vv = matmul([memn], xattn_wv[l], BF16)
            wqk, vo = xattn_fold(kk, vv, xattn_wq[l], xattn_wo[l])
            xcur = xattn(xcur, xattn_norm_g[l], wqk, vo)
            wr = _route_lanes(router_group_w[l], router_expert_w[l])
            br = _route_lanes(router_group_b[l][None], router_expert_b[l][None])
            hn, route, counts = router(xcur, moe_norm_g[l], wr, br)
            slots, block_expert, next_block, n_used, n_blocks = _moe_layout(route, counts, blk)
            xs = moe_gather(hn, slots, n_used, n_blocks, blk)
            y = moe_experts(xs, block_expert, next_block, n_used, expert_w_gate[l], expert_w_up[l],
                            expert_w_down[l], blk=blk)
            xcur = combine(xcur, route, y, slots, final_norm_g, final_norm=l == depth - 1)
        outs.append(xcur)
    return outs[0][None] if b == 1 else jnp.stack(outs, axis=0)
```

```python
import functools

import jax
import jax.numpy as jnp
from jax import lax
from jax.experimental import pallas as pl
from jax.experimental.pallas import tpu as pltpu

F32 = jnp.float32
BF16 = jnp.bfloat16

RET_HEADS = 8
RET_HEAD_DIM = 256
RET_CHUNK = 128
LRU_BLOCKS = 8
CONV_WIDTH = 4
RG_C = 8.0
ROPE_BASE = 10000.0
X_HEADS = 4
N_GROUPS = 4
EXPERTS_PER_GROUP = 8
N_EXPERTS = N_GROUPS * EXPERTS_PER_GROUP
NORM_EPS = 1e-6
GN_EPS = 1e-5

LANES = 128
SUBLANES = 8
VMEM_LIMIT = 56 * 1024 * 1024

NORM_ROWS = 512
MM_TM = 1024
MM_TN = 512
MM_STREAM_TM = 1024
MM_STREAM_TN = 1024
RET_ROWS = 512
RET_HEADS_PER_STEP = 8
LRU_ROWS = 256
ROUTE_ROWS = 256
MOE_BLK = 256
MOE_TF = 512
MOE_TN = 4096
XF_TILE = 1024
SCALAR_UNROLL = 8
COMB_ROWS = 256
ROUTE_LANES = LANES
GROUP_LANE0 = 0
EXPERT_LANE0 = 8


def _params(sem):
    return pltpu.CompilerParams(dimension_semantics=sem, vmem_limit_bytes=VMEM_LIMIT)


def _normcast_kernel(x_ref, g_ref, o_ref):
    x = x_ref[...]
    ms = jnp.mean(x * x, axis=-1, keepdims=True)
    o_ref[...] = (x * lax.rsqrt(ms + NORM_EPS) * g_ref[...]).astype(o_ref.dtype)


def normcast(x, g, out_dtype, tm):
    m, d = x.shape
    tm = min(tm, m)
    return pl.pallas_call(
        _normcast_kernel,
        grid=(m // tm,),
        in_specs=[pl.BlockSpec((tm, d), lambda i: (i, 0)),
                  pl.BlockSpec((1, d), lambda i: (0, 0))],
        out_specs=pl.BlockSpec((tm, d), lambda i: (i, 0)),
        out_shape=jax.ShapeDtypeStruct((m, d), out_dtype),
        compiler_params=_params(("arbitrary",)),
        name="normcast",
    )(x, g.reshape(1, d))


def _cast_rows(src_ref, dst_ref, rows_per_iter=256):
    k = src_ref.shape[0]
    step = min(rows_per_iter, k)

    def body(i, carry):
        r0 = pl.multiple_of(i * step, step)
        dst_ref[pl.ds(r0, step), :] = src_ref[pl.ds(r0, step), :].astype(dst_ref.dtype)
        return carry

    lax.fori_loop(0, k // step, body, 0)


def _mm_kernel(*refs, n_a, has_res):
    a_refs = refs[:n_a]
    w_ref = refs[n_a]
    res_ref = refs[n_a + 1] if has_res else None
    o_ref = refs[n_a + 1 + int(has_res)]
    wbf_ref = refs[n_a + 2 + int(has_res)]

    @pl.when(pl.program_id(1) == 0)
    def _():
        _cast_rows(w_ref, wbf_ref)

    kp = a_refs[0].shape[1]
    acc = None
    for p, a_ref in enumerate(a_refs):
        d = jnp.dot(a_ref[...], wbf_ref[p * kp:(p + 1) * kp, :], preferred_element_type=F32)
        acc = d if acc is None else acc + d
    if has_res:
        acc = acc + res_ref[...]
    o_ref[...] = acc.astype(o_ref.dtype)


def matmul(a_parts, w, out_dtype, res=None, tm=MM_TM, tn=MM_TN):
    m, kp = a_parts[0].shape
    k, n = w.shape
    assert kp * len(a_parts) == k
    tm = min(tm, m)
    tn = min(tn, n)
    in_specs = [pl.BlockSpec((tm, kp), lambda j, i: (i, 0)) for _ in a_parts]
    in_specs.append(pl.BlockSpec((k, tn), lambda j, i: (0, j)))
    args = list(a_parts) + [w]
    if res is not None:
        in_specs.append(pl.BlockSpec((tm, tn), lambda j, i: (i, j)))
        args.append(res)
    return pl.pallas_call(
        functools.partial(_mm_kernel, n_a=len(a_parts), has_res=res is not None),
        grid=(n // tn, m // tm),
        in_specs=in_specs,
        out_specs=pl.BlockSpec((tm, tn), lambda j, i: (i, j)),
        out_shape=jax.ShapeDtypeStruct((m, n), out_dtype),
        scratch_shapes=[pltpu.VMEM((k, tn), BF16)],
        compiler_params=_params(("arbitrary", "arbitrary")),
        name="matmul",
    )(*args)


def _mm_stream_kernel(*refs, n_a, has_res):
    a_refs = refs[:n_a]
    w_hbm = refs[n_a]
    res_ref = refs[n_a + 1] if has_res else None
    o_ref = refs[n_a + 1 + int(has_res)]
    wbf_ref, stage_ref, sem = refs[n_a + 2 + int(has_res):]
    j = pl.program_id(0)
    i = pl.program_id(1)
    n_j = pl.num_programs(0)
    n_i = pl.num_programs(1)
    kc, tn = stage_ref.shape[1:]
    n_chunks = wbf_ref.shape[1] // kc

    def chunk_copy(col_tile, c, slot):
        rows = pl.ds(pl.multiple_of(c * kc, kc), kc)
        cols = pl.ds(pl.multiple_of(col_tile * tn, tn), tn)
        return pltpu.make_async_copy(w_hbm.at[rows, cols], stage_ref.at[slot], sem.at[slot])

    def cast_chunk(buf, c, slot):
        rows = pl.ds(pl.multiple_of(c * kc, kc), kc)
        wbf_ref[buf, rows, :] = stage_ref[slot].astype(BF16)

    @pl.when((j == 0) & (i == 0))
    def _():
        chunk_copy(0, 0, 0).start()
        for c in range(n_chunks):
            if c + 1 < n_chunks:
                chunk_copy(0, c + 1, (c + 1) % 2).start()
            chunk_copy(0, c, c % 2).wait()
            cast_chunk(0, c, c % 2)

    cur = j % 2
    kp = a_refs[0].shape[1]
    acc = None
    for p, a_ref in enumerate(a_refs):
        d = jnp.dot(a_ref[...], wbf_ref[cur, p * kp:(p + 1) * kp, :], preferred_element_type=F32)
        acc = d if acc is None else acc + d
    if has_res:
        acc = acc + res_ref[...]
    o_ref[...] = acc.astype(o_ref.dtype)

    @pl.when(j + 1 < n_j)
    def _():
        slot = i % 2

        @pl.when(i == 0)
        def _():
            chunk_copy(j + 1, 0, 0).start()

        chunk_copy(j + 1, i, slot).wait()

        @pl.when(i + 1 < n_i)
        def _():
            chunk_copy(j + 1, i + 1, 1 - slot).start()

        cast_chunk(1 - cur, i, slot)


def matmul_streamed(a_parts, w, out_dtype, res=None, tm=MM_STREAM_TM, tn=MM_STREAM_TN):
    m, kp = a_parts[0].shape
    k, n = w.shape
    assert kp * len(a_parts) == k
    tm = min(tm, m)
    tn = min(tn, n)
    n_i = m // tm
    kc = k // n_i
    assert kc * n_i == k and kc % SUBLANES == 0
    in_specs = [pl.BlockSpec((tm, kp), lambda j, i: (i, 0)) for _ in a_parts]
    in_specs.append(pl.BlockSpec(memory_space=pl.ANY))
    args = list(a_parts) + [w]
    if res is not None:
        in_specs.append(pl.BlockSpec((tm, tn), lambda j, i: (i, j)))
        args.append(res)
    return pl.pallas_call(
        functools.partial(_mm_stream_kernel, n_a=len(a_parts), has_res=res is not None),
        grid=(n // tn, n_i),
        in_specs=in_specs,
        out_specs=pl.BlockSpec((tm, tn), lambda j, i: (i, j)),
        out_shape=jax.ShapeDtypeStruct((m, n), out_dtype),
        scratch_shapes=[pltpu.VMEM((2, k, tn), BF16),
                        pltpu.VMEM((2, kc, tn), F32),
                        pltpu.SemaphoreType.DMA((2,))],
        compiler_params=_params(("arbitrary", "arbitrary")),
        name="matmul_streamed",
    )(*args)


def _rope_kernel(pos_ref, invf_ref, cos_ref, sin_ref):
    ang = pos_ref[...] * invf_ref[...]
    cos_ref[...] = jnp.cos(ang)
    sin_ref[...] = jnp.sin(ang)


def rope_tables(pos_f, inv_freq, tm=512):
    s = pos_f.shape[0]
    hd = inv_freq.shape[0]
    tm = min(tm, s)
    return pl.pallas_call(
        _rope_kernel,
        grid=(s // tm,),
        in_specs=[pl.BlockSpec((tm, 1), lambda i: (i, 0)),
                  pl.BlockSpec((1, hd), lambda i: (0, 0))],
        out_specs=[pl.BlockSpec((tm, hd), lambda i: (i, 0))] * 2,
        out_shape=[jax.ShapeDtypeStruct((s, hd), F32)] * 2,
        compiler_params=_params(("arbitrary",)),
        name="rope_tables",
    )(pos_f.reshape(s, 1), inv_freq.reshape(1, hd))


def _ret_kernel(q_ref, k_ref, v_ref, g_ref, cos_ref, sin_ref, lg_ref, gn_ref, o_ref, r_ref,
                decay_ref, xi_ref, zeta_ref, *, n_chunks, hpb):
    c = RET_CHUNK
    dk = RET_HEAD_DIM
    half = dk // 2
    scale = dk ** -0.5

    @pl.when(pl.program_id(1) == 0)
    def _():
        r_ref[...] = jnp.zeros_like(r_ref)
        row = lax.broadcasted_iota(jnp.int32, (c, c), 0).astype(F32)
        col = lax.broadcasted_iota(jnp.int32, (c, c), 1).astype(F32)
        diff = row - col
        rowk = lax.broadcasted_iota(jnp.int32, (c, dk), 0).astype(F32)
        for hh in range(hpb):
            lg = lg_ref[hh]
            decay_ref[hh] = jnp.where(diff >= 0, jnp.exp(lg[:, :c] * jnp.maximum(diff, 0.0)), 0.0)
            xi_ref[hh] = jnp.exp(lg * (rowk + 1.0))
            zeta_ref[hh] = jnp.exp(lg * (c - 1.0 - rowk))

    def rope(t, cos, sin):
        t1 = t[:, :half]
        t2 = t[:, half:]
        return jnp.concatenate([t1 * cos - t2 * sin, t1 * sin + t2 * cos], axis=-1)

    def body(j, carry):
        r0 = pl.multiple_of(j * c, c)
        rows = pl.ds(r0, c)
        cos = cos_ref[rows, :]
        sin = sin_ref[rows, :]
        for hh in range(hpb):
            cs = slice(hh * dk, (hh + 1) * dk)
            qr = rope(q_ref[rows, cs], cos, sin)
            kr = rope(k_ref[rows, cs], cos, sin) * scale
            qb = qr.astype(BF16)
            kb = kr.astype(BF16)
            vb = v_ref[rows, cs].astype(BF16)
            state = r_ref[hh]
            inner = lax.dot_general(qb, kb, (((1,), (1,)), ((), ())),
                                    preferred_element_type=F32) * decay_ref[hh]
            o = (jnp.dot(inner.astype(BF16), vb, preferred_element_type=F32)
                 + jnp.dot(qb, state.astype(BF16), preferred_element_type=F32) * xi_ref[hh])
            kz = (kr * zeta_ref[hh]).astype(BF16)
            chunk_decay = jnp.exp(lg_ref[hh] * c)
            r_ref[hh] = state * chunk_decay + lax.dot_general(
                kz, vb, (((0,), (0,)), ((), ())), preferred_element_type=F32)
            mu = jnp.mean(o, axis=-1, keepdims=True)
            oc = o - mu
            var = jnp.mean(oc * oc, axis=-1, keepdims=True)
            on = oc * lax.rsqrt(var + GN_EPS) * gn_ref[hh]
            g = g_ref[rows, cs]
            o_ref[rows, cs] = (on * (g * (1.0 / (1.0 + jnp.exp(-g))))).astype(o_ref.dtype)
        return carry

    lax.fori_loop(0, n_chunks, body, 0, unroll=2)


def retention(proj, cos, sin, lg_rows, gn_g, tr=RET_ROWS, hpb=RET_HEADS_PER_STEP):
    s = proj.shape[0]
    dk = RET_HEAD_DIM
    h = RET_HEADS
    tr = min(tr, s)
    c = RET_CHUNK
    w = hpb * dk

    def col(base):
        return pl.BlockSpec((tr, w), lambda hg, ci, base=base: (ci, base // hpb + hg))

    per_head = pl.BlockSpec((hpb, 1, dk), lambda hg, ci: (hg, 0, 0))
    return pl.pallas_call(
        functools.partial(_ret_kernel, n_chunks=tr // c, hpb=hpb),
        grid=(h // hpb, s // tr),
        in_specs=[col(0), col(h), col(2 * h), col(3 * h),
                  pl.BlockSpec((tr, dk // 2), lambda hg, ci: (ci, 0)),
                  pl.BlockSpec((tr, dk // 2), lambda hg, ci: (ci, 0)),
                  per_head, per_head],
        out_specs=pl.BlockSpec((tr, w), lambda hg, ci: (ci, hg)),
        out_shape=jax.ShapeDtypeStruct((s, h * dk), BF16),
        scratch_shapes=[pltpu.VMEM((hpb, dk, dk), F32),
                        pltpu.VMEM((hpb, c, c), F32),
                        pltpu.VMEM((hpb, c, dk), F32),
                        pltpu.VMEM((hpb, c, dk), F32)],
        compiler_params=_params(("arbitrary", "arbitrary")),
        name="retention",
    )(proj, proj, proj, proj, cos, sin, lg_rows, gn_g.reshape(h, 1, dk))


def _sigmoid(x):
    return 1.0 / (1.0 + jnp.exp(-x))


def _lru_kernel(xb_ref, gb_ref, cw_ref, cb_ref, wa_ref, ba_ref, wi_ref, bi_ref, lam_ref, og_ref,
                o_ref, tail_ref, xs_ref, hs_ref, h_ref, wabf_ref, wibf_ref):
    tr, cdim = xb_ref.shape
    nb = wa_ref.shape[0]
    bd = cdim // nb
    ph = SUBLANES
    ng = tr // ph

    @pl.when(pl.program_id(0) == 0)
    def _():
        tail_ref[...] = jnp.zeros_like(tail_ref)
        h_ref[...] = jnp.zeros_like(h_ref)
        wabf_ref[...] = wa_ref[...].astype(BF16)
        wibf_ref[...] = wi_ref[...].astype(BF16)

    lam = lam_ref[...]
    sp = jnp.maximum(-lam, 0.0) + jnp.log1p(jnp.exp(-jnp.abs(lam)))
    rowg = lax.broadcasted_iota(jnp.int32, (ng, bd), 0)
    lpb = bd // LANES
    for c in range(cdim // LANES):
        xs_ref[c] = xb_ref[:, c * LANES:(c + 1) * LANES]

    def phase_rows(ref, n, p):
        return jnp.concatenate([ref[n * lpb + c, pl.ds(p, ng, stride=ph), :] for c in range(lpb)],
                               axis=1)

    for n in range(nb):
        cs = slice(n * bd, (n + 1) * bd)
        x = [phase_rows(xs_ref, n, p) for p in range(ph)]

        def prev_group(p):
            return jnp.where(rowg == 0, tail_ref[p:p + 1, cs], pltpu.roll(x[p], 1, 0))

        back = {-k: prev_group(ph - k) for k in range(1, CONV_WIDTH)}

        def xat(p):
            return x[p] if p >= 0 else back[p]

        xc = []
        for p in range(ph):
            acc = cb_ref[:, cs] + cw_ref[CONV_WIDTH - 1:CONV_WIDTH, cs] * xat(p)
            for k in range(1, CONV_WIDTH):
                acc = acc + cw_ref[CONV_WIDTH - 1 - k:CONV_WIDTH - k, cs] * xat(p - k)
            xc.append(acc)
        xg = jnp.concatenate(xc, axis=0)
        xgb = xg.astype(BF16)
        r = _sigmoid(jnp.dot(xgb, wabf_ref[n], preferred_element_type=F32) + ba_ref[:, cs])
        ig = _sigmoid(jnp.dot(xgb, wibf_ref[n], preferred_element_type=F32) + bi_ref[:, cs])
        log_a = (-RG_C * r) * sp[:, cs]
        a = jnp.exp(log_a)
        b = jnp.sqrt(-jnp.tanh(log_a) * (a * a + 1.0)) * (ig * xg)

        cum_a = [a[0:ng]]
        cum_b = [b[0:ng]]
        for p in range(1, ph):
            ap = a[p * ng:(p + 1) * ng]
            cum_b.append(ap * cum_b[-1] + b[p * ng:(p + 1) * ng])
            cum_a.append(ap * cum_a[-1])
        sa, sb = cum_a[-1], cum_b[-1]
        d = 1
        while d < ng:
            keep = rowg >= d
            sa_sh = pltpu.roll(sa, d, 0)
            sb_sh = pltpu.roll(sb, d, 0)
            sb = jnp.where(keep, sa * sb_sh + sb, sb)
            sa = jnp.where(keep, sa * sa_sh, sa)
            d *= 2
        h_in = h_ref[:, cs]
        h_end = sa * h_in + sb
        h_prev = jnp.where(rowg == 0, h_in, pltpu.roll(h_end, 1, 0))
        for p in range(ph):
            hp = cum_a[p] * h_prev + cum_b[p]
            for c in range(lpb):
                hs_ref[n * lpb + c, pl.ds(p, ng, stride=ph), :] = hp[:, c * LANES:(c + 1) * LANES]
        h_ref[:, cs] = h_end[ng - 1:ng, :]

    gb = gb_ref[...]
    gelu = 0.5 * gb * (1.0 + jnp.tanh(0.7978845608028654 * (gb + 0.044715 * (gb * gb * gb))))
    y = jnp.concatenate([hs_ref[c] for c in range(cdim // LANES)], axis=1) * gelu
    ms = jnp.mean(y * y, axis=-1, keepdims=True)
    o_ref[...] = (y * lax.rsqrt(ms + NORM_EPS) * og_ref[...]).astype(o_ref.dtype)
    tail_ref[...] = xb_ref[tr - ph:tr, :]


def rg_lru(proj, xb_block, gb_block, conv_w, conv_b, w_a, b_a, w_i, b_i, lam, out_g, tr=LRU_ROWS):
    s = proj.shape[0]
    cdim = conv_w.shape[1]
    nb, bd, _ = w_a.shape
    tr = min(tr, s)
    vec = pl.BlockSpec((1, cdim), lambda i: (0, 0))
    wspec = pl.BlockSpec((nb, bd, bd), lambda i: (0, 0, 0))
    return pl.pallas_call(
        _lru_kernel,
        grid=(s // tr,),
        in_specs=[pl.BlockSpec((tr, cdim), lambda i: (i, xb_block)),
                  pl.BlockSpec((tr, cdim), lambda i: (i, gb_block)),
                  pl.BlockSpec((CONV_WIDTH, cdim), lambda i: (0, 0)),
                  vec, wspec, vec, wspec, vec, vec, vec],
        out_specs=pl.BlockSpec((tr, cdim), lambda i: (i, 0)),
        out_shape=jax.ShapeDtypeStruct((s, cdim), BF16),
        scratch_shapes=[pltpu.VMEM((SUBLANES, cdim), F32),
                        pltpu.VMEM((cdim // LANES, tr, LANES), F32),
                        pltpu.VMEM((cdim // LANES, tr, LANES), F32),
                        pltpu.VMEM((1, cdim), F32),
                        pltpu.VMEM((nb, bd, bd), BF16),
                        pltpu.VMEM((nb, bd, bd), BF16)],
        compiler_params=_params(("arbitrary",)),
        name="rg_lru",
    )(proj, proj, conv_w, conv_b.reshape(1, cdim), w_a, b_a.reshape(1, cdim), w_i,
      b_i.reshape(1, cdim), lam.reshape(1, cdim), out_g.reshape(1, cdim))


def _wqk_kernel(wq_ref, k_ref, o_ref):
    o_ref[...] = lax.dot_general(wq_ref[...].astype(BF16), k_ref[...], (((1,), (1,)), ((), ())),
                                 preferred_element_type=F32).astype(o_ref.dtype)


def _vo_kernel(v_ref, wo_ref, o_ref):
    o_ref[...] = jnp.dot(v_ref[...], wo_ref[...].astype(BF16),
                         preferred_element_type=F32).astype(o_ref.dtype)


def xattn_fold(k, v, wq, wo, tile=XF_TILE):
    mlen, d = k.shape
    hd = d // X_HEADS
    tile = min(tile, d)
    wqk = pl.pallas_call(
        _wqk_kernel,
        grid=(X_HEADS, d // tile),
        in_specs=[pl.BlockSpec((tile, hd), lambda h, r: (r, h)),
                  pl.BlockSpec((mlen, hd), lambda h, r: (0, h))],
        out_specs=pl.BlockSpec((tile, mlen), lambda h, r: (r, h)),
        out_shape=jax.ShapeDtypeStruct((d, X_HEADS * mlen), BF16),
        compiler_params=_params(("arbitrary", "arbitrary")),
        name="xattn_wqk",
    )(wq, k)
    vo = pl.pallas_call(
        _vo_kernel,
        grid=(X_HEADS, d // tile),
        in_specs=[pl.BlockSpec((mlen, hd), lambda h, j: (0, h)),
                  pl.BlockSpec((hd, tile), lambda h, j: (h, j))],
        out_specs=pl.BlockSpec((mlen, tile), lambda h, j: (h, j)),
        out_shape=jax.ShapeDtypeStruct((X_HEADS * mlen, d), BF16),
        compiler_params=_params(("arbitrary", "arbitrary")),
        name="xattn_vo",
    )(v, wo)
    return wqk, vo


def _xattn_kernel(x_ref, g_ref, wqk_ref, vo_ref, o_ref):
    d = x_ref.shape[1]
    mlen = wqk_ref.shape[1] // X_HEADS
    scale = (d // X_HEADS) ** -0.5
    x = x_ref[...]
    ms = jnp.mean(x * x, axis=-1, keepdims=True)
    h = (x * lax.rsqrt(ms + NORM_EPS) * g_ref[...]).astype(BF16)
    s = jnp.dot(h, wqk_ref[...], preferred_element_type=F32) * scale
    ps = []
    for hh in range(X_HEADS):
        sh = s[:, hh * mlen:(hh + 1) * mlen]
        m = jnp.max(sh, axis=-1, keepdims=True)
        e = jnp.exp(sh - m)
        ps.append((e / jnp.sum(e, axis=-1, keepdims=True)).astype(BF16))
    p = jnp.concatenate(ps, axis=1)
    o_ref[...] = x + jnp.dot(p, vo_ref[...], preferred_element_type=F32)


def _split_bf16(a):
    hi = a.astype(BF16)
    return hi, (a - hi.astype(F32)).astype(BF16)


def _route_rows(x, g_ref, wr_ref, br_ref, h_ref, route_ref, counts_ref, carry_ref, wsplit_ref):
    tm = x.shape[0]
    nl = ROUTE_LANES

    @pl.when(pl.program_id(0) == 0)
    def _():
        carry_ref[...] = jnp.zeros_like(carry_ref)
        w_hi, w_lo = _split_bf16(wr_ref[...])
        wsplit_ref[:, :nl] = w_hi
        wsplit_ref[:, nl:] = w_lo

    ms = jnp.mean(x * x, axis=-1, keepdims=True)
    h = x * lax.rsqrt(ms + NORM_EPS) * g_ref[...]
    h_ref[...] = h
    h_hi, h_lo = _split_bf16(h)
    both = jnp.dot(h_hi, wsplit_ref[...], preferred_element_type=F32)
    cross = jnp.dot(h_lo, wsplit_ref[:, :nl], preferred_element_type=F32)
    logits = both[:, :nl] + (both[:, nl:] + cross) + br_ref[...]
    lane = lax.broadcasted_iota(jnp.int32, (tm, ROUTE_LANES), 1).astype(F32)
    neg = -jnp.inf
    big = float(ROUTE_LANES)

    gmask = (lane >= GROUP_LANE0) & (lane < GROUP_LANE0 + N_GROUPS)
    gl = jnp.where(gmask, logits, neg)
    gmax = jnp.max(gl, axis=-1, keepdims=True)
    gsum = jnp.sum(jnp.where(gmask, jnp.exp(gl - gmax), 0.0), axis=-1, keepdims=True)
    g_val = 1.0 / gsum
    g_idx = jnp.min(jnp.where(gl == gmax, lane, big), axis=-1, keepdims=True) - GROUP_LANE0

    lo = EXPERT_LANE0 + g_idx * EXPERTS_PER_GROUP
    emask = (lane >= lo) & (lane < lo + EXPERTS_PER_GROUP)
    el = jnp.where(emask, logits, neg)
    t1 = jnp.max(el, axis=-1, keepdims=True)
    i1 = jnp.min(jnp.where(emask & (el == t1), lane, big), axis=-1, keepdims=True)
    emask2 = emask & (lane != i1)
    el2 = jnp.where(emask2, logits, neg)
    t2 = jnp.max(el2, axis=-1, keepdims=True)
    i2 = jnp.min(jnp.where(emask2 & (el2 == t2), lane, big), axis=-1, keepdims=True)
    dexp = jnp.exp(t2 - t1)
    w0 = g_val / (1.0 + dexp)
    w1 = g_val * dexp / (1.0 + dexp)

    sel1 = lane == i1
    sel2 = lane == i2
    onehot = jnp.where(sel1 | sel2, 1.0, 0.0)
    rr = lax.broadcasted_iota(jnp.int32, (tm, tm), 0)
    cc = lax.broadcasted_iota(jnp.int32, (tm, tm), 1)
    tri = jnp.where(cc < rr, 1.0, 0.0).astype(BF16)
    prefix = jnp.dot(tri, onehot.astype(BF16), preferred_element_type=F32) + carry_ref[...]
    rank0 = jnp.sum(jnp.where(sel1, prefix, 0.0), axis=-1, keepdims=True)
    rank1 = jnp.sum(jnp.where(sel2, prefix, 0.0), axis=-1, keepdims=True)
    total = carry_ref[...] + jnp.sum(onehot, axis=0, keepdims=True)
    carry_ref[...] = total
    counts_ref[...] = total

    e0 = i1 - EXPERT_LANE0
    e1 = i2 - EXPERT_LANE0
    route = jnp.where(lane == 0, e0, 0.0)
    route = jnp.where(lane == 1, e1, route)
    route = jnp.where(lane == 2, w0, route)
    route = jnp.where(lane == 3, w1, route)
    route = jnp.where(lane == 4, rank0, route)
    route = jnp.where(lane == 5, rank1, route)
    route_ref[...] = route


def _xattn_route_kernel(x_ref, g_ref, wqk_ref, vo_ref, gm_ref, wr_ref, br_ref,
                        o_ref, h_ref, route_ref, counts_ref, carry_ref, wsplit_ref):
    _xattn_kernel(x_ref, g_ref, wqk_ref, vo_ref, o_ref)
    _route_rows(o_ref[...], gm_ref, wr_ref, br_ref, h_ref, route_ref, counts_ref, carry_ref,
                wsplit_ref)


def xattn_route(x, g, wqk, vo, g_moe, wr, br, tm=ROUTE_ROWS):
    t, d = x.shape
    tm = min(tm, t)
    row = lambda i: (i, 0)
    fixed = lambda i: (0, 0)
    return pl.pallas_call(
        _xattn_route_kernel,
        grid=(t // tm,),
        in_specs=[pl.BlockSpec((tm, d), row),
                  pl.BlockSpec((1, d), fixed),
                  pl.BlockSpec(wqk.shape, fixed, pipeline_mode=pl.Buffered(1)),
                  pl.BlockSpec(vo.shape, fixed, pipeline_mode=pl.Buffered(1)),
                  pl.BlockSpec((1, d), fixed),
                  pl.BlockSpec((d, ROUTE_LANES), fixed),
                  pl.BlockSpec((1, ROUTE_LANES), fixed)],
        out_specs=[pl.BlockSpec((tm, d), row),
                   pl.BlockSpec((tm, d), row),
                   pl.BlockSpec((tm, ROUTE_LANES), row),
                   pl.BlockSpec((1, ROUTE_LANES), fixed)],
        out_shape=[jax.ShapeDtypeStruct((t, d), F32),
                   jax.ShapeDtypeStruct((t, d), F32),
                   jax.ShapeDtypeStruct((t, ROUTE_LANES), F32),
                   jax.ShapeDtypeStruct((1, ROUTE_LANES), F32)],
        scratch_shapes=[pltpu.VMEM((1, ROUTE_LANES), F32),
                        pltpu.VMEM((d, 2 * ROUTE_LANES), BF16)],
        compiler_params=_params(("arbitrary",)),
        name="xattn_route",
    )(x, g.reshape(1, d), wqk, vo, g_moe.reshape(1, d), wr, br)


def _gather_kernel(e0_ref, e1_ref, r0_ref, r1_ref, ps_ref, nu_ref, x_hbm, xs_ref,
                   rowtok_ref, buf_ref, sem, *, blk, n_tok):
    i = pl.program_id(0)
    nu = nu_ref[0]

    def row_copy(b, slot, r):
        tok = rowtok_ref[b * blk + r]
        return pltpu.make_async_copy(x_hbm.at[pl.ds(tok, 1), :],
                                     buf_ref.at[slot, pl.ds(r, 1), :], sem.at[slot])

    def for_rows(fn):
        def body(grp, carry):
            r0 = pl.multiple_of(grp * SUBLANES, SUBLANES)
            for k in range(SUBLANES):
                fn(r0 + k)
            return carry
        lax.fori_loop(0, blk // SUBLANES, body, 0)

    def start_block(b, slot):
        for_rows(lambda r: row_copy(b, slot, r).start())

    def wait_block(b, slot):
        for_rows(lambda r: row_copy(b, slot, r).wait())

    @pl.when(i == 0)
    def _():
        n_rows = rowtok_ref.shape[0]
        for base in range(0, n_rows, n_tok):
            def init(r, carry, base=base):
                rowtok_ref[base + r] = r
                return carry
            lax.fori_loop(0, min(n_tok, n_rows - base), init, 0, unroll=SCALAR_UNROLL)

        def fill(t, carry):
            rowtok_ref[ps_ref[e0_ref[t]] + r0_ref[t]] = t
            rowtok_ref[ps_ref[e1_ref[t]] + r1_ref[t]] = t
            return carry
        lax.fori_loop(0, n_tok, fill, 0, unroll=SCALAR_UNROLL)
        start_block(0, 0)

    @pl.when(i + 1 < nu)
    def _():
        start_block(i + 1, (i + 1) % 2)

    @pl.when(i < nu)
    def _():
        slot = i % 2
        wait_block(i, slot)
        xs_ref[...] = buf_ref[slot].astype(xs_ref.dtype)

    @pl.when(i >= nu)
    def _():
        xs_ref[...] = jnp.zeros_like(xs_ref)


def moe_gather(hn, slots, n_used, n_blocks, blk):
    t, d = hn.shape
    grid_spec = pltpu.PrefetchScalarGridSpec(
        num_scalar_prefetch=6,
        grid=(n_blocks,),
        in_specs=[pl.BlockSpec(memory_space=pl.ANY)],
        out_specs=pl.BlockSpec((blk, d), lambda i, *_: (i, 0)),
        scratch_shapes=[pltpu.SMEM((n_blocks * blk,), jnp.int32),
                        pltpu.VMEM((2, blk, d), F32),
                        pltpu.SemaphoreType.DMA((2,))],
    )
    return pl.pallas_call(
        functools.partial(_gather_kernel, blk=blk, n_tok=t),
        grid_spec=grid_spec,
        out_shape=jax.ShapeDtypeStruct((n_blocks * blk, d), BF16),
        compiler_params=_params(("arbitrary",)),
        name="moe_gather",
    )(*slots, n_used, hn)


def _expert_changed(be_ref, i):
    prev = be_ref[jnp.maximum(i - 1, 0)]
    return (i == 0) | (be_ref[i] != prev)


def _stream_expert_weights(w_hbms, col0, be_ref, nx_ref, nu, i, wst_ref, wbf_ref, sem, slot_ref):
    tn = wst_ref.shape[-1]

    def copies(e, slot):
        return [pltpu.make_async_copy(w.at[e, :, pl.ds(col0, tn)], wst_ref.at[slot, l], sem.at[slot])
                for l, w in enumerate(w_hbms)]

    @pl.when(i == 0)
    def _():
        slot_ref[0] = 0
        for c in copies(be_ref[0], 0):
            c.start()

    @pl.when(_expert_changed(be_ref, i))
    def _():
        slot = slot_ref[0]
        for c in copies(be_ref[i], slot):
            c.wait()
        nxt = nx_ref[i]

        @pl.when(nxt < nu)
        def _():
            for c in copies(be_ref[jnp.minimum(nxt, be_ref.shape[0] - 1)], 1 - slot):
                c.start()

        for l in range(len(w_hbms)):
            _cast_rows(wst_ref.at[slot, l], wbf_ref.at[l])
        slot_ref[0] = 1 - slot


def _moe_up_kernel(be_ref, nx_ref, nu_ref, xs_ref, wg_hbm, wu_hbm, act_ref,
                   wst_ref, wbf_ref, sem, slot_ref):
    j = pl.program_id(0)
    i = pl.program_id(1)
    nu = nu_ref[0]
    tf = act_ref.shape[1]

    @pl.when(i < nu)
    def _():
        _stream_expert_weights([wg_hbm, wu_hbm], pl.multiple_of(j * tf, tf), be_ref, nx_ref, nu, i,
                               wst_ref, wbf_ref, sem, slot_ref)
        x = xs_ref[...]
        gate = jnp.dot(x, wbf_ref[0], preferred_element_type=F32)
        up = jnp.dot(x, wbf_ref[1], preferred_element_type=F32)
        act_ref[...] = (gate * _sigmoid(gate) * up).astype(act_ref.dtype)

    @pl.when(i >= nu)
    def _():
        act_ref[...] = jnp.zeros_like(act_ref)


def _moe_down_kernel(be_ref, nx_ref, nu_ref, act_ref, wd_hbm, y_ref, wst_ref, wbf_ref, sem, slot_ref):
    j = pl.program_id(0)
    i = pl.program_id(1)
    nu = nu_ref[0]
    tn = y_ref.shape[1]

    @pl.when(i < nu)
    def _():
        _stream_expert_weights([wd_hbm], pl.multiple_of(j * tn, tn), be_ref, nx_ref, nu, i,
                               wst_ref, wbf_ref, sem, slot_ref)
        y_ref[...] = jnp.dot(act_ref[...], wbf_ref[0], preferred_element_type=F32)

    @pl.when(i >= nu)
    def _():
        y_ref[...] = jnp.zeros_like(y_ref)


def moe_experts(xs, block_expert, next_expert_block, n_used, w_gate, w_up, w_down,
                blk=MOE_BLK, tf=MOE_TF, tn=MOE_TN):
    r, dw = xs.shape
    _, d, f = w_gate.shape
    n_blocks = r // blk
    tf = min(tf, f)
    tn = min(tn, d)

    def used(i, nu):
        return jnp.minimum(i, jnp.maximum(nu[0] - 1, 0))

    def stream_scratch(n_mats, k, n):
        return [pltpu.VMEM((2, n_mats, k, n), F32),
                pltpu.VMEM((n_mats, k, n), BF16),
                pltpu.SemaphoreType.DMA((2,)),
                pltpu.SMEM((1,), jnp.int32)]

    up_spec = pltpu.PrefetchScalarGridSpec(
        num_scalar_prefetch=3,
        grid=(f // tf, n_blocks),
        in_specs=[pl.BlockSpec((blk, dw), lambda j, i, be, nx, nu: (used(i, nu), 0)),
                  pl.BlockSpec(memory_space=pl.ANY),
                  pl.BlockSpec(memory_space=pl.ANY)],
        out_specs=pl.BlockSpec((blk, tf), lambda j, i, be, nx, nu: (i, j)),
        scratch_shapes=stream_scratch(2, d, tf),
    )
    act = pl.pallas_call(
        _moe_up_kernel,
        grid_spec=up_spec,
        out_shape=jax.ShapeDtypeStruct((r, f), BF16),
        compiler_params=_params(("arbitrary", "arbitrary")),
        name="moe_up",
    )(block_expert, next_expert_block, n_used, xs, w_gate, w_up)
    down_spec = pltpu.PrefetchScalarGridSpec(
        num_scalar_prefetch=3,
        grid=(d // tn, n_blocks),
        in_specs=[pl.BlockSpec((blk, f), lambda j, i, be, nx, nu: (used(i, nu), 0)),
                  pl.BlockSpec(memory_space=pl.ANY)],
        out_specs=pl.BlockSpec((blk, tn), lambda j, i, be, nx, nu: (i, j)),
        scratch_shapes=stream_scratch(1, f, tn),
    )
    return pl.pallas_call(
        _moe_down_kernel,
        grid_spec=down_spec,
        out_shape=jax.ShapeDtypeStruct((r, d), F32),
        compiler_params=_params(("arbitrary", "arbitrary")),
        name="moe_down",
    )(block_expert, next_expert_block, n_used, act, w_down)


def _combine_kernel(e0_ref, e1_ref, r0_ref, r1_ref, ps_ref, x_ref, route_ref, g_ref, y_hbm, o_ref,
                    ya_ref, yb_ref, sem, *, tb, final_norm):
    step = pl.program_id(0)

    def copies(b, slot, i):
        t = b * tb + i
        row0 = ps_ref[e0_ref[t]] + r0_ref[t]
        row1 = ps_ref[e1_ref[t]] + r1_ref[t]
        return (pltpu.make_async_copy(y_hbm.at[pl.ds(row0, 1), :],
                                      ya_ref.at[slot, pl.ds(i, 1), :], sem.at[slot]),
                pltpu.make_async_copy(y_hbm.at[pl.ds(row1, 1), :],
                                      yb_ref.at[slot, pl.ds(i, 1), :], sem.at[slot]))

    def for_rows(fn):
        def body(grp, carry):
            r0 = pl.multiple_of(grp * SUBLANES, SUBLANES)
            for k in range(SUBLANES):
                fn(r0 + k)
            return carry
        lax.fori_loop(0, tb // SUBLANES, body, 0)

    def start_block(b, slot):
        def start(i):
            c0, c1 = copies(b, slot, i)
            c0.start()
            c1.start()
        for_rows(start)

    def wait_block(b, slot):
        def wait(i):
            c0, c1 = copies(b, slot, i)
            c0.wait()
            c1.wait()
        for_rows(wait)

    @pl.when(step == 0)
    def _():
        start_block(0, 0)

    @pl.when(step + 1 < pl.num_programs(0))
    def _():
        start_block(step + 1, (step + 1) % 2)

    slot = step % 2
    wait_block(step, slot)
    w0 = route_ref[:, 2:3]
    w1 = route_ref[:, 3:4]
    x = x_ref[...] + (ya_ref[slot] * w0 + yb_ref[slot] * w1)
    if final_norm:
        ms = jnp.mean(x * x, axis=-1, keepdims=True)
        x = x * lax.rsqrt(ms + NORM_EPS) * g_ref[...]
    o_ref[...] = x


def combine(x, route, y, slots, g, final_norm, tb=COMB_ROWS):
    t, d = x.shape
    tb = min(tb, t)
    grid_spec = pltpu.PrefetchScalarGridSpec(
        num_scalar_prefetch=5,
        grid=(t // tb,),
        in_specs=[pl.BlockSpec((tb, d), lambda i, *_: (i, 0)),
                  pl.BlockSpec((tb, ROUTE_LANES), lambda i, *_: (i, 0)),
                  pl.BlockSpec((1, d), lambda i, *_: (0, 0)),
                  pl.BlockSpec(memory_space=pl.ANY)],
        out_specs=pl.BlockSpec((tb, d), lambda i, *_: (i, 0)),
        scratch_shapes=[pltpu.VMEM((2, tb, d), F32), pltpu.VMEM((2, tb, d), F32),
                        pltpu.SemaphoreType.DMA((2,))],
    )
    return pl.pallas_call(
        functools.partial(_combine_kernel, tb=tb, final_norm=final_norm),
        grid_spec=grid_spec,
        out_shape=jax.ShapeDtypeStruct((t, d), F32),
        compiler_params=_params(("arbitrary",)),
        name="moe_combine",
    )(*slots, x, route, g.reshape(1, d), y)


def _route_lanes(group_part, expert_part):
    rows = group_part.shape[0]
    gap = jnp.zeros((rows, EXPERT_LANE0 - GROUP_LANE0 - N_GROUPS), F32)
    tail = jnp.zeros((rows, ROUTE_LANES - EXPERT_LANE0 - N_EXPERTS), F32)
    return jnp.concatenate([group_part, gap, expert_part, tail], axis=1)


def _moe_layout(route, counts, blk):
    t = route.shape[0]
    ri = route[:, :8].astype(jnp.int32)
    e0, e1, rank0, rank1 = ri[:, 0], ri[:, 1], ri[:, 4], ri[:, 5]
    cnt = counts[0, EXPERT_LANE0:EXPERT_LANE0 + N_EXPERTS].astype(jnp.int32)
    padded = (cnt + blk - 1) // blk * blk
    pends = jnp.cumsum(padded)
    pstarts = pends - padded
    n_blocks = (2 * t) // blk + N_EXPERTS
    block_start = jnp.arange(n_blocks, dtype=jnp.int32) * blk
    block_expert = jnp.minimum(
        jnp.sum((block_start[:, None] >= pends[None, :]).astype(jnp.int32), axis=1), N_EXPERTS - 1)
    n_used = (pends[-1] // blk).astype(jnp.int32).reshape(1)
    block_expert = block_expert[jnp.minimum(jnp.arange(n_blocks), jnp.maximum(n_used[0] - 1, 0))]
    next_expert_block = pends[block_expert] // blk
    return (e0, e1, rank0, rank1, pstarts), block_expert, next_expert_block, n_used, n_blocks


def kernel(x, mem, positions, mix_norm_g, w_in, ret_norm_g, lru_conv_w, lru_conv_b, lru_w_a, lru_b_a, lru_w_i, lru_b_i, lru_lambda, lru_norm_g, w_out, xattn_norm_g, mem_norm_g, xattn_wq, xattn_wk, xattn_wv, xattn_wo, moe_norm_g, router_group_w, router_group_b, router_expert_w, router_expert_b, expert_w_gate, expert_w_up, expert_w_down, final_norm_g):
    b, s, d = x.shape
    depth = w_in.shape[0]
    ret_width = RET_HEADS * RET_HEAD_DIM
    lru_width = lru_conv_w.shape[-1]
    assert ret_width == lru_width and ret_width + lru_width == d
    inv_freq = ROPE_BASE ** (-jnp.arange(0, RET_HEAD_DIM, 2, dtype=F32) / RET_HEAD_DIM)
    lg = jnp.log1p(-jnp.exp2(-5.0 - jnp.arange(RET_HEADS, dtype=F32)))
    lg_rows = jnp.broadcast_to(lg[:, None, None], (RET_HEADS, 1, RET_HEAD_DIM))
    blk = min(MOE_BLK, s)
    outs = []
    for bi in range(b):
        xcur = x[bi]
        cos, sin = rope_tables(positions[bi].astype(F32), inv_freq)
        for l in range(depth):
            h = normcast(xcur, mix_norm_g[l], BF16, NORM_ROWS)
            proj = matmul_streamed([h], w_in[l], F32)
            ret = retention(proj, cos, sin, lg_rows, ret_norm_g[l])
            lru = rg_lru(proj, 4 * ret_width // lru_width, 4 * ret_width // lru_width + 1,
                         lru_conv_w[l], lru_conv_b[l], lru_w_a[l], lru_b_a[l], lru_w_i[l],
                         lru_b_i[l], lru_lambda[l], lru_norm_g[l])
            xcur = matmul_streamed([ret, lru], w_out[l], F32, res=xcur, tm=MM_STREAM_TM // 2)
            memn = normcast(mem[bi], mem_norm_g[l], BF16, NORM_ROWS)
            kk = matmul([memn], xattn_wk[l], BF16)
            vv = matmul([memn], xattn_wv[l], BF16)
            wqk, vo = xattn_fold(kk, vv, xattn_wq[l], xattn_wo[l])
            wr = _route_lanes(router_group_w[l], router_expert_w[l])
            br = _route_lanes(router_group_b[l][None], router_expert_b[l][None])
            xcur, hn, route, counts = xattn_route(xcur, xattn_norm_g[l], wqk, vo, moe_norm_g[l],
                                                  wr, br)
            slots, block_expert, next_block, n_used, n_blocks = _moe_layout(route, counts, blk)
            xs = moe_gather(hn, slots, n_used, n_blocks, blk)
            y = moe_experts(xs, block_expert, next_block, n_used, expert_w_gate[l], expert_w_up[l],
                            expert_w_down[l], blk=blk)
            xcur = combine(xcur, route, y, slots, final_norm_g, final_norm=l == depth - 1)
        outs.append(xcur)
    return outs[0][None] if b == 1 else jnp.stack(outs, axis=0)
```
